```python
import math
import jax, jax.numpy as jnp
from jax import lax
import numpy as np

D_MODEL = 1024
BATCH = 16
SEQ = 2048
DEPTH = 2

D_MIX = D_MODEL
D_A = D_MIX // 2
D_B = D_MIX - D_A
G_A = 8
DG_A = D_A // G_A
H_B = 8
DH_B = D_B // H_B
CHUNK = 128
Q_BLOCK = 128
PROJ_COLS = 2 * D_A + 3 * D_B
PEER_HEADS = 8
PEER_DK = 128
PEER_HALF = PEER_DK // 2
N_KEYS = 128
N_EXPERTS = N_KEYS * N_KEYS
PEER_TOPK = 16
PEER_TOK_BLOCK = 128
EPS = 1e-6

kernel_name = "hybrid_gmlp_stickbreak_peer_adaln"


def rms_norm(x, g):
    xf = x.astype(jnp.float32)
    y = xf * lax.rsqrt(jnp.mean(xf * xf, axis=-1, keepdims=True) + EPS)
    return (y * g.astype(jnp.float32)).astype(x.dtype)


def group_rms_norm(x, g, n_groups):
    shp = x.shape
    xg = x.reshape(shp[:-1] + (n_groups, shp[-1] // n_groups))
    xf = xg.astype(jnp.float32)
    y = xf * lax.rsqrt(jnp.mean(xf * xf, axis=-1, keepdims=True) + EPS)
    y = y.reshape(shp) * g.astype(jnp.float32)
    return y.astype(x.dtype)


def chunked_gmlp(u, v, w_s, b_s):
    B, T, _ = v.shape
    nc = T // CHUNK
    vg = v.reshape(B, nc, CHUNK, G_A, DG_A).astype(jnp.float32)
    mu = jnp.mean(vg, axis=-1, keepdims=True)
    var = jnp.mean(jnp.square(vg - mu), axis=-1, keepdims=True)
    vn = ((vg - mu) * lax.rsqrt(var + EPS)).astype(v.dtype)
    w_causal = jnp.tril(w_s)
    s = jnp.einsum("gts,bnsgd->bntgd", w_causal, vn)
    s = s + jnp.transpose(b_s)[None, None, :, :, None]
    return u * s.reshape(B, T, D_A)


def stick_breaking_attention(q, k, v):
    B, H, T, d = q.shape
    scale = 1.0 / math.sqrt(d)
    outs = []
    for i in range(T // Q_BLOCK):
        t0 = i * Q_BLOCK
        kend = t0 + Q_BLOCK
        qb = q[:, :, t0:kend]
        kb = k[:, :, :kend]
        vb = v[:, :, :kend]
        z = jnp.einsum("bhtd,bhsd->bhts", qb, kb).astype(jnp.float32) * scale
        t_idx = t0 + jnp.arange(Q_BLOCK)[:, None]
        s_idx = jnp.arange(kend)[None, :]
        mask = s_idx < t_idx
        log1m = jnp.where(mask, jax.nn.log_sigmoid(-z), 0.0)
        tail = lax.cumsum(log1m, axis=3, reverse=True) - log1m
        a = jnp.where(mask, jnp.exp(jax.nn.log_sigmoid(z) + tail), 0.0)
        outs.append(jnp.einsum("bhts,bhsd->bhtd", a.astype(v.dtype), vb))
    return jnp.concatenate(outs, axis=2)


def peer_layer(h, w_q, k1, k2, u_tab, v_tab):
    B, T, D = h.shape
    q = jnp.einsum("btd,dk->btk", h, w_q).reshape(B, T, PEER_HEADS, PEER_DK)
    q1, q2 = q[..., :PEER_HALF], q[..., PEER_HALF:]
    s1 = jnp.einsum("bthd,nd->bthn", q1, k1)
    s2 = jnp.einsum("bthd,nd->bthn", q2, k2)
    v1, i1 = lax.top_k(s1, PEER_TOPK)
    v2, i2 = lax.top_k(s2, PEER_TOPK)
    cand = (v1[..., :, None] + v2[..., None, :]).reshape(B, T, PEER_HEADS, PEER_TOPK * PEER_TOPK)
    cidx = (i1[..., :, None] * N_KEYS + i2[..., None, :]).reshape(B, T, PEER_HEADS, PEER_TOPK * PEER_TOPK)
    sc, pos = lax.top_k(cand, PEER_TOPK)
    eidx = jnp.take_along_axis(cidx, pos, axis=-1)
    g = jax.nn.softmax(sc.astype(jnp.float32), axis=-1).astype(h.dtype)
    n_blk = (B * T) // PEER_TOK_BLOCK
    HK = PEER_HEADS * PEER_TOPK
    h_blk = h.reshape(n_blk, PEER_TOK_BLOCK, D)
    idx_blk = eidx.reshape(n_blk, PEER_TOK_BLOCK, HK)
    g_blk = g.reshape(n_blk, PEER_TOK_BLOCK, HK)

    def eval_block(args):
        hb, ib, gb = args
        ue = jnp.take(u_tab, ib, axis=0)
        act = jax.nn.gelu(jnp.einsum("tkd,td->tk", ue, hb))
        ve = jnp.take(v_tab, ib, axis=0)
        return jnp.einsum("tk,tkd->td", gb * act, ve)

    y = lax.map(eval_block, (h_blk, idx_blk, g_blk))
    return y.reshape(B, T, D)


def setup_inputs(seed: int = 0) -> dict:
    key = jax.random.key(seed)
    ks = jax.random.split(key, 20)
    f = jnp.float32
    L = DEPTH
    x = jax.random.normal(ks[0], (BATCH, SEQ, D_MODEL), f)
    c = jax.random.normal(ks[1], (BATCH, D_MODEL), f)
    ada_w = jax.random.normal(ks[2], (L, D_MODEL, 6 * D_MODEL), f) * (0.5 * D_MODEL ** -0.5)
    ada_b = jax.random.normal(ks[3], (L, 6 * D_MODEL), f) * 0.02
    norm1_g = 1.0 + 0.02 * jax.random.normal(ks[4], (L, D_MODEL), f)
    norm2_g = 1.0 + 0.02 * jax.random.normal(ks[5], (L, D_MODEL), f)
    w_in = jax.random.normal(ks[6], (L, D_MODEL, PROJ_COLS), f) * D_MODEL ** -0.5
    sgu_w = jax.random.normal(ks[7], (L, G_A, CHUNK, CHUNK), f) * CHUNK ** -0.5
    sgu_b = 1.0 + 0.02 * jax.random.normal(ks[8], (L, G_A, CHUNK), f)
    out_norm_a = 1.0 + 0.02 * jax.random.normal(ks[9], (L, D_A), f)
    out_norm_b = 1.0 + 0.02 * jax.random.normal(ks[10], (L, D_B), f)
    w_out = jax.random.normal(ks[11], (L, D_MIX, D_MODEL), f) * D_MIX ** -0.5
    peer_wq = jax.random.normal(ks[12], (L, D_MODEL, PEER_HEADS * PEER_DK), f) * D_MODEL ** -0.5
    peer_k1 = jax.random.normal(ks[13], (L, N_KEYS, PEER_HALF), f) * PEER_HALF ** -0.5
    peer_k2 = jax.random.normal(ks[14], (L, N_KEYS, PEER_HALF), f) * PEER_HALF ** -0.5
    peer_u = jax.random.normal(ks[15], (L, N_EXPERTS, D_MODEL), f) * D_MODEL ** -0.5
    peer_v = jax.random.normal(ks[16], (L, N_EXPERTS, D_MODEL), f) * PEER_HEADS ** -0.5
    final_g = 1.0 + 0.02 * jax.random.normal(ks[17], (D_MODEL,), f)
    return {"x": x, "c": c, "ada_w": ada_w, "ada_b": ada_b, "norm1_g": norm1_g,
            "norm2_g": norm2_g, "w_in": w_in, "sgu_w": sgu_w, "sgu_b": sgu_b,
            "out_norm_a": out_norm_a, "out_norm_b": out_norm_b, "w_out": w_out,
            "peer_wq": peer_wq, "peer_k1": peer_k1, "peer_k2": peer_k2,
            "peer_u": peer_u, "peer_v": peer_v, "final_g": final_g}


def reference(x, c, ada_w, ada_b, norm1_g, norm2_g, w_in, sgu_w, sgu_b,
              out_norm_a, out_norm_b, w_out, peer_wq, peer_k1, peer_k2,
              peer_u, peer_v, final_g):
    B, T, D = x.shape
    c_act = jax.nn.silu(c)
    for l in range(DEPTH):
        mod = jnp.einsum("bd,de->be", c_act, ada_w[l]) + ada_b[l]
        sh1, sc1, gt1, sh2, sc2, gt2 = [m[:, None, :] for m in jnp.split(mod, 6, axis=-1)]

        h = rms_norm(x, norm1_g[l]) * (1.0 + sc1) + sh1
        proj = jnp.einsum("btd,dc->btc", h, w_in[l])
        u_a = jax.nn.gelu(proj[..., :D_A])
        v_a = jax.nn.gelu(proj[..., D_A:2 * D_A])
        q_b = proj[..., 2 * D_A:2 * D_A + D_B]
        k_b = proj[..., 2 * D_A + D_B:2 * D_A + 2 * D_B]
        v_b = proj[..., 2 * D_A + 2 * D_B:]
        y_a = chunked_gmlp(u_a, v_a, sgu_w[l], sgu_b[l])
        to_heads = lambda z: z.reshape(B, T, H_B, DH_B).transpose(0, 2, 1, 3)
        y_b = stick_breaking_attention(to_heads(q_b), to_heads(k_b), to_heads(v_b))
        y_b = y_b.transpose(0, 2, 1, 3).reshape(B, T, D_B)
        y_mix = jnp.concatenate([group_rms_norm(y_a, out_norm_a[l], G_A),
                                 group_rms_norm(y_b, out_norm_b[l], H_B)], axis=-1)
        x = x + gt1 * jnp.einsum("btc,cd->btd", y_mix, w_out[l])

        h2 = rms_norm(x, norm2_g[l]) * (1.0 + sc2) + sh2
        x = x + gt2 * peer_layer(h2, peer_wq[l], peer_k1[l], peer_k2[l], peer_u[l], peer_v[l])
    return rms_norm(x, final_g)
```

```python
import functools
import math

import numpy as np
import jax
import jax.numpy as jnp
from jax import lax
from jax.experimental import pallas as pl
from jax.experimental.pallas import tpu as pltpu
from jax.experimental.pallas import tpu_sc as plsc

F32 = jnp.float32
BF16 = jnp.bfloat16
HIGHEST = lax.Precision.HIGHEST

D_MODEL = 1024
D_A = 512
D_B = 512
GROUP = 64
CHUNK = 128
N_KEYS = 128
PEER_HEADS = 8
PEER_TOPK = 16
HK = PEER_HEADS * PEER_TOPK
EPS = 1e-6
LANES = 128
VMEM_LIMIT = 48 * 1024 * 1024

_STAIR = [(a, b) for a in range(PEER_TOPK) for b in range(PEER_TOPK) if (a + 1) * (b + 1) <= PEER_TOPK]
N_STAIR = len(_STAIR)


def _cparams(sem):
    return pltpu.CompilerParams(dimension_semantics=sem, vmem_limit_bytes=VMEM_LIMIT)


def _dot(a, b):
    return jnp.dot(a, b, preferred_element_type=F32)


def _block_diag_mean(n, group):
    i = np.arange(n)
    return jnp.asarray((i[:, None] // group == i[None, :] // group).astype(np.float32) / group, BF16)


def _ada_kernel(c_ref, w_ref, b_ref, o_ref):
    c = c_ref[...]
    ca = c * jax.nn.sigmoid(c)
    o_ref[0] = jnp.dot(ca, w_ref[0], precision=HIGHEST, preferred_element_type=F32) + b_ref[0]


def ada_mod(c, ada_w, ada_b):
    L, Dm, E = ada_w.shape
    Bc = c.shape[0]
    tn = 1536
    return pl.pallas_call(
        _ada_kernel,
        grid=(L, E // tn),
        in_specs=[pl.BlockSpec((Bc, Dm), lambda l, j: (0, 0)),
                  pl.BlockSpec((1, Dm, tn), lambda l, j: (l, 0, j)),
                  pl.BlockSpec((1, 1, tn), lambda l, j: (l, 0, j))],
        out_specs=pl.BlockSpec((1, Bc, tn), lambda l, j: (l, 0, j)),
        out_shape=jax.ShapeDtypeStruct((L, Bc, E), F32),
        compiler_params=_cparams(("parallel", "parallel")),
        name="ada_mod",
    )(c, ada_w, ada_b.reshape(L, 1, E))


def _mod_spec(tm, T, j):
    return pl.BlockSpec((1, 1, D_MODEL), lambda i: (((i * tm) // T) * 6 + j, 0, 0))


def _rms_mod(x, g, sc, sh):
    ms = jnp.mean(x * x, axis=-1, keepdims=True)
    return (x * lax.rsqrt(ms + EPS) * g) * (1.0 + sc) + sh


def _proj_kernel(has_y, *refs):
    if has_y:
        x_ref, y_ref, gt_ref, g_ref, sc_ref, sh_ref, w_ref, xo_ref, ua_ref, va_ref, q_ref, k_ref, v_ref = refs
        x = x_ref[...] + gt_ref[0] * y_ref[...]
        xo_ref[...] = x
    else:
        x_ref, g_ref, sc_ref, sh_ref, w_ref, ua_ref, va_ref, q_ref, k_ref, v_ref = refs
        x = x_ref[...]
    h = _rms_mod(x, g_ref[...], sc_ref[0], sh_ref[0]).astype(BF16)
    outs = (ua_ref, va_ref, q_ref, k_ref, v_ref)
    for j, o_ref in enumerate(outs):
        p = _dot(h, w_ref[:, j * D_A:(j + 1) * D_A])
        if j < 2:
            p = jax.nn.gelu(p)
        o_ref[...] = p.astype(o_ref.dtype)


def proj_call(x, y, mod_prev, mod, g1, w_in_bf, T, tm=512):
    N = x.shape[0]
    has_y = y is not None
    row = pl.BlockSpec((tm, D_MODEL), lambda i: (i, 0))
    half = pl.BlockSpec((tm, D_A), lambda i: (i, 0))
    in_specs = [row]
    args = [x]
    if has_y:
        in_specs += [row, _mod_spec(tm, T, 5)]
        args += [y, mod_prev]
    in_specs += [pl.BlockSpec((1, D_MODEL), lambda i: (0, 0)), _mod_spec(tm, T, 1), _mod_spec(tm, T, 0),
                 pl.BlockSpec(w_in_bf.shape, lambda i: (0, 0))]
    args += [g1, mod, mod, w_in_bf]
    out_specs = [half] * 5
    out_shape = [jax.ShapeDtypeStruct((N, D_A), F32)] * 2 + [jax.ShapeDtypeStruct((N, D_A), BF16)] * 3
    if has_y:
        out_specs = [row] + out_specs
        out_shape = [jax.ShapeDtypeStruct((N, D_MODEL), F32)] + out_shape
    return pl.pallas_call(
        functools.partial(_proj_kernel, has_y),
        grid=(N // tm,), in_specs=in_specs, out_specs=out_specs, out_shape=out_shape,
        compiler_params=_cparams(("parallel",)), name="proj",
    )(*args)


def _gmlp_kernel(ua_ref, va_ref, w_ref, bias_ref, a_ref, ga_ref, o_ref):
    A = a_ref[...]
    row = lax.broadcasted_iota(jnp.int32, (CHUNK, CHUNK), 0)
    col = lax.broadcasted_iota(jnp.int32, (CHUNK, CHUNK), 1)
    causal = row >= col
    first_group = col < GROUP
    ws = [jnp.where(causal, w_ref[g], 0.0).astype(BF16) for g in range(D_A // GROUP)]
    bias = bias_ref[...]
    ga = ga_ref[...]
    for c in range(ua_ref.shape[0] // CHUNK):
        rows = slice(c * CHUNK, (c + 1) * CHUNK)
        v = va_ref[rows, :]
        v_hi = v.astype(BF16)
        v_lo = (v - v_hi.astype(F32)).astype(BF16)
        d = v - (_dot(v_hi, A) + _dot(v_lo, A))
        var = _dot((d * d).astype(BF16), A)
        vn = (d * lax.rsqrt(var + EPS)).astype(BF16)
        parts = []
        for p in range(D_A // LANES):
            vp = vn[:, p * LANES:(p + 1) * LANES]
            parts.append(jnp.where(first_group, _dot(ws[2 * p], vp), _dot(ws[2 * p + 1], vp)))
        s = jnp.concatenate(parts, axis=1) + bias
        y = ua_ref[rows, :] * s
        ms = _dot((y * y).astype(BF16), A)
        o_ref[rows, :] = (y * lax.rsqrt(ms + EPS) * ga).astype(BF16)


def gmlp_call(ua, va, sgu_w, bias_full, ga, tm=512):
    N = ua.shape[0]
    half = pl.BlockSpec((tm, D_A), lambda i: (i, 0))
    return pl.pallas_call(
        _gmlp_kernel, grid=(N // tm,),
        in_specs=[half, half,
                  pl.BlockSpec(sgu_w.shape, lambda i: (0, 0, 0)),
                  pl.BlockSpec((CHUNK, D_A), lambda i: (0, 0)),
                  pl.BlockSpec((D_A, D_A), lambda i: (0, 0)),
                  pl.BlockSpec((1, D_A), lambda i: (0, 0))],
        out_specs=half, out_shape=jax.ShapeDtypeStruct((N, D_A), BF16),
        compiler_params=_cparams(("parallel",)), name="gmlp",
    )(ua, va, sgu_w, bias_full, _block_diag_mean(D_A, GROUP), ga)


def _attn_kernel(q_ref, k_ref, v_ref, m_ref, ones_ref, a_ref, gb_ref, o_ref):
    i = pl.program_id(2)
    q = q_ref[...]
    lane = lax.broadcasted_iota(jnp.int32, (CHUNK, LANES), 1)
    row = lax.broadcasted_iota(jnp.int32, (CHUNK, LANES), 0)
    head_lanes = (lane < GROUP, lane >= GROUP)
    qh = [jnp.where(hl, q, jnp.zeros_like(q)) for hl in head_lanes]
    M = m_ref[...]
    ONES = ones_ref[...]
    scale = 1.0 / math.sqrt(GROUP)
    zeros = jnp.zeros((CHUNK, LANES), F32)

    def body(jj, carry):
        acc, c0, c1 = carry
        cs = [c0, c1]
        j = i - jj
        start = pl.multiple_of(j * CHUNK, CHUNK)
        ks = k_ref[pl.ds(start, CHUNK), :]
        vs = v_ref[pl.ds(start, CHUNK), :]
        mask = (j * CHUNK + lane) < (i * CHUNK + row)
        for h in range(2):
            z = lax.dot_general(qh[h], ks, (((1,), (1,)), ((), ())), preferred_element_type=F32) * scale
            lsz = jnp.minimum(z, 0.0) - jnp.log1p(jnp.exp(-jnp.abs(z)))
            L = jnp.where(mask, lsz - z, 0.0)
            L_hi = L.astype(BF16)
            L_lo = (L - L_hi.astype(F32)).astype(BF16)
            tail = _dot(L_hi, M) + _dot(L_lo, M) + cs[h]
            a = jnp.where(mask, jnp.exp(lsz + tail), 0.0).astype(BF16)
            acc = acc + _dot(a, jnp.where(head_lanes[h], vs, jnp.zeros_like(vs)))
            cs[h] = cs[h] + _dot(L_hi, ONES) + _dot(L_lo, ONES)
        return acc, cs[0], cs[1]

    acc, _, _ = lax.fori_loop(0, i + 1, body, (zeros, zeros, zeros))
    ms = _dot((acc * acc).astype(BF16), a_ref[...])
    o_ref[...] = (acc * lax.rsqrt(ms + EPS) * gb_ref[...]).astype(BF16)


def attn_call(q, k, v, gb, B, T):
    N = q.shape[0]
    nq = T // CHUNK
    npair = D_B // LANES
    i = np.arange(CHUNK)
    M = jnp.asarray((i[:, None] > i[None, :]).astype(np.float32), BF16)
    ONES = jnp.ones((CHUNK, LANES), BF16)
    const = pl.BlockSpec((CHUNK, LANES), lambda b, p, i: (0, 0))
    kv = pl.BlockSpec((T, LANES), lambda b, p, i: (b, p))
    return pl.pallas_call(
        _attn_kernel, grid=(B, npair, nq),
        in_specs=[pl.BlockSpec((CHUNK, LANES), lambda b, p, i: (b * nq + i, p)), kv, kv, const, const, const,
                  pl.BlockSpec((1, LANES), lambda b, p, i: (0, p))],
        out_specs=pl.BlockSpec((CHUNK, LANES), lambda b, p, i: (b * nq + i, p)),
        out_shape=jax.ShapeDtypeStruct((N, D_B), BF16),
        compiler_params=_cparams(("parallel", "parallel", "arbitrary")), name="stickbreak_attn",
    )(q, k, v, M, ONES, _block_diag_mean(LANES, GROUP), gb)


def _out_kernel(ya_ref, yb_ref, x_ref, w_ref, gt_ref, g2_ref, sc_ref, sh_ref, xo_ref, h2_ref):
    o = _dot(ya_ref[...], w_ref[:D_A, :]) + _dot(yb_ref[...], w_ref[D_A:, :])
    x = x_ref[...] + gt_ref[0] * o
    xo_ref[...] = x
    h2_ref[...] = _rms_mod(x, g2_ref[...], sc_ref[0], sh_ref[0])


def out_call(ya, yb, x, w_out_bf, mod, g2, T, tm=512):
    N = x.shape[0]
    row = pl.BlockSpec((tm, D_MODEL), lambda i: (i, 0))
    half = pl.BlockSpec((tm, D_A), lambda i: (i, 0))
    return pl.pallas_call(
        _out_kernel, grid=(N // tm,),
        in_specs=[half, half, row, pl.BlockSpec(w_out_bf.shape, lambda i: (0, 0)), _mod_spec(tm, T, 2),
                  pl.BlockSpec((1, D_MODEL), lambda i: (0, 0)), _mod_spec(tm, T, 4), _mod_spec(tm, T, 3)],
        out_specs=[row, row], out_shape=[jax.ShapeDtypeStruct((N, D_MODEL), F32)] * 2,
        compiler_params=_cparams(("parallel",)), name="out_proj",
    )(ya, yb, x, w_out_bf, mod, g2, mod, mod)


def _topk_rows(s, lane_f, out_lane0, vals, idxs):
    neg = jnp.float32(-jnp.inf)

    def body(r, carry):
        s, vals, idxs = carry
        m = jnp.max(s, axis=-1, keepdims=True)
        pos = jnp.min(jnp.where(s == m, lane_f, float(LANES)), axis=-1, keepdims=True)
        dst = lane_f == (out_lane0 + r).astype(F32)
        return jnp.where(lane_f == pos, neg, s), jnp.where(dst, m, vals), jnp.where(dst, pos, idxs)

    _, vals, idxs = lax.fori_loop(0, PEER_TOPK, body, (s, vals, idxs))
    return vals, idxs


def _peerq_kernel(h2_ref, wq_ref, k1_ref, k2_ref, sa_ref, sb_ref, idx_ref, g_ref, q_scr):
    h = pl.program_id(1)
    tm = h2_ref.shape[0]

    @pl.when(h == 0)
    def _():
        q_scr[...] = _dot(h2_ref[...].astype(BF16), wq_ref[...]).astype(BF16)
        idx_ref[...] = jnp.zeros_like(idx_ref)
        g_ref[...] = jnp.zeros_like(g_ref)

    qh = q_scr[:, pl.ds(pl.multiple_of(h * LANES, LANES), LANES)]
    lane_f = lax.broadcasted_iota(jnp.int32, (tm, LANES), 1).astype(F32)
    zeros = jnp.zeros((tm, LANES), F32)
    zero_i = jnp.int32(0)
    v1, i1 = _topk_rows(_dot(qh, k1_ref[...]), lane_f, zero_i, zeros, zeros)
    v2, i2 = _topk_rows(_dot(qh, k2_ref[...]), lane_f, zero_i, zeros, zeros)
    sa = sa_ref[...]
    sb = sb_ref[...]
    cand = (jnp.dot(v1, sa, precision=HIGHEST, preferred_element_type=F32)
            + jnp.dot(v2, sb, precision=HIGHEST, preferred_element_type=F32))
    cand = jnp.where(lane_f < float(N_STAIR), cand, -jnp.inf)
    ecand = (jnp.dot(i1, sa, precision=HIGHEST, preferred_element_type=F32) * float(N_KEYS)
             + jnp.dot(i2, sb, precision=HIGHEST, preferred_element_type=F32))
    neg = jnp.float32(-jnp.inf)
    base = (h * PEER_TOPK).astype(F32)

    def body(r, carry):
        cand, sc, ex = carry
        m = jnp.max(cand, axis=-1, keepdims=True)
        pos = jnp.min(jnp.where(cand == m, lane_f, float(LANES)), axis=-1, keepdims=True)
        hit = lane_f == pos
        e = jnp.max(jnp.where(hit, ecand, -1.0), axis=-1, keepdims=True)
        dst = lane_f == base + r.astype(F32)
        return jnp.where(hit, neg, cand), jnp.where(dst, m, sc), jnp.where(dst, e, ex)

    _, sc, ex = lax.fori_loop(0, PEER_TOPK, body, (cand, jnp.full((tm, LANES), neg), zeros))
    mine = (lane_f >= base) & (lane_f < base + float(PEER_TOPK))
    mx = jnp.max(sc, axis=-1, keepdims=True)
    p = jnp.where(mine, jnp.exp(sc - mx), 0.0)
    gate = p / jnp.sum(p, axis=-1, keepdims=True)
    idx_ref[...] = jnp.where(mine, ex.astype(jnp.int32), idx_ref[...])
    g_ref[...] = jnp.where(mine, gate, g_ref[...])


def peerq_call(h2, wq_bf, k1, k2, tm=256):
    N = h2.shape[0]
    half = N_KEYS // 2
    k1p = jnp.concatenate([k1.T, jnp.zeros((half, N_KEYS), F32)], axis=0).astype(BF16)
    k2p = jnp.concatenate([jnp.zeros((half, N_KEYS), F32), k2.T], axis=0).astype(BF16)
    sa = np.zeros((LANES, LANES), np.float32)
    sb = np.zeros((LANES, LANES), np.float32)
    for c, (a, b) in enumerate(_STAIR):
        sa[a, c] = 1.0
        sb[b, c] = 1.0
    const = lambda shape: pl.BlockSpec(shape, lambda i, h: (0, 0))
    row = pl.BlockSpec((tm, HK), lambda i, h: (i, 0))
    return pl.pallas_call(
        _peerq_kernel, grid=(N // tm, PEER_HEADS),
        in_specs=[pl.BlockSpec((tm, D_MODEL), lambda i, h: (i, 0)), const(wq_bf.shape), const((LANES, N_KEYS)),
                  const((LANES, N_KEYS)), const((LANES, LANES)), const((LANES, LANES))],
        out_specs=[row, row],
        out_shape=[jax.ShapeDtypeStruct((N, HK), jnp.int32), jax.ShapeDtypeStruct((N, HK), F32)],
        scratch_shapes=[pltpu.VMEM((tm, D_MODEL), BF16)],
        compiler_params=_cparams(("parallel", "arbitrary")), name="peer_retrieve",
    )(h2, wq_bf, k1p, k2p, jnp.asarray(sa), jnp.asarray(sb))


SC_LANES = 16
SC_CORES = 2
SC_WORKERS = 32
ROWS_PER_GATHER = 32
TOK_BLOCK = 8
N_DCHUNK = D_MODEL // SC_LANES
N_GATHER = HK // ROWS_PER_GATHER


def _sc_gelu(x):
    y = math.sqrt(2.0 / math.pi) * (x + 0.044715 * (x * x * x))
    t = 1.0 - 2.0 / (jnp.exp(2.0 * y) + 1.0)
    return 0.5 * x * (1.0 + t)


def _peer_eval_body(per_w, h_hbm, idx_hbm, g_hbm, u_hbm, v_hbm, o_hbm,
                    idx_blk, g_blk, h_blk, out_blk, rows0, rows1, w_v, sem0, sem1):
    wid = lax.axis_index("s") * SC_CORES + lax.axis_index("c")
    rows = (rows0, rows1)
    sems = (sem0, sem1)
    lane = lax.iota(jnp.int32, SC_LANES)
    zeros = jnp.zeros((SC_LANES,), F32)

    def gather(table, i, j, buf):
        return pltpu.make_async_copy(table.at[idx_blk.at[i, j]], rows[buf], sems[buf])

    def u_compute(i, n, rbuf):
        for gq in range(ROWS_PER_GATHER // SC_LANES):
            def row_body(r, actv):
                def c_body(c, acc):
                    sl = pl.ds(pl.multiple_of(c * SC_LANES, SC_LANES), SC_LANES)
                    return acc + rbuf[gq * SC_LANES + r, sl] * h_blk[i, sl]
                acc = lax.fori_loop(0, N_DCHUNK, c_body, zeros)
                return jnp.where(lane == r, jnp.sum(acc), actv)
            actv = lax.fori_loop(0, SC_LANES, row_body, zeros)
            sl = pl.ds(n * ROWS_PER_GATHER + gq * SC_LANES, SC_LANES)
            w_v[sl] = g_blk[i, sl] * _sc_gelu(actv)

    def v_compute(i, m, rbuf):
        if m == 0:
            @pl.loop(0, N_DCHUNK)
            def _(c):
                out_blk[i, pl.ds(pl.multiple_of(c * SC_LANES, SC_LANES), SC_LANES)] = zeros

        @pl.loop(0, ROWS_PER_GATHER)
        def _(r):
            wk = plsc.load_gather(w_v, [jnp.full((SC_LANES,), m * ROWS_PER_GATHER, jnp.int32) + r])

            @pl.loop(0, N_DCHUNK)
            def _(c):
                sl = pl.ds(pl.multiple_of(c * SC_LANES, SC_LANES), SC_LANES)
                plsc.addupdate(out_blk.at[i, sl], wk * rbuf[r, sl])

    @pl.loop(0, per_w // TOK_BLOCK)
    def _(bi):
        t0 = wid * per_w + bi * TOK_BLOCK
        pltpu.sync_copy(idx_hbm.at[pl.ds(t0, TOK_BLOCK)], idx_blk)
        pltpu.sync_copy(g_hbm.at[pl.ds(t0, TOK_BLOCK)], g_blk)
        pltpu.sync_copy(h_hbm.at[pl.ds(t0, TOK_BLOCK)], h_blk)
        gather(u_hbm, 0, 0, 0).start()

        @pl.loop(0, TOK_BLOCK)
        def _(i):
            for n in range(2 * N_GATHER):
                buf = n % 2
                table = u_hbm if n < N_GATHER else v_hbm
                if n + 1 < 2 * N_GATHER:
                    nxt = u_hbm if n + 1 < N_GATHER else v_hbm
                    gather(nxt, i, (n + 1) % N_GATHER, 1 - buf).start()
                else:
                    @pl.when(i + 1 < TOK_BLOCK)
                    def _():
                        gather(u_hbm, i + 1, 0, 1 - buf).start()
                gather(table, i, n % N_GATHER, buf).wait()
                if n < N_GATHER:
                    u_compute(i, n, rows[buf])
                else:
                    v_compute(i, n - N_GATHER, rows[buf])

        pltpu.sync_copy(out_blk, o_hbm.at[pl.ds(t0, TOK_BLOCK)])


def peer_eval_call(h2, idx, g, u_tab, v_tab):
    N = h2.shape[0]
    per_w = N // SC_WORKERS
    assert N % (SC_WORKERS * TOK_BLOCK) == 0
    mesh = plsc.VectorSubcoreMesh(core_axis_name="c", subcore_axis_name="s",
                                  num_cores=SC_CORES, num_subcores=SC_WORKERS // SC_CORES)
    return pl.kernel(
        functools.partial(_peer_eval_body, per_w),
        out_type=jax.ShapeDtypeStruct((N, D_MODEL), F32),
        mesh=mesh,
        scratch_types=[
            pltpu.VMEM((TOK_BLOCK, N_GATHER, ROWS_PER_GATHER), jnp.int32),
            pltpu.VMEM((TOK_BLOCK, HK), F32),
            pltpu.VMEM((TOK_BLOCK, D_MODEL), F32),
            pltpu.VMEM((TOK_BLOCK, D_MODEL), F32),
            pltpu.VMEM((ROWS_PER_GATHER, D_MODEL), F32),
            pltpu.VMEM((ROWS_PER_GATHER, D_MODEL), F32),
            pltpu.VMEM((HK,), F32),
            pltpu.SemaphoreType.DMA,
            pltpu.SemaphoreType.DMA,
        ],
        compiler_params=pltpu.CompilerParams(needs_layout_passes=False),
        name="peer_eval",
    )(h2, idx.reshape(N, N_GATHER, ROWS_PER_GATHER), g, u_tab, v_tab)


def _final_kernel(x_ref, y_ref, gt_ref, g_ref, o_ref):
    x = x_ref[...] + gt_ref[0] * y_ref[...]
    ms = jnp.mean(x * x, axis=-1, keepdims=True)
    o_ref[...] = x * lax.rsqrt(ms + EPS) * g_ref[...]


def final_call(x, y, mod, gf, T, tm=512):
    N = x.shape[0]
    row = pl.BlockSpec((tm, D_MODEL), lambda i: (i, 0))
    return pl.pallas_call(
        _final_kernel, grid=(N // tm,),
        in_specs=[row, row, _mod_spec(tm, T, 5), pl.BlockSpec((1, D_MODEL), lambda i: (0, 0))],
        out_specs=row, out_shape=jax.ShapeDtypeStruct((N, D_MODEL), F32),
        compiler_params=_cparams(("parallel",)), name="final_norm",
    )(x, y, mod, gf)


def kernel(x, c, ada_w, ada_b, norm1_g, norm2_g, w_in, sgu_w, sgu_b, out_norm_a, out_norm_b, w_out,
           peer_wq, peer_k1, peer_k2, peer_u, peer_v, final_g):
    B, T, Dm = x.shape
    L = ada_w.shape[0]
    N = B * T
    mods = ada_mod(c, ada_w, ada_b).reshape(L, B * 6, 1, Dm)
    xf = x.reshape(N, Dm)
    y = None
    for l in range(L):
        mod = mods[l]
        outs = proj_call(xf, y, mods[l - 1] if l else None, mod, norm1_g[l].reshape(1, Dm), w_in[l].astype(BF16), T)
        if l:
            xf, outs = outs[0], outs[1:]
        ua, va, q, k, v = outs
        bias_full = jnp.repeat(sgu_b[l].T, GROUP, axis=1)
        ya = gmlp_call(ua, va, sgu_w[l], bias_full, out_norm_a[l].reshape(1, D_A))
        yb = attn_call(q, k, v, out_norm_b[l].reshape(1, D_B), B, T)
        xf, h2 = out_call(ya, yb, xf, w_out[l].astype(BF16), mod, norm2_g[l].reshape(1, Dm), T)
        idx, gate = peerq_call(h2, peer_wq[l].astype(BF16), peer_k1[l], peer_k2[l])
        y = peer_eval_call(h2, idx, gate, peer_u[l], peer_v[l])
    out = final_call(xf, y, mods[L - 1], final_g.reshape(1, Dm), T)
    return out.reshape(B, T, Dm)
```

```python
import functools
import math

import numpy as np
import jax
import jax.numpy as jnp
from jax import lax
from jax.experimental import pallas as pl
from jax.experimental.pallas import tpu as pltpu
from jax.experimental.pallas import tpu_sc as plsc

F32 = jnp.float32
BF16 = jnp.bfloat16
HIGHEST = lax.Precision.HIGHEST

D_MODEL = 1024
D_A = 512
D_B = 512
GROUP = 64
CHUNK = 128
N_KEYS = 128
PEER_HEADS = 8
PEER_TOPK = 16
HK = PEER_HEADS * PEER_TOPK
EPS = 1e-6
LANES = 128
VMEM_LIMIT = 48 * 1024 * 1024

_STAIR = [(a, b) for a in range(PEER_TOPK) for b in range(PEER_TOPK) if (a + 1) * (b + 1) <= PEER_TOPK]
N_STAIR = len(_STAIR)


def _cparams(sem):
    return pltpu.CompilerParams(dimension_semantics=sem, vmem_limit_bytes=VMEM_LIMIT)


def _dot(a, b):
    return jnp.dot(a, b, preferred_element_type=F32)


def _block_diag_mean(n, group):
    i = np.arange(n)
    return jnp.asarray((i[:, None] // group == i[None, :] // group).astype(np.float32) / group, BF16)


def _ada_kernel(c_ref, w_ref, b_ref, o_ref):
    c = c_ref[...]
    ca = c * jax.nn.sigmoid(c)
    o_ref[0] = jnp.dot(ca, w_ref[0], precision=HIGHEST, preferred_element_type=F32) + b_ref[0]


def ada_mod(c, ada_w, ada_b):
    L, Dm, E = ada_w.shape
    Bc = c.shape[0]
    tn = 1536
    return pl.pallas_call(
        _ada_kernel,
        grid=(L, E // tn),
        in_specs=[pl.BlockSpec((Bc, Dm), lambda l, j: (0, 0)),
                  pl.BlockSpec((1, Dm, tn), lambda l, j: (l, 0, j)),
                  pl.BlockSpec((1, 1, tn), lambda l, j: (l, 0, j))],
        out_specs=pl.BlockSpec((1, Bc, tn), lambda l, j: (l, 0, j)),
        out_shape=jax.ShapeDtypeStruct((L, Bc, E), F32),
        compiler_params=_cparams(("parallel", "parallel")),
        name="ada_mod",
    )(c, ada_w, ada_b.reshape(L, 1, E))


def _mod_spec(tm, T, j):
    return pl.BlockSpec((1, 1, D_MODEL), lambda i: (((i * tm) // T) * 6 + j, 0, 0))


def _rms_mod(x, g, sc, sh):
    ms = jnp.mean(x * x, axis=-1, keepdims=True)
    return (x * lax.rsqrt(ms + EPS) * g) * (1.0 + sc) + sh


def _proj_kernel(has_y, *refs):
    if has_y:
        x_ref, y_ref, gt_ref, g_ref, sc_ref, sh_ref, w_ref, xo_ref, ua_ref, va_ref, q_ref, k_ref, v_ref = refs
        x = x_ref[...] + gt_ref[0] * y_ref[...]
        xo_ref[...] = x
    else:
        x_ref, g_ref, sc_ref, sh_ref, w_ref, ua_ref, va_ref, q_ref, k_ref, v_ref = refs
        x = x_ref[...]
    h = _rms_mod(x, g_ref[...], sc_ref[0], sh_ref[0]).astype(BF16)
    outs = (ua_ref, va_ref, q_ref, k_ref, v_ref)
    for j, o_ref in enumerate(outs):
        p = _dot(h, w_ref[:, j * D_A:(j + 1) * D_A])
        if j < 2:
            p = jax.nn.gelu(p)
        o_ref[...] = p.astype(o_ref.dtype)


def proj_call(x, y, mod_prev, mod, g1, w_in_bf, T, tm=512):
    N = x.shape[0]
    has_y = y is not None
    row = pl.BlockSpec((tm, D_MODEL), lambda i: (i, 0))
    half = pl.BlockSpec((tm, D_A), lambda i: (i, 0))
    in_specs = [row]
    args = [x]
    if has_y:
        in_specs += [row, _mod_spec(tm, T, 5)]
        args += [y, mod_prev]
    in_specs += [pl.BlockSpec((1, D_MODEL), lambda i: (0, 0)), _mod_spec(tm, T, 1), _mod_spec(tm, T, 0),
                 pl.BlockSpec(w_in_bf.shape, lambda i: (0, 0))]
    args += [g1, mod, mod, w_in_bf]
    out_specs = [half] * 5
    out_shape = [jax.ShapeDtypeStruct((N, D_A), F32)] * 2 + [jax.ShapeDtypeStruct((N, D_A), BF16)] * 3
    if has_y:
        out_specs = [row] + out_specs
        out_shape = [jax.ShapeDtypeStruct((N, D_MODEL), F32)] + out_shape
    return pl.pallas_call(
        functools.partial(_proj_kernel, has_y),
        grid=(N // tm,), in_specs=in_specs, out_specs=out_specs, out_shape=out_shape,
        compiler_params=_cparams(("parallel",)), name="proj",
    )(*args)


def _gmlp_kernel(ua_ref, va_ref, w_ref, bias_ref, a_ref, ga_ref, o_ref):
    A = a_ref[...]
    row = lax.broadcasted_iota(jnp.int32, (CHUNK, CHUNK), 0)
    col = lax.broadcasted_iota(jnp.int32, (CHUNK, CHUNK), 1)
    causal = row >= col
    first_group = col < GROUP
    ws = [jnp.where(causal, w_ref[g], 0.0).astype(BF16) for g in range(D_A // GROUP)]
    bias = bias_ref[...]
    ga = ga_ref[...]
    for c in range(ua_ref.shape[0] // CHUNK):
        rows = slice(c * CHUNK, (c + 1) * CHUNK)
        v = va_ref[rows, :]
        v_hi = v.astype(BF16)
        v_lo = (v - v_hi.astype(F32)).astype(BF16)
        d = v - (_dot(v_hi, A) + _dot(v_lo, A))
        var = _dot((d * d).astype(BF16), A)
        vn = (d * lax.rsqrt(var + EPS)).astype(BF16)
        parts = []
        for p in range(D_A // LANES):
            vp = vn[:, p * LANES:(p + 1) * LANES]
            parts.append(jnp.where(first_group, _dot(ws[2 * p], vp), _dot(ws[2 * p + 1], vp)))
        s = jnp.concatenate(parts, axis=1) + bias
        y = ua_ref[rows, :] * s
        ms = _dot((y * y).astype(BF16), A)
        o_ref[rows, :] = (y * lax.rsqrt(ms + EPS) * ga).astype(BF16)


def gmlp_call(ua, va, sgu_w, bias_full, ga, tm=512):
    N = ua.shape[0]
    half = pl.BlockSpec((tm, D_A), lambda i: (i, 0))
    return pl.pallas_call(
        _gmlp_kernel, grid=(N // tm,),
        in_specs=[half, half,
                  pl.BlockSpec(sgu_w.shape, lambda i: (0, 0, 0)),
                  pl.BlockSpec((CHUNK, D_A), lambda i: (0, 0)),
                  pl.BlockSpec((D_A, D_A), lambda i: (0, 0)),
                  pl.BlockSpec((1, D_A), lambda i: (0, 0))],
        out_specs=half, out_shape=jax.ShapeDtypeStruct((N, D_A), BF16),
        compiler_params=_cparams(("parallel",)), name="gmlp",
    )(ua, va, sgu_w, bias_full, _block_diag_mean(D_A, GROUP), ga)


EXP_UNDERFLOW = -104.0


def _attn_kernel(q_ref, k_ref, v_ref, m_ref, ones_ref, a_ref, gb_ref, o_ref, acc_ref, carry_ref):
    i = pl.program_id(1)
    n_pair = D_B // LANES
    lane = lax.broadcasted_iota(jnp.int32, (CHUNK, LANES), 1)
    row = lax.broadcasted_iota(jnp.int32, (CHUNK, LANES), 0)
    head_lanes = (lane < GROUP, lane >= GROUP)
    M = m_ref[...]
    ONES = ones_ref[...]
    scale = 1.0 / math.sqrt(GROUP)
    acc_ref[...] = jnp.zeros_like(acc_ref)
    carry_ref[...] = jnp.zeros_like(carry_ref)

    def cond(state):
        jj, cmax = state
        return jnp.logical_and(jj <= i, cmax > EXP_UNDERFLOW)

    def body(state):
        jj, _ = state
        j = i - jj
        start = pl.multiple_of(j * CHUNK, CHUNK)
        mask = (j * CHUNK + lane) < (i * CHUNK + row)
        cmax = jnp.full((CHUNK, LANES), -jnp.inf, F32)
        for p in range(n_pair):
            cols = slice(p * LANES, (p + 1) * LANES)
            q = q_ref[:, cols]
            ks = k_ref[pl.ds(start, CHUNK), cols]
            vs = v_ref[pl.ds(start, CHUNK), cols]
            acc = acc_ref[:, cols]
            for h in range(2):
                qh = jnp.where(head_lanes[h], q, jnp.zeros_like(q))
                z = lax.dot_general(qh, ks, (((1,), (1,)), ((), ())), preferred_element_type=F32) * scale
                lsz = jnp.minimum(z, 0.0) - jnp.log1p(jnp.exp(-jnp.abs(z)))
                L = jnp.where(mask, lsz - z, 0.0)
                L_hi = L.astype(BF16)
                L_lo = (L - L_hi.astype(F32)).astype(BF16)
                c = carry_ref[2 * p + h]
                tail = _dot(L_hi, M) + _dot(L_lo, M) + c
                a = jnp.where(mask, jnp.exp(lsz + tail), 0.0).astype(BF16)
                acc = acc + _dot(a, jnp.where(head_lanes[h], vs, jnp.zeros_like(vs)))
                c = c + _dot(L_hi, ONES) + _dot(L_lo, ONES)
                carry_ref[2 * p + h] = c
                cmax = jnp.maximum(cmax, c)
            acc_ref[:, cols] = acc
        return jj + 1, jnp.max(cmax)

    lax.while_loop(cond, body, (jnp.int32(0), jnp.float32(0.0)))
    for p in range(n_pair):
        cols = slice(p * LANES, (p + 1) * LANES)
        acc = acc_ref[:, cols]
        ms = _dot((acc * acc).astype(BF16), a_ref[...])
        o_ref[:, cols] = (acc * lax.rsqrt(ms + EPS) * gb_ref[:, cols]).astype(BF16)


def attn_call(q, k, v, gb, B, T):
    N = q.shape[0]
    nq = T // CHUNK
    i = np.arange(CHUNK)
    M = jnp.asarray((i[:, None] > i[None, :]).astype(np.float32), BF16)
    ONES = jnp.ones((CHUNK, LANES), BF16)
    const = pl.BlockSpec((CHUNK, LANES), lambda b, i: (0, 0))
    kv = pl.BlockSpec((T, D_B), lambda b, i: (b, 0))
    qo = pl.BlockSpec((CHUNK, D_B), lambda b, i: (b * nq + i, 0))
    return pl.pallas_call(
        _attn_kernel, grid=(B, nq),
        in_specs=[qo, kv, kv, const, const, const, pl.BlockSpec((1, D_B), lambda b, i: (0, 0))],
        out_specs=qo,
        out_shape=jax.ShapeDtypeStruct((N, D_B), BF16),
        scratch_shapes=[pltpu.VMEM((CHUNK, D_B), F32), pltpu.VMEM((D_B // GROUP, CHUNK, LANES), F32)],
        compiler_params=_cparams(("parallel", "arbitrary")), name="stickbreak_attn",
    )(q, k, v, M, ONES, _block_diag_mean(LANES, GROUP), gb)


def _out_kernel(ya_ref, yb_ref, x_ref, w_ref, gt_ref, g2_ref, sc_ref, sh_ref, xo_ref, h2_ref):
    o = _dot(ya_ref[...], w_ref[:D_A, :]) + _dot(yb_ref[...], w_ref[D_A:, :])
    x = x_ref[...] + gt_ref[0] * o
    xo_ref[...] = x
    h2_ref[...] = _rms_mod(x, g2_ref[...], sc_ref[0], sh_ref[0])


def out_call(ya, yb, x, w_out_bf, mod, g2, T, tm=512):
    N = x.shape[0]
    row = pl.BlockSpec((tm, D_MODEL), lambda i: (i, 0))
    half = pl.BlockSpec((tm, D_A), lambda i: (i, 0))
    return pl.pallas_call(
        _out_kernel, grid=(N // tm,),
        in_specs=[half, half, row, pl.BlockSpec(w_out_bf.shape, lambda i: (0, 0)), _mod_spec(tm, T, 2),
                  pl.BlockSpec((1, D_MODEL), lambda i: (0, 0)), _mod_spec(tm, T, 4), _mod_spec(tm, T, 3)],
        out_specs=[row, row], out_shape=[jax.ShapeDtypeStruct((N, D_MODEL), F32)] * 2,
        compiler_params=_cparams(("parallel",)), name="out_proj",
    )(ya, yb, x, w_out_bf, mod, g2, mod, mod)


N_CAND_ROWS = 64


def _topk_cols(s, payload=None):
    n_rows, tb = s.shape
    row_f = lax.broadcasted_iota(jnp.int32, (n_rows, tb), 0).astype(F32)
    rank = lax.broadcasted_iota(jnp.int32, (PEER_TOPK, tb), 0)
    neg = jnp.float32(-jnp.inf)

    def body(r, carry):
        s, vals, second = carry
        m = jnp.max(s, axis=0, keepdims=True)
        pos = jnp.min(jnp.where(s == m, row_f, float(n_rows)), axis=0, keepdims=True)
        hit = row_f == pos
        out = pos if payload is None else jnp.max(jnp.where(hit, payload, -1.0), axis=0, keepdims=True)
        dst = rank == r
        return jnp.where(hit, neg, s), jnp.where(dst, m, vals), jnp.where(dst, out, second)

    zeros = jnp.zeros((PEER_TOPK, tb), F32)
    _, vals, second = lax.fori_loop(0, PEER_TOPK, body, (s, zeros, zeros))
    return vals, second


def _peerq_kernel(h2_ref, wq_ref, k1_ref, k2_ref, sel_ref, sele_ref, idx_ref, g_ref, q_scr, it_scr, gt_scr):
    h = pl.program_id(1)
    tm = h2_ref.shape[0]

    @pl.when(h == 0)
    def _():
        q_scr[...] = _dot(h2_ref[...].astype(BF16), wq_ref[...]).astype(BF16)

    nt = (((1,), (1,)), ((), ()))
    vals, keys = [], []
    for tb in range(tm // LANES):
        qh = q_scr[tb * LANES:(tb + 1) * LANES, pl.ds(pl.multiple_of(h * LANES, LANES), LANES)]
        s12 = jnp.concatenate([lax.dot_general(k1_ref[...], qh, nt, preferred_element_type=F32),
                               lax.dot_general(k2_ref[...], qh, nt, preferred_element_type=F32)], axis=1)
        v12, i12 = _topk_cols(s12)
        vals.append(jnp.concatenate([v12[:, :LANES], v12[:, LANES:]], axis=0))
        keys.append(jnp.concatenate([i12[:, :LANES], i12[:, LANES:]], axis=0))
    vals = jnp.concatenate(vals, axis=1)
    keys = jnp.concatenate(keys, axis=1)
    cand = jnp.dot(sel_ref[...], vals, precision=HIGHEST, preferred_element_type=F32)
    cand_row = lax.broadcasted_iota(jnp.int32, cand.shape, 0)
    cand = jnp.where(cand_row < N_STAIR, cand, -jnp.inf)
    ecand = jnp.dot(sele_ref[...], keys, precision=HIGHEST, preferred_element_type=F32)
    sc, ex = _topk_cols(cand, ecand)
    p = jnp.exp(sc - sc[0:1, :])
    out_rows = pl.ds(pl.multiple_of(h * PEER_TOPK, PEER_TOPK), PEER_TOPK)
    it_scr[out_rows, :] = ex
    gt_scr[out_rows, :] = p / jnp.sum(p, axis=0, keepdims=True)

    @pl.when(h == PEER_HEADS - 1)
    def _():
        idx_ref[...] = it_scr[...].T.astype(jnp.int32)
        g_ref[...] = gt_scr[...].T


def peerq_call(h2, wq_bf, k1, k2, tm=256):
    N = h2.shape[0]
    half = N_KEYS // 2
    k1p = jnp.concatenate([k1, jnp.zeros((N_KEYS, half), F32)], axis=1).astype(BF16)
    k2p = jnp.concatenate([jnp.zeros((N_KEYS, half), F32), k2], axis=1).astype(BF16)
    sel = np.zeros((N_CAND_ROWS, 2 * PEER_TOPK), np.float32)
    sele = np.zeros((N_CAND_ROWS, 2 * PEER_TOPK), np.float32)
    for c, (a, b) in enumerate(_STAIR):
        sel[c, a] = sel[c, PEER_TOPK + b] = 1.0
        sele[c, a] = float(N_KEYS)
        sele[c, PEER_TOPK + b] = 1.0
    const = lambda shape: pl.BlockSpec(shape, lambda i, h: (0, 0))
    row = pl.BlockSpec((tm, HK), lambda i, h: (i, 0))
    return pl.pallas_call(
        _peerq_kernel, grid=(N // tm, PEER_HEADS),
        in_specs=[pl.BlockSpec((tm, D_MODEL), lambda i, h: (i, 0)), const(wq_bf.shape), const((N_KEYS, LANES)),
                  const((N_KEYS, LANES)), const(sel.shape), const(sele.shape)],
        out_specs=[row, row],
        out_shape=[jax.ShapeDtypeStruct((N, HK), jnp.int32), jax.ShapeDtypeStruct((N, HK), F32)],
        scratch_shapes=[pltpu.VMEM((tm, D_MODEL), BF16), pltpu.VMEM((HK, tm), F32), pltpu.VMEM((HK, tm), F32)],
        compiler_params=_cparams(("parallel", "arbitrary")), name="peer_retrieve",
    )(h2, wq_bf, k1p, k2p, jnp.asarray(sel), jnp.asarray(sele))


SC_LANES = 16
SC_CORES = 2
SC_WORKERS = 32
ROWS_PER_GATHER = 32
TOK_BLOCK = 16
ROW_BLOCK = 8
U_UNROLL = 2
COL_BLOCK = 16
N_DCHUNK = D_MODEL // SC_LANES
N_GATHER = HK // ROWS_PER_GATHER


def _sc_gelu(x):
    y = math.sqrt(2.0 / math.pi) * (x + 0.044715 * (x * x * x))
    t = 1.0 - 2.0 / (jnp.exp(2.0 * y) + 1.0)
    return 0.5 * x * (1.0 + t)


def _peer_eval_body(per_w, h_hbm, idx_hbm, g_hbm, u_hbm, v_hbm, o_hbm,
                    idx_blk, g_blk, h_blk, out_blk, rows0, rows1, w_v, sem0, sem1):
    wid = lax.axis_index("s") * SC_CORES + lax.axis_index("c")
    rows = (rows0, rows1)
    sems = (sem0, sem1)
    lane = lax.iota(jnp.int32, SC_LANES)
    zeros = jnp.zeros((SC_LANES,), F32)

    def gather(table, i, j, buf):
        return pltpu.make_async_copy(table.at[idx_blk.at[i, j]], rows[buf], sems[buf])

    def u_compute(i, n, rbuf):
        for half in range(ROWS_PER_GATHER // SC_LANES):
            actv = zeros
            for rg in range(SC_LANES // ROW_BLOCK):
                r0 = half * SC_LANES + rg * ROW_BLOCK

                def c_body(cb, accs):
                    accs = list(accs)
                    for cc in range(U_UNROLL):
                        sl = pl.ds(pl.multiple_of((cb * U_UNROLL + cc) * SC_LANES, SC_LANES), SC_LANES)
                        hc = h_blk[i, sl]
                        for r in range(ROW_BLOCK):
                            accs[r] = accs[r] + rbuf[r0 + r, sl] * hc
                    return tuple(accs)

                accs = lax.fori_loop(0, N_DCHUNK // U_UNROLL, c_body, (zeros,) * ROW_BLOCK)
                for r in range(ROW_BLOCK):
                    actv = jnp.where(lane == rg * ROW_BLOCK + r, jnp.sum(accs[r]), actv)
            sl = pl.ds(n * ROWS_PER_GATHER + half * SC_LANES, SC_LANES)
            w_v[sl] = g_blk[i, sl] * _sc_gelu(actv)

    def v_compute(i, m, rbuf):
        for cb in range(N_DCHUNK // COL_BLOCK):
            def r_body(r, accs):
                wk = plsc.load_gather(w_v, [jnp.full((SC_LANES,), m * ROWS_PER_GATHER, jnp.int32) + r])
                return tuple(accs[cc] + wk * rbuf[r, pl.ds((cb * COL_BLOCK + cc) * SC_LANES, SC_LANES)]
                             for cc in range(COL_BLOCK))

            accs = lax.fori_loop(0, ROWS_PER_GATHER, r_body, (zeros,) * COL_BLOCK)
            for cc in range(COL_BLOCK):
                sl = pl.ds((cb * COL_BLOCK + cc) * SC_LANES, SC_LANES)
                if m == 0:
                    out_blk[i, sl] = accs[cc]
                else:
                    plsc.addupdate(out_blk.at[i, sl], accs[cc])

    @pl.loop(0, per_w // TOK_BLOCK)
    def _(bi):
        t0 = wid * per_w + bi * TOK_BLOCK
        pltpu.sync_copy(idx_hbm.at[pl.ds(t0, TOK_BLOCK)], idx_blk)
        pltpu.sync_copy(g_hbm.at[pl.ds(t0, TOK_BLOCK)], g_blk)
        pltpu.sync_copy(h_hbm.at[pl.ds(t0, TOK_BLOCK)], h_blk)
        gather(u_hbm, 0, 0, 0).start()

        @pl.loop(0, TOK_BLOCK)
        def _(i):
            for n in range(2 * N_GATHER):
                buf = n % 2
                table = u_hbm if n < N_GATHER else v_hbm
                if n + 1 < 2 * N_GATHER:
                    nxt = u_hbm if n + 1 < N_GATHER else v_hbm
                    gather(nxt, i, (n + 1) % N_GATHER, 1 - buf).start()
                else:
                    @pl.when(i + 1 < TOK_BLOCK)
                    def _():
                        gather(u_hbm, i + 1, 0, 1 - buf).start()
                gather(table, i, n % N_GATHER, buf).wait()
                if n < N_GATHER:
                    u_compute(i, n, rows[buf])
                else:
                    v_compute(i, n - N_GATHER, rows[buf])

        pltpu.sync_copy(out_blk, o_hbm.at[pl.ds(t0, TOK_BLOCK)])


def peer_eval_call(h2, idx, g, u_tab, v_tab):
    N = h2.shape[0]
    per_w = N // SC_WORKERS
    assert N % (SC_WORKERS * TOK_BLOCK) == 0
    mesh = plsc.VectorSubcoreMesh(core_axis_name="c", subcore_axis_name="s",
                                  num_cores=SC_CORES, num_subcores=SC_WORKERS // SC_CORES)
    return pl.kernel(
        functools.partial(_peer_eval_body, per_w),
        out_type=jax.ShapeDtypeStruct((N, D_MODEL), F32),
        mesh=mesh,
        scratch_types=[
            pltpu.VMEM((TOK_BLOCK, N_GATHER, ROWS_PER_GATHER), jnp.int32),
            pltpu.VMEM((TOK_BLOCK, HK), F32),
            pltpu.VMEM((TOK_BLOCK, D_MODEL), F32),
            pltpu.VMEM((TOK_BLOCK, D_MODEL), F32),
            pltpu.VMEM((ROWS_PER_GATHER, D_MODEL), F32),
            pltpu.VMEM((ROWS_PER_GATHER, D_MODEL), F32),
            pltpu.VMEM((HK,), F32),
            pltpu.SemaphoreType.DMA,
            pltpu.SemaphoreType.DMA,
        ],
        compiler_params=pltpu.CompilerParams(needs_layout_passes=False),
        name="peer_eval",
    )(h2, idx.reshape(N, N_GATHER, ROWS_PER_GATHER), g, u_tab, v_tab)


def _final_kernel(x_ref, y_ref, gt_ref, g_ref, o_ref):
    x = x_ref[...] + gt_ref[0] * y_ref[...]
    ms = jnp.mean(x * x, axis=-1, keepdims=True)
    o_ref[...] = x * lax.rsqrt(ms + EPS) * g_ref[...]


def final_call(x, y, mod, gf, T, tm=512):
    N = x.shape[0]
    row = pl.BlockSpec((tm, D_MODEL), lambda i: (i, 0))
    return pl.pallas_call(
        _final_kernel, grid=(N // tm,),
        in_specs=[row, row, _mod_spec(tm, T, 5), pl.BlockSpec((1, D_MODEL), lambda i: (0, 0))],
        out_specs=row, out_shape=jax.ShapeDtypeStruct((N, D_MODEL), F32),
        compiler_params=_cparams(("parallel",)), name="final_norm",
    )(x, y, mod, gf)


def kernel(x, c, ada_w, ada_b, norm1_g, norm2_g, w_in, sgu_w, sgu_b, out_norm_a, out_norm_b, w_out,
           peer_wq, peer_k1, peer_k2, peer_u, peer_v, final_g):
    B, T, Dm = x.shape
    L = ada_w.shape[0]
    N = B * T
    mods = ada_mod(c, ada_w, ada_b).reshape(L, B * 6, 1, Dm)
    xf = x.reshape(N, Dm)
    y = None
    for l in range(L):
        mod = mods[l]
        outs = proj_call(xf, y, mods[l - 1] if l else None, mod, norm1_g[l].reshape(1, Dm), w_in[l].astype(BF16), T)
        if l:
            xf, outs = outs[0], outs[1:]
        ua, va, q, k, v = outs
        bias_full = jnp.repeat(sgu_b[l].T, GROUP, axis=1)
        ya = gmlp_call(ua, va, sgu_w[l], bias_full, out_norm_a[l].reshape(1, D_A))
        yb = attn_call(q, k, v, out_norm_b[l].reshape(1, D_B), B, T)
        xf, h2 = out_call(ya, yb, xf, w_out[l].astype(BF16), mod, norm2_g[l].reshape(1, Dm), T)
        idx, gate = peerq_call(h2, peer_wq[l].astype(BF16), peer_k1[l], peer_k2[l])
        y = peer_eval_call(h2, idx, gate, peer_u[l], peer_v[l])
    out = final_call(xf, y, mods[L - 1], final_g.reshape(1, Dm), T)
    return out.reshape(B, T, Dm)
```

```python
import functools
import math

import numpy as np
import jax
import jax.numpy as jnp
from jax import lax
from jax.experimental import pallas as pl
from jax.experimental.pallas import tpu as pltpu
from jax.experimental.pallas import tpu_sc as plsc

F32 = jnp.float32
BF16 = jnp.bfloat16
HIGHEST = lax.Precision.HIGHEST

D_MODEL = 1024
D_A = 512
D_B = 512
GROUP = 64
CHUNK = 128
N_KEYS = 128
PEER_HEADS = 8
PEER_TOPK = 16
HK = PEER_HEADS * PEER_TOPK
EPS = 1e-6
LANES = 128
VMEM_LIMIT = 48 * 1024 * 1024
BATCH_GROUPS = 4

_STAIR = [(a, b) for a in range(PEER_TOPK) for b in range(PEER_TOPK) if (a + 1) * (b + 1) <= PEER_TOPK]
N_STAIR = len(_STAIR)


def _cparams(sem):
    return pltpu.CompilerParams(dimension_semantics=sem, vmem_limit_bytes=VMEM_LIMIT)


def _dot(a, b):
    return jnp.dot(a, b, preferred_element_type=F32)


def _block_diag_mean(n, group):
    i = np.arange(n)
    return jnp.asarray((i[:, None] // group == i[None, :] // group).astype(np.float32) / group, BF16)


def _ada_kernel(c_ref, w_ref, b_ref, o_ref):
    c = c_ref[...]
    ca = c * jax.nn.sigmoid(c)
    o_ref[0] = jnp.dot(ca, w_ref[0], precision=HIGHEST, preferred_element_type=F32) + b_ref[0]


def ada_mod(c, ada_w, ada_b):
    L, Dm, E = ada_w.shape
    Bc = c.shape[0]
    tn = 1536
    return pl.pallas_call(
        _ada_kernel,
        grid=(L, E // tn),
        in_specs=[pl.BlockSpec((Bc, Dm), lambda l, j: (0, 0)),
                  pl.BlockSpec((1, Dm, tn), lambda l, j: (l, 0, j)),
                  pl.BlockSpec((1, 1, tn), lambda l, j: (l, 0, j))],
        out_specs=pl.BlockSpec((1, Bc, tn), lambda l, j: (l, 0, j)),
        out_shape=jax.ShapeDtypeStruct((L, Bc, E), F32),
        compiler_params=_cparams(("parallel", "parallel")),
        name="ada_mod",
    )(c, ada_w, ada_b.reshape(L, 1, E))


def _mod_spec(tm, T, j):
    return pl.BlockSpec((1, 1, D_MODEL), lambda i: (((i * tm) // T) * 6 + j, 0, 0))


def _rms_mod(x, g, sc, sh):
    ms = jnp.mean(x * x, axis=-1, keepdims=True)
    return (x * lax.rsqrt(ms + EPS) * g) * (1.0 + sc) + sh


def _proj_kernel(has_y, *refs):
    if has_y:
        x_ref, y_ref, gt_ref, g_ref, sc_ref, sh_ref, w_ref, xo_ref, ua_ref, va_ref, q_ref, k_ref, v_ref = refs
        x = x_ref[...] + gt_ref[0] * y_ref[...]
        xo_ref[...] = x
    else:
        x_ref, g_ref, sc_ref, sh_ref, w_ref, ua_ref, va_ref, q_ref, k_ref, v_ref = refs
        x = x_ref[...]
    h = _rms_mod(x, g_ref[...], sc_ref[0], sh_ref[0]).astype(BF16)
    outs = (ua_ref, va_ref, q_ref, k_ref, v_ref)
    for j, o_ref in enumerate(outs):
        p = _dot(h, w_ref[:, j * D_A:(j + 1) * D_A])
        if j < 2:
            p = jax.nn.gelu(p)
        o_ref[...] = p.astype(o_ref.dtype)


def proj_call(x, y, mod_prev, mod, g1, w_in_bf, T, tm=512):
    N = x.shape[0]
    has_y = y is not None
    row = pl.BlockSpec((tm, D_MODEL), lambda i: (i, 0))
    half = pl.BlockSpec((tm, D_A), lambda i: (i, 0))
    in_specs = [row]
    args = [x]
    if has_y:
        in_specs += [row, _mod_spec(tm, T, 5)]
        args += [y, mod_prev]
    in_specs += [pl.BlockSpec((1, D_MODEL), lambda i: (0, 0)), _mod_spec(tm, T, 1), _mod_spec(tm, T, 0),
                 pl.BlockSpec(w_in_bf.shape, lambda i: (0, 0))]
    args += [g1, mod, mod, w_in_bf]
    out_specs = [half] * 5
    out_shape = [jax.ShapeDtypeStruct((N, D_A), F32)] * 2 + [jax.ShapeDtypeStruct((N, D_A), BF16)] * 3
    if has_y:
        out_specs = [row] + out_specs
        out_shape = [jax.ShapeDtypeStruct((N, D_MODEL), F32)] + out_shape
    return pl.pallas_call(
        functools.partial(_proj_kernel, has_y),
        grid=(N // tm,), in_specs=in_specs, out_specs=out_specs, out_shape=out_shape,
        compiler_params=_cparams(("parallel",)), name="proj",
    )(*args)


def _gmlp_kernel(ua_ref, va_ref, w_ref, bias_ref, a_ref, ga_ref, o_ref):
    A = a_ref[...]
    row = lax.broadcasted_iota(jnp.int32, (CHUNK, CHUNK), 0)
    col = lax.broadcasted_iota(jnp.int32, (CHUNK, CHUNK), 1)
    causal = row >= col
    first_group = col < GROUP
    ws = [jnp.where(causal, w_ref[g], 0.0).astype(BF16) for g in range(D_A // GROUP)]
    bias = bias_ref[...]
    ga = ga_ref[...]
    for c in range(ua_ref.shape[0] // CHUNK):
        rows = slice(c * CHUNK, (c + 1) * CHUNK)
        v = va_ref[rows, :]
        v_hi = v.astype(BF16)
        v_lo = (v - v_hi.astype(F32)).astype(BF16)
        d = v - (_dot(v_hi, A) + _dot(v_lo, A))
        var = _dot((d * d).astype(BF16), A)
        vn = (d * lax.rsqrt(var + EPS)).astype(BF16)
        parts = []
        for p in range(D_A // LANES):
            vp = vn[:, p * LANES:(p + 1) * LANES]
            parts.append(jnp.where(first_group, _dot(ws[2 * p], vp), _dot(ws[2 * p + 1], vp)))
        s = jnp.concatenate(parts, axis=1) + bias
        y = ua_ref[rows, :] * s
        ms = _dot((y * y).astype(BF16), A)
        o_ref[rows, :] = (y * lax.rsqrt(ms + EPS) * ga).astype(BF16)


def gmlp_call(ua, va, sgu_w, bias_full, ga, tm=512):
    N = ua.shape[0]
    half = pl.BlockSpec((tm, D_A), lambda i: (i, 0))
    return pl.pallas_call(
        _gmlp_kernel, grid=(N // tm,),
        in_specs=[half, half,
                  pl.BlockSpec(sgu_w.shape, lambda i: (0, 0, 0)),
                  pl.BlockSpec((CHUNK, D_A), lambda i: (0, 0)),
                  pl.BlockSpec((D_A, D_A), lambda i: (0, 0)),
                  pl.BlockSpec((1, D_A), lambda i: (0, 0))],
        out_specs=half, out_shape=jax.ShapeDtypeStruct((N, D_A), BF16),
        compiler_params=_cparams(("parallel",)), name="gmlp",
    )(ua, va, sgu_w, bias_full, _block_diag_mean(D_A, GROUP), ga)


EXP_UNDERFLOW = -104.0


def _attn_kernel(q_ref, k_ref, v_ref, m_ref, ones_ref, a_ref, gb_ref, o_ref, acc_ref, carry_ref):
    i = pl.program_id(1)
    n_pair = D_B // LANES
    lane = lax.broadcasted_iota(jnp.int32, (CHUNK, LANES), 1)
    row = lax.broadcasted_iota(jnp.int32, (CHUNK, LANES), 0)
    head_lanes = (lane < GROUP, lane >= GROUP)
    M = m_ref[...]
    ONES = ones_ref[...]
    scale = 1.0 / math.sqrt(GROUP)
    acc_ref[...] = jnp.zeros_like(acc_ref)
    carry_ref[...] = jnp.zeros_like(carry_ref)

    def cond(state):
        jj, cmax = state
        return jnp.logical_and(jj <= i, cmax > EXP_UNDERFLOW)

    def body(state):
        jj, _ = state
        j = i - jj
        start = pl.multiple_of(j * CHUNK, CHUNK)
        mask = (j * CHUNK + lane) < (i * CHUNK + row)
        cmax = jnp.full((CHUNK, LANES), -jnp.inf, F32)
        for p in range(n_pair):
            cols = slice(p * LANES, (p + 1) * LANES)
            q = q_ref[:, cols]
            ks = k_ref[pl.ds(start, CHUNK), cols]
            vs = v_ref[pl.ds(start, CHUNK), cols]
            acc = acc_ref[:, cols]
            for h in range(2):
                qh = jnp.where(head_lanes[h], q, jnp.zeros_like(q))
                z = lax.dot_general(qh, ks, (((1,), (1,)), ((), ())), preferred_element_type=F32) * scale
                lsz = jnp.minimum(z, 0.0) - jnp.log1p(jnp.exp(-jnp.abs(z)))
                L = jnp.where(mask, lsz - z, 0.0)
                L_hi = L.astype(BF16)
                L_lo = (L - L_hi.astype(F32)).astype(BF16)
                c = carry_ref[2 * p + h]
                tail = _dot(L_hi, M) + _dot(L_lo, M) + c
                a = jnp.where(mask, jnp.exp(lsz + tail), 0.0).astype(BF16)
                acc = acc + _dot(a, jnp.where(head_lanes[h], vs, jnp.zeros_like(vs)))
                c = c + _dot(L_hi, ONES) + _dot(L_lo, ONES)
                carry_ref[2 * p + h] = c
                cmax = jnp.maximum(cmax, c)
            acc_ref[:, cols] = acc
        return jj + 1, jnp.max(cmax)

    lax.while_loop(cond, body, (jnp.int32(0), jnp.float32(0.0)))
    for p in range(n_pair):
        cols = slice(p * LANES, (p + 1) * LANES)
        acc = acc_ref[:, cols]
        ms = _dot((acc * acc).astype(BF16), a_ref[...])
        o_ref[:, cols] = (acc * lax.rsqrt(ms + EPS) * gb_ref[:, cols]).astype(BF16)


def attn_call(q, k, v, gb, B, T):
    N = q.shape[0]
    nq = T // CHUNK
    i = np.arange(CHUNK)
    M = jnp.asarray((i[:, None] > i[None, :]).astype(np.float32), BF16)
    ONES = jnp.ones((CHUNK, LANES), BF16)
    const = pl.BlockSpec((CHUNK, LANES), lambda b, i: (0, 0))
    kv = pl.BlockSpec((T, D_B), lambda b, i: (b, 0))
    qo = pl.BlockSpec((CHUNK, D_B), lambda b, i: (b * nq + i, 0))
    return pl.pallas_call(
        _attn_kernel, grid=(B, nq),
        in_specs=[qo, kv, kv, const, const, const, pl.BlockSpec((1, D_B), lambda b, i: (0, 0))],
        out_specs=qo,
        out_shape=jax.ShapeDtypeStruct((N, D_B), BF16),
        scratch_shapes=[pltpu.VMEM((CHUNK, D_B), F32), pltpu.VMEM((D_B // GROUP, CHUNK, LANES), F32)],
        compiler_params=_cparams(("parallel", "arbitrary")), name="stickbreak_attn",
    )(q, k, v, M, ONES, _block_diag_mean(LANES, GROUP), gb)


def _out_kernel(ya_ref, yb_ref, x_ref, w_ref, gt_ref, g2_ref, sc_ref, sh_ref, xo_ref, h2_ref):
    o = _dot(ya_ref[...], w_ref[:D_A, :]) + _dot(yb_ref[...], w_ref[D_A:, :])
    x = x_ref[...] + gt_ref[0] * o
    xo_ref[...] = x
    h2_ref[...] = _rms_mod(x, g2_ref[...], sc_ref[0], sh_ref[0])


def out_call(ya, yb, x, w_out_bf, mod, g2, T, tm=512):
    N = x.shape[0]
    row = pl.BlockSpec((tm, D_MODEL), lambda i: (i, 0))
    half = pl.BlockSpec((tm, D_A), lambda i: (i, 0))
    return pl.pallas_call(
        _out_kernel, grid=(N // tm,),
        in_specs=[half, half, row, pl.BlockSpec(w_out_bf.shape, lambda i: (0, 0)), _mod_spec(tm, T, 2),
                  pl.BlockSpec((1, D_MODEL), lambda i: (0, 0)), _mod_spec(tm, T, 4), _mod_spec(tm, T, 3)],
        out_specs=[row, row], out_shape=[jax.ShapeDtypeStruct((N, D_MODEL), F32)] * 2,
        compiler_params=_cparams(("parallel",)), name="out_proj",
    )(ya, yb, x, w_out_bf, mod, g2, mod, mod)


N_CAND_ROWS = 64


def _topk_cols(s, payload=None):
    n_rows, tb = s.shape
    row_f = lax.broadcasted_iota(jnp.int32, (n_rows, tb), 0).astype(F32)
    rank = lax.broadcasted_iota(jnp.int32, (PEER_TOPK, tb), 0)
    neg = jnp.float32(-jnp.inf)

    def body(r, carry):
        s, vals, second = carry
        m = jnp.max(s, axis=0, keepdims=True)
        pos = jnp.min(jnp.where(s == m, row_f, float(n_rows)), axis=0, keepdims=True)
        hit = row_f == pos
        out = pos if payload is None else jnp.max(jnp.where(hit, payload, -1.0), axis=0, keepdims=True)
        dst = rank == r
        return jnp.where(hit, neg, s), jnp.where(dst, m, vals), jnp.where(dst, out, second)

    zeros = jnp.zeros((PEER_TOPK, tb), F32)
    _, vals, second = lax.fori_loop(0, PEER_TOPK, body, (s, zeros, zeros))
    return vals, second


def _peerq_kernel(h2_ref, wq_ref, k1_ref, k2_ref, sel_ref, sele_ref, idx_ref, g_ref, q_scr, it_scr, gt_scr):
    h = pl.program_id(1)
    tm = h2_ref.shape[0]

    @pl.when(h == 0)
    def _():
        q_scr[...] = _dot(h2_ref[...].astype(BF16), wq_ref[...]).astype(BF16)

    nt = (((1,), (1,)), ((), ()))
    vals, keys = [], []
    for tb in range(tm // LANES):
        qh = q_scr[tb * LANES:(tb + 1) * LANES, pl.ds(pl.multiple_of(h * LANES, LANES), LANES)]
        s12 = jnp.concatenate([lax.dot_general(k1_ref[...], qh, nt, preferred_element_type=F32),
                               lax.dot_general(k2_ref[...], qh, nt, preferred_element_type=F32)], axis=1)
        v12, i12 = _topk_cols(s12)
        vals.append(jnp.concatenate([v12[:, :LANES], v12[:, LANES:]], axis=0))
        keys.append(jnp.concatenate([i12[:, :LANES], i12[:, LANES:]], axis=0))
    vals = jnp.concatenate(vals, axis=1)
    keys = jnp.concatenate(keys, axis=1)
    cand = jnp.dot(sel_ref[...], vals, precision=HIGHEST, preferred_element_type=F32)
    cand_row = lax.broadcasted_iota(jnp.int32, cand.shape, 0)
    cand = jnp.where(cand_row < N_STAIR, cand, -jnp.inf)
    ecand = jnp.dot(sele_ref[...], keys, precision=HIGHEST, preferred_element_type=F32)
    sc, ex = _topk_cols(cand, ecand)
    p = jnp.exp(sc - sc[0:1, :])
    out_rows = pl.ds(pl.multiple_of(h * PEER_TOPK, PEER_TOPK), PEER_TOPK)
    it_scr[out_rows, :] = ex
    gt_scr[out_rows, :] = p / jnp.sum(p, axis=0, keepdims=True)

    @pl.when(h == PEER_HEADS - 1)
    def _():
        idx_ref[...] = it_scr[...].T.astype(jnp.int32)
        g_ref[...] = gt_scr[...].T


def peerq_call(h2, wq_bf, k1, k2, tm=256):
    N = h2.shape[0]
    half = N_KEYS // 2
    k1p = jnp.concatenate([k1, jnp.zeros((N_KEYS, half), F32)], axis=1).astype(BF16)
    k2p = jnp.concatenate([jnp.zeros((N_KEYS, half), F32), k2], axis=1).astype(BF16)
    sel = np.zeros((N_CAND_ROWS, 2 * PEER_TOPK), np.float32)
    sele = np.zeros((N_CAND_ROWS, 2 * PEER_TOPK), np.float32)
    for c, (a, b) in enumerate(_STAIR):
        sel[c, a] = sel[c, PEER_TOPK + b] = 1.0
        sele[c, a] = float(N_KEYS)
        sele[c, PEER_TOPK + b] = 1.0
    const = lambda shape: pl.BlockSpec(shape, lambda i, h: (0, 0))
    row = pl.BlockSpec((tm, HK), lambda i, h: (i, 0))
    return pl.pallas_call(
        _peerq_kernel, grid=(N // tm, PEER_HEADS),
        in_specs=[pl.BlockSpec((tm, D_MODEL), lambda i, h: (i, 0)), const(wq_bf.shape), const((N_KEYS, LANES)),
                  const((N_KEYS, LANES)), const(sel.shape), const(sele.shape)],
        out_specs=[row, row],
        out_shape=[jax.ShapeDtypeStruct((N, HK), jnp.int32), jax.ShapeDtypeStruct((N, HK), F32)],
        scratch_shapes=[pltpu.VMEM((tm, D_MODEL), BF16), pltpu.VMEM((HK, tm), F32), pltpu.VMEM((HK, tm), F32)],
        compiler_params=_cparams(("parallel", "arbitrary")), name="peer_retrieve",
    )(h2, wq_bf, k1p, k2p, jnp.asarray(sel), jnp.asarray(sele))


SC_LANES = 16
SC_CORES = 2
SC_WORKERS = 32
ROWS_PER_GATHER = 16
N_ROW_BUF = 4
LOOKAHEAD = N_ROW_BUF - 1
TOK_BLOCK = 16
ROW_BLOCK = 8
U_UNROLL = 2
COL_BLOCK = 16
N_DCHUNK = D_MODEL // SC_LANES
N_GATHER = HK // ROWS_PER_GATHER


def _sc_gelu(x):
    y = math.sqrt(2.0 / math.pi) * (x + 0.044715 * (x * x * x))
    t = 1.0 - 2.0 / (jnp.exp(2.0 * y) + 1.0)
    return 0.5 * x * (1.0 + t)


def _peer_eval_body(per_w, h_hbm, idx_hbm, g_hbm, u_hbm, v_hbm, o_hbm,
                    idx_blk, g_blk, h_blk, out_blk, w_v, *bufs):
    wid = lax.axis_index("s") * SC_CORES + lax.axis_index("c")
    rows = bufs[:N_ROW_BUF]
    sems = bufs[N_ROW_BUF:]
    lane = lax.iota(jnp.int32, SC_LANES)
    zeros = jnp.zeros((SC_LANES,), F32)

    def gather(table, i, j, buf):
        return pltpu.make_async_copy(table.at[idx_blk.at[i, j]], rows[buf], sems[buf])

    def u_compute(i, n, rbuf):
        for half in range(ROWS_PER_GATHER // SC_LANES):
            actv = zeros
            for rg in range(SC_LANES // ROW_BLOCK):
                r0 = half * SC_LANES + rg * ROW_BLOCK

                def c_body(cb, accs):
                    accs = list(accs)
                    for cc in range(U_UNROLL):
                        sl = pl.ds(pl.multiple_of((cb * U_UNROLL + cc) * SC_LANES, SC_LANES), SC_LANES)
                        hc = h_blk[i, sl]
                        for r in range(ROW_BLOCK):
                            accs[r] = accs[r] + rbuf[r0 + r, sl] * hc
                    return tuple(accs)

                accs = lax.fori_loop(0, N_DCHUNK // U_UNROLL, c_body, (zeros,) * ROW_BLOCK)
                for r in range(ROW_BLOCK):
                    actv = jnp.where(lane == rg * ROW_BLOCK + r, jnp.sum(accs[r]), actv)
            sl = pl.ds(n * ROWS_PER_GATHER + half * SC_LANES, SC_LANES)
            w_v[sl] = g_blk[i, sl] * _sc_gelu(actv)

    def v_compute(i, m, rbuf):
        for cb in range(N_DCHUNK // COL_BLOCK):
            def r_body(r, accs):
                wk = plsc.load_gather(w_v, [jnp.full((SC_LANES,), m * ROWS_PER_GATHER, jnp.int32) + r])
                return tuple(accs[cc] + wk * rbuf[r, pl.ds((cb * COL_BLOCK + cc) * SC_LANES, SC_LANES)]
                             for cc in range(COL_BLOCK))

            accs = lax.fori_loop(0, ROWS_PER_GATHER, r_body, (zeros,) * COL_BLOCK)
            for cc in range(COL_BLOCK):
                sl = pl.ds((cb * COL_BLOCK + cc) * SC_LANES, SC_LANES)
                if m == 0:
                    out_blk[i, sl] = accs[cc]
                else:
                    plsc.addupdate(out_blk.at[i, sl], accs[cc])

    @pl.loop(0, per_w // TOK_BLOCK)
    def _(bi):
        t0 = wid * per_w + bi * TOK_BLOCK
        pltpu.sync_copy(idx_hbm.at[pl.ds(t0, TOK_BLOCK)], idx_blk)
        pltpu.sync_copy(g_hbm.at[pl.ds(t0, TOK_BLOCK)], g_blk)
        pltpu.sync_copy(h_hbm.at[pl.ds(t0, TOK_BLOCK)], h_blk)
        def task_gather(i, n):
            table = u_hbm if n < N_GATHER else v_hbm
            return gather(table, i, n % N_GATHER, n % N_ROW_BUF)

        for n in range(LOOKAHEAD):
            task_gather(0, n).start()

        @pl.loop(0, TOK_BLOCK)
        def _(i):
            for n in range(2 * N_GATHER):
                ahead = n + LOOKAHEAD
                if ahead < 2 * N_GATHER:
                    task_gather(i, ahead).start()
                else:
                    @pl.when(i + 1 < TOK_BLOCK)
                    def _():
                        task_gather(i + 1, ahead - 2 * N_GATHER).start()
                task_gather(i, n).wait()
                if n < N_GATHER:
                    u_compute(i, n, rows[n % N_ROW_BUF])
                else:
                    v_compute(i, n - N_GATHER, rows[n % N_ROW_BUF])

        pltpu.sync_copy(out_blk, o_hbm.at[pl.ds(t0, TOK_BLOCK)])


def peer_eval_call(h2, idx, g, u_tab, v_tab):
    N = h2.shape[0]
    per_w = N // SC_WORKERS
    assert N % (SC_WORKERS * TOK_BLOCK) == 0
    assert (2 * N_GATHER) % N_ROW_BUF == 0 and LOOKAHEAD < N_ROW_BUF
    mesh = plsc.VectorSubcoreMesh(core_axis_name="c", subcore_axis_name="s",
                                  num_cores=SC_CORES, num_subcores=SC_WORKERS // SC_CORES)
    return pl.kernel(
        functools.partial(_peer_eval_body, per_w),
        out_type=jax.ShapeDtypeStruct((N, D_MODEL), F32),
        mesh=mesh,
        scratch_types=[
            pltpu.VMEM((TOK_BLOCK, N_GATHER, ROWS_PER_GATHER), jnp.int32),
            pltpu.VMEM((TOK_BLOCK, HK), F32),
            pltpu.VMEM((TOK_BLOCK, D_MODEL), F32),
            pltpu.VMEM((TOK_BLOCK, D_MODEL), F32),
            pltpu.VMEM((HK,), F32),
        ] + [pltpu.VMEM((ROWS_PER_GATHER, D_MODEL), F32)] * N_ROW_BUF + [pltpu.SemaphoreType.DMA] * N_ROW_BUF,
        compiler_params=pltpu.CompilerParams(needs_layout_passes=False),
        name="peer_eval",
    )(h2, idx.reshape(N, N_GATHER, ROWS_PER_GATHER), g, u_tab, v_tab)


def _final_kernel(x_ref, y_ref, gt_ref, g_ref, o_ref):
    x = x_ref[...] + gt_ref[0] * y_ref[...]
    ms = jnp.mean(x * x, axis=-1, keepdims=True)
    o_ref[...] = x * lax.rsqrt(ms + EPS) * g_ref[...]


def final_call(x, y, mod, gf, T, tm=512):
    N = x.shape[0]
    row = pl.BlockSpec((tm, D_MODEL), lambda i: (i, 0))
    return pl.pallas_call(
        _final_kernel, grid=(N // tm,),
        in_specs=[row, row, _mod_spec(tm, T, 5), pl.BlockSpec((1, D_MODEL), lambda i: (0, 0))],
        out_specs=row, out_shape=jax.ShapeDtypeStruct((N, D_MODEL), F32),
        compiler_params=_cparams(("parallel",)), name="final_norm",
    )(x, y, mod, gf)


def kernel(x, c, ada_w, ada_b, norm1_g, norm2_g, w_in, sgu_w, sgu_b, out_norm_a, out_norm_b, w_out,
           peer_wq, peer_k1, peer_k2, peer_u, peer_v, final_g):
    B, T, Dm = x.shape
    L = ada_w.shape[0]
    N = B * T
    mods = ada_mod(c, ada_w, ada_b).reshape(L, B * 6, 1, Dm)
    Bg = B // BATCH_GROUPS
    gmods = [[mods[l, g * Bg * 6:(g + 1) * Bg * 6] for g in range(BATCH_GROUPS)] for l in range(L)]
    xs = [x[g * Bg:(g + 1) * Bg].reshape(Bg * T, Dm) for g in range(BATCH_GROUPS)]
    ys = [None] * BATCH_GROUPS
    for l in range(L):
        w_in_bf, w_out_bf, wq_bf = w_in[l].astype(BF16), w_out[l].astype(BF16), peer_wq[l].astype(BF16)
        bias_full = jnp.repeat(sgu_b[l].T, GROUP, axis=1)
        for g in range(BATCH_GROUPS):
            mod = gmods[l][g]
            outs = proj_call(xs[g], ys[g], gmods[l - 1][g] if l else None, mod, norm1_g[l].reshape(1, Dm), w_in_bf, T)
            if l:
                xs[g], outs = outs[0], outs[1:]
            ua, va, q, k, v = outs
            ya = gmlp_call(ua, va, sgu_w[l], bias_full, out_norm_a[l].reshape(1, D_A))
            yb = attn_call(q, k, v, out_norm_b[l].reshape(1, D_B), Bg, T)
            xs[g], h2 = out_call(ya, yb, xs[g], w_out_bf, mod, norm2_g[l].reshape(1, Dm), T)
            idx, gate = peerq_call(h2, wq_bf, peer_k1[l], peer_k2[l])
            ys[g] = peer_eval_call(h2, idx, gate, peer_u[l], peer_v[l])
    outs = [final_call(xs[g], ys[g], gmods[L - 1][g], final_g.reshape(1, Dm), T) for g in range(BATCH_GROUPS)]
    return jnp.concatenate(outs, axis=0).reshape(B, T, Dm)
```

```python
import functools
import math

import numpy as np
import jax
import jax.numpy as jnp
from jax import lax
from jax.experimental import pallas as pl
from jax.experimental.pallas import tpu as pltpu
from jax.experimental.pallas import tpu_sc as plsc

F32 = jnp.float32
BF16 = jnp.bfloat16
HIGHEST = lax.Precision.HIGHEST

D_MODEL = 1024
D_A = 512
D_B = 512
GROUP = 64
CHUNK = 128
N_KEYS = 128
PEER_HEADS = 8
PEER_TOPK = 16
HK = PEER_HEADS * PEER_TOPK
EPS = 1e-6
LANES = 128
VMEM_LIMIT = 48 * 1024 * 1024
BATCH_GROUPS = 4

_STAIR = [(a, b) for a in range(PEER_TOPK) for b in range(PEER_TOPK) if (a + 1) * (b + 1) <= PEER_TOPK]
N_STAIR = len(_STAIR)


def _cparams(sem):
    return pltpu.CompilerParams(dimension_semantics=sem, vmem_limit_bytes=VMEM_LIMIT)


def _dot(a, b):
    return jnp.dot(a, b, preferred_element_type=F32)


def _block_diag_mean(n, group):
    i = np.arange(n)
    return jnp.asarray((i[:, None] // group == i[None, :] // group).astype(np.float32) / group, BF16)


def _ada_kernel(c_ref, w_ref, b_ref, o_ref):
    c = c_ref[...]
    ca = c * jax.nn.sigmoid(c)
    o_ref[0] = jnp.dot(ca, w_ref[0], precision=HIGHEST, preferred_element_type=F32) + b_ref[0]


def ada_mod(c, ada_w, ada_b):
    L, Dm, E = ada_w.shape
    Bc = c.shape[0]
    tn = 1536
    return pl.pallas_call(
        _ada_kernel,
        grid=(L, E // tn),
        in_specs=[pl.BlockSpec((Bc, Dm), lambda l, j: (0, 0)),
                  pl.BlockSpec((1, Dm, tn), lambda l, j: (l, 0, j)),
                  pl.BlockSpec((1, 1, tn), lambda l, j: (l, 0, j))],
        out_specs=pl.BlockSpec((1, Bc, tn), lambda l, j: (l, 0, j)),
        out_shape=jax.ShapeDtypeStruct((L, Bc, E), F32),
        compiler_params=_cparams(("parallel", "parallel")),
        name="ada_mod",
    )(c, ada_w, ada_b.reshape(L, 1, E))


def _mod_spec(tm, T, j):
    return pl.BlockSpec((1, 1, D_MODEL), lambda i: (((i * tm) // T) * 6 + j, 0, 0))


def _rms_mod(x, g, sc, sh):
    ms = jnp.mean(x * x, axis=-1, keepdims=True)
    return (x * lax.rsqrt(ms + EPS) * g) * (1.0 + sc) + sh


def _proj_kernel(has_y, *refs):
    if has_y:
        x_ref, y_ref, gt_ref, g_ref, sc_ref, sh_ref, w_ref, xo_ref, ua_ref, va_ref, q_ref, k_ref, v_ref = refs
        x = x_ref[...] + gt_ref[0] * y_ref[...]
        xo_ref[...] = x
    else:
        x_ref, g_ref, sc_ref, sh_ref, w_ref, ua_ref, va_ref, q_ref, k_ref, v_ref = refs
        x = x_ref[...]
    h = _rms_mod(x, g_ref[...], sc_ref[0], sh_ref[0]).astype(BF16)
    outs = (ua_ref, va_ref, q_ref, k_ref, v_ref)
    for j, o_ref in enumerate(outs):
        p = _dot(h, w_ref[:, j * D_A:(j + 1) * D_A])
        if j < 2:
            p = jax.nn.gelu(p)
        o_ref[...] = p.astype(o_ref.dtype)


def proj_call(x, y, mod_prev, mod, g1, w_in_bf, T, tm=512):
    N = x.shape[0]
    has_y = y is not None
    row = pl.BlockSpec((tm, D_MODEL), lambda i: (i, 0))
    half = pl.BlockSpec((tm, D_A), lambda i: (i, 0))
    in_specs = [row]
    args = [x]
    if has_y:
        in_specs += [row, _mod_spec(tm, T, 5)]
        args += [y, mod_prev]
    in_specs += [pl.BlockSpec((1, D_MODEL), lambda i: (0, 0)), _mod_spec(tm, T, 1), _mod_spec(tm, T, 0),
                 pl.BlockSpec(w_in_bf.shape, lambda i: (0, 0))]
    args += [g1, mod, mod, w_in_bf]
    out_specs = [half] * 5
    out_shape = [jax.ShapeDtypeStruct((N, D_A), F32)] * 2 + [jax.ShapeDtypeStruct((N, D_A), BF16)] * 3
    if has_y:
        out_specs = [row] + out_specs
        out_shape = [jax.ShapeDtypeStruct((N, D_MODEL), F32)] + out_shape
    return pl.pallas_call(
        functools.partial(_proj_kernel, has_y),
        grid=(N // tm,), in_specs=in_specs, out_specs=out_specs, out_shape=out_shape,
        compiler_params=_cparams(("parallel",)), name="proj",
    )(*args)


def _gmlp_kernel(ua_ref, va_ref, w_ref, bias_ref, a_ref, ga_ref, o_ref):
    A = a_ref[...]
    row = lax.broadcasted_iota(jnp.int32, (CHUNK, CHUNK), 0)
    col = lax.broadcasted_iota(jnp.int32, (CHUNK, CHUNK), 1)
    causal = row >= col
    first_group = col < GROUP
    ws = [jnp.where(causal, w_ref[g], 0.0).astype(BF16) for g in range(D_A // GROUP)]
    bias = bias_ref[...]
    ga = ga_ref[...]
    for c in range(ua_ref.shape[0] // CHUNK):
        rows = slice(c * CHUNK, (c + 1) * CHUNK)
        v = va_ref[rows, :]
        v_hi = v.astype(BF16)
        v_lo = (v - v_hi.astype(F32)).astype(BF16)
        d = v - (_dot(v_hi, A) + _dot(v_lo, A))
        var = _dot((d * d).astype(BF16), A)
        vn = (d * lax.rsqrt(var + EPS)).astype(BF16)
        parts = []
        for p in range(D_A // LANES):
            vp = vn[:, p * LANES:(p + 1) * LANES]
            parts.append(jnp.where(first_group, _dot(ws[2 * p], vp), _dot(ws[2 * p + 1], vp)))
        s = jnp.concatenate(parts, axis=1) + bias
        y = ua_ref[rows, :] * s
        ms = _dot((y * y).astype(BF16), A)
        o_ref[rows, :] = (y * lax.rsqrt(ms + EPS) * ga).astype(BF16)


def gmlp_call(ua, va, sgu_w, bias_full, ga, tm=512):
    N = ua.shape[0]
    half = pl.BlockSpec((tm, D_A), lambda i: (i, 0))
    return pl.pallas_call(
        _gmlp_kernel, grid=(N // tm,),
        in_specs=[half, half,
                  pl.BlockSpec(sgu_w.shape, lambda i: (0, 0, 0)),
                  pl.BlockSpec((CHUNK, D_A), lambda i: (0, 0)),
                  pl.BlockSpec((D_A, D_A), lambda i: (0, 0)),
                  pl.BlockSpec((1, D_A), lambda i: (0, 0))],
        out_specs=half, out_shape=jax.ShapeDtypeStruct((N, D_A), BF16),
        compiler_params=_cparams(("parallel",)), name="gmlp",
    )(ua, va, sgu_w, bias_full, _block_diag_mean(D_A, GROUP), ga)


EXP_UNDERFLOW = -104.0


def _attn_kernel(q_ref, k_ref, v_ref, m_ref, ones_ref, a_ref, gb_ref, o_ref, acc_ref, carry_ref):
    i = pl.program_id(1)
    n_pair = D_B // LANES
    lane = lax.broadcasted_iota(jnp.int32, (CHUNK, LANES), 1)
    row = lax.broadcasted_iota(jnp.int32, (CHUNK, LANES), 0)
    head_lanes = (lane < GROUP, lane >= GROUP)
    M = m_ref[...]
    ONES = ones_ref[...]
    scale = 1.0 / math.sqrt(GROUP)
    acc_ref[...] = jnp.zeros_like(acc_ref)
    carry_ref[...] = jnp.zeros_like(carry_ref)

    def cond(state):
        jj, cmax = state
        return jnp.logical_and(jj <= i, cmax > EXP_UNDERFLOW)

    def body(state):
        jj, _ = state
        j = i - jj
        start = pl.multiple_of(j * CHUNK, CHUNK)
        mask = (j * CHUNK + lane) < (i * CHUNK + row)
        cmax = jnp.full((CHUNK, LANES), -jnp.inf, F32)
        for p in range(n_pair):
            cols = slice(p * LANES, (p + 1) * LANES)
            q = q_ref[:, cols]
            ks = k_ref[pl.ds(start, CHUNK), cols]
            vs = v_ref[pl.ds(start, CHUNK), cols]
            acc = acc_ref[:, cols]
            for h in range(2):
                qh = jnp.where(head_lanes[h], q, jnp.zeros_like(q))
                z = lax.dot_general(qh, ks, (((1,), (1,)), ((), ())), preferred_element_type=F32) * scale
                lsz = jnp.minimum(z, 0.0) - jnp.log1p(jnp.exp(-jnp.abs(z)))
                L = jnp.where(mask, lsz - z, 0.0)
                L_hi = L.astype(BF16)
                L_lo = (L - L_hi.astype(F32)).astype(BF16)
                c = carry_ref[2 * p + h]
                tail = _dot(L_hi, M) + _dot(L_lo, M) + c
                a = jnp.where(mask, jnp.exp(lsz + tail), 0.0).astype(BF16)
                acc = acc + _dot(a, jnp.where(head_lanes[h], vs, jnp.zeros_like(vs)))
                c = c + _dot(L_hi, ONES) + _dot(L_lo, ONES)
                carry_ref[2 * p + h] = c
                cmax = jnp.maximum(cmax, c)
            acc_ref[:, cols] = acc
        return jj + 1, jnp.max(cmax)

    lax.while_loop(cond, body, (jnp.int32(0), jnp.float32(0.0)))
    for p in range(n_pair):
        cols = slice(p * LANES, (p + 1) * LANES)
        acc = acc_ref[:, cols]
        ms = _dot((acc * acc).astype(BF16), a_ref[...])
        o_ref[:, cols] = (acc * lax.rsqrt(ms + EPS) * gb_ref[:, cols]).astype(BF16)


def attn_call(q, k, v, gb, B, T):
    N = q.shape[0]
    nq = T // CHUNK
    i = np.arange(CHUNK)
    M = jnp.asarray((i[:, None] > i[None, :]).astype(np.float32), BF16)
    ONES = jnp.ones((CHUNK, LANES), BF16)
    const = pl.BlockSpec((CHUNK, LANES), lambda b, i: (0, 0))
    kv = pl.BlockSpec((T, D_B), lambda b, i: (b, 0))
    qo = pl.BlockSpec((CHUNK, D_B), lambda b, i: (b * nq + i, 0))
    return pl.pallas_call(
        _attn_kernel, grid=(B, nq),
        in_specs=[qo, kv, kv, const, const, const, pl.BlockSpec((1, D_B), lambda b, i: (0, 0))],
        out_specs=qo,
        out_shape=jax.ShapeDtypeStruct((N, D_B), BF16),
        scratch_shapes=[pltpu.VMEM((CHUNK, D_B), F32), pltpu.VMEM((D_B // GROUP, CHUNK, LANES), F32)],
        compiler_params=_cparams(("parallel", "arbitrary")), name="stickbreak_attn",
    )(q, k, v, M, ONES, _block_diag_mean(LANES, GROUP), gb)


def _out_kernel(ya_ref, yb_ref, x_ref, w_ref, gt_ref, g2_ref, sc_ref, sh_ref, xo_ref, h2_ref):
    o = _dot(ya_ref[...], w_ref[:D_A, :]) + _dot(yb_ref[...], w_ref[D_A:, :])
    x = x_ref[...] + gt_ref[0] * o
    xo_ref[...] = x
    h2_ref[...] = _rms_mod(x, g2_ref[...], sc_ref[0], sh_ref[0])


def out_call(ya, yb, x, w_out_bf, mod, g2, T, tm=512):
    N = x.shape[0]
    row = pl.BlockSpec((tm, D_MODEL), lambda i: (i, 0))
    half = pl.BlockSpec((tm, D_A), lambda i: (i, 0))
    return pl.pallas_call(
        _out_kernel, grid=(N // tm,),
        in_specs=[half, half, row, pl.BlockSpec(w_out_bf.shape, lambda i: (0, 0)), _mod_spec(tm, T, 2),
                  pl.BlockSpec((1, D_MODEL), lambda i: (0, 0)), _mod_spec(tm, T, 4), _mod_spec(tm, T, 3)],
        out_specs=[row, row], out_shape=[jax.ShapeDtypeStruct((N, D_MODEL), F32)] * 2,
        compiler_params=_cparams(("parallel",)), name="out_proj",
    )(ya, yb, x, w_out_bf, mod, g2, mod, mod)


N_CAND_ROWS = 64


def _topk_cols(s, payload=None):
    n_rows, tb = s.shape
    row_f = lax.broadcasted_iota(jnp.int32, (n_rows, tb), 0).astype(F32)
    rank = lax.broadcasted_iota(jnp.int32, (PEER_TOPK, tb), 0)
    neg = jnp.float32(-jnp.inf)

    def body(r, carry):
        s, vals, second = carry
        m = jnp.max(s, axis=0, keepdims=True)
        pos = jnp.min(jnp.where(s == m, row_f, float(n_rows)), axis=0, keepdims=True)
        hit = row_f == pos
        out = pos if payload is None else jnp.max(jnp.where(hit, payload, -1.0), axis=0, keepdims=True)
        dst = rank == r
        return jnp.where(hit, neg, s), jnp.where(dst, m, vals), jnp.where(dst, out, second)

    zeros = jnp.zeros((PEER_TOPK, tb), F32)
    _, vals, second = lax.fori_loop(0, PEER_TOPK, body, (s, zeros, zeros))
    return vals, second


def _peerq_kernel(h2_ref, wq_ref, k1_ref, k2_ref, sel_ref, sele_ref, idx_ref, g_ref, q_scr, it_scr, gt_scr):
    h = pl.program_id(1)
    tm = h2_ref.shape[0]

    @pl.when(h == 0)
    def _():
        q_scr[...] = _dot(h2_ref[...].astype(BF16), wq_ref[...]).astype(BF16)

    nt = (((1,), (1,)), ((), ()))
    vals, keys = [], []
    for tb in range(tm // LANES):
        qh = q_scr[tb * LANES:(tb + 1) * LANES, pl.ds(pl.multiple_of(h * LANES, LANES), LANES)]
        s12 = jnp.concatenate([lax.dot_general(k1_ref[...], qh, nt, preferred_element_type=F32),
                               lax.dot_general(k2_ref[...], qh, nt, preferred_element_type=F32)], axis=1)
        v12, i12 = _topk_cols(s12)
        vals.append(jnp.concatenate([v12[:, :LANES], v12[:, LANES:]], axis=0))
        keys.append(jnp.concatenate([i12[:, :LANES], i12[:, LANES:]], axis=0))
    vals = jnp.concatenate(vals, axis=1)
    keys = jnp.concatenate(keys, axis=1)
    cand = jnp.dot(sel_ref[...], vals, precision=HIGHEST, preferred_element_type=F32)
    cand_row = lax.broadcasted_iota(jnp.int32, cand.shape, 0)
    cand = jnp.where(cand_row < N_STAIR, cand, -jnp.inf)
    ecand = jnp.dot(sele_ref[...], keys, precision=HIGHEST, preferred_element_type=F32)
    sc, ex = _topk_cols(cand, ecand)
    p = jnp.exp(sc - sc[0:1, :])
    out_rows = pl.ds(pl.multiple_of(h * PEER_TOPK, PEER_TOPK), PEER_TOPK)
    it_scr[out_rows, :] = ex
    gt_scr[out_rows, :] = p / jnp.sum(p, axis=0, keepdims=True)

    @pl.when(h == PEER_HEADS - 1)
    def _():
        idx_ref[...] = it_scr[...].T.astype(jnp.int32)
        g_ref[...] = gt_scr[...].T


def peerq_call(h2, wq_bf, k1, k2, tm=256):
    N = h2.shape[0]
    half = N_KEYS // 2
    k1p = jnp.concatenate([k1, jnp.zeros((N_KEYS, half), F32)], axis=1).astype(BF16)
    k2p = jnp.concatenate([jnp.zeros((N_KEYS, half), F32), k2], axis=1).astype(BF16)
    sel = np.zeros((N_CAND_ROWS, 2 * PEER_TOPK), np.float32)
    sele = np.zeros((N_CAND_ROWS, 2 * PEER_TOPK), np.float32)
    for c, (a, b) in enumerate(_STAIR):
        sel[c, a] = sel[c, PEER_TOPK + b] = 1.0
        sele[c, a] = float(N_KEYS)
        sele[c, PEER_TOPK + b] = 1.0
    const = lambda shape: pl.BlockSpec(shape, lambda i, h: (0, 0))
    row = pl.BlockSpec((tm, HK), lambda i, h: (i, 0))
    return pl.pallas_call(
        _peerq_kernel, grid=(N // tm, PEER_HEADS),
        in_specs=[pl.BlockSpec((tm, D_MODEL), lambda i, h: (i, 0)), const(wq_bf.shape), const((N_KEYS, LANES)),
                  const((N_KEYS, LANES)), const(sel.shape), const(sele.shape)],
        out_specs=[row, row],
        out_shape=[jax.ShapeDtypeStruct((N, HK), jnp.int32), jax.ShapeDtypeStruct((N, HK), F32)],
        scratch_shapes=[pltpu.VMEM((tm, D_MODEL), BF16), pltpu.VMEM((HK, tm), F32), pltpu.VMEM((HK, tm), F32)],
        compiler_params=_cparams(("parallel", "arbitrary")), name="peer_retrieve",
    )(h2, wq_bf, k1p, k2p, jnp.asarray(sel), jnp.asarray(sele))


SC_LANES = 16
SC_CORES = 2
SC_WORKERS = 32
ROWS_PER_GATHER = 16
N_ROW_BUF = 4
LOOKAHEAD = N_ROW_BUF - 1
TOK_BLOCK = 16
ROW_BLOCK = 8
U_UNROLL = 2
COL_BLOCK = 16
N_DCHUNK = D_MODEL // SC_LANES
N_GATHER = HK // ROWS_PER_GATHER


def _sc_gelu(x):
    y = math.sqrt(2.0 / math.pi) * (x + 0.044715 * (x * x * x))
    t = 1.0 - 2.0 / (jnp.exp(2.0 * y) + 1.0)
    return 0.5 * x * (1.0 + t)


def _peer_eval_body(per_w, h_hbm, idx_hbm, g_hbm, u_hbm, v_hbm, o_hbm,
                    idx_blk, g_blk, h_blk, out_blk, w_v, *bufs):
    wid = lax.axis_index("s") * SC_CORES + lax.axis_index("c")
    rows = bufs[:N_ROW_BUF]
    sems = bufs[N_ROW_BUF:]
    lane = lax.iota(jnp.int32, SC_LANES)
    zeros = jnp.zeros((SC_LANES,), F32)

    def gather(table, i, j, buf):
        return pltpu.make_async_copy(table.at[idx_blk.at[i, j]], rows[buf], sems[buf])

    def u_compute(i, n, rbuf):
        for half in range(ROWS_PER_GATHER // SC_LANES):
            actv = zeros
            for rg in range(SC_LANES // ROW_BLOCK):
                r0 = half * SC_LANES + rg * ROW_BLOCK

                def c_body(cb, accs):
                    accs = list(accs)
                    for cc in range(U_UNROLL):
                        sl = pl.ds(pl.multiple_of((cb * U_UNROLL + cc) * SC_LANES, SC_LANES), SC_LANES)
                        hc = h_blk[i, sl]
                        for r in range(ROW_BLOCK):
                            accs[r] = accs[r] + rbuf[r0 + r, sl] * hc
                    return tuple(accs)

                accs = lax.fori_loop(0, N_DCHUNK // U_UNROLL, c_body, (zeros,) * ROW_BLOCK)
                for r in range(ROW_BLOCK):
                    actv = jnp.where(lane == rg * ROW_BLOCK + r, jnp.sum(accs[r]), actv)
            sl = pl.ds(n * ROWS_PER_GATHER + half * SC_LANES, SC_LANES)
            w_v[sl] = g_blk[i, sl] * _sc_gelu(actv)

    def v_compute(i, m, rbuf):
        for cb in range(N_DCHUNK // COL_BLOCK):
            def r_body(r, accs):
                wk = plsc.load_gather(w_v, [jnp.full((SC_LANES,), m * ROWS_PER_GATHER, jnp.int32) + r])
                return tuple(accs[cc] + wk * rbuf[r, pl.ds((cb * COL_BLOCK + cc) * SC_LANES, SC_LANES)]
                             for cc in range(COL_BLOCK))

            accs = lax.fori_loop(0, ROWS_PER_GATHER, r_body, (zeros,) * COL_BLOCK)
            for cc in range(COL_BLOCK):
                sl = pl.ds((cb * COL_BLOCK + cc) * SC_LANES, SC_LANES)
                if m == 0:
                    out_blk[i, sl] = accs[cc]
                else:
                    plsc.addupdate(out_blk.at[i, sl], accs[cc])

    @pl.loop(0, per_w // TOK_BLOCK)
    def _(bi):
        t0 = wid * per_w + bi * TOK_BLOCK
        pltpu.sync_copy(idx_hbm.at[pl.ds(t0, TOK_BLOCK)], idx_blk)
        pltpu.sync_copy(g_hbm.at[pl.ds(t0, TOK_BLOCK)], g_blk)
        pltpu.sync_copy(h_hbm.at[pl.ds(t0, TOK_BLOCK)], h_blk)
        def task_gather(i, n):
            table = u_hbm if n < N_GATHER else v_hbm
            return gather(table, i, n % N_GATHER, n % N_ROW_BUF)

        for n in range(LOOKAHEAD):
            task_gather(0, n).start()

        @pl.loop(0, TOK_BLOCK)
        def _(i):
            for n in range(2 * N_GATHER):
                ahead = n + LOOKAHEAD
                if ahead < 2 * N_GATHER:
                    task_gather(i, ahead).start()
                else:
                    @pl.when(i + 1 < TOK_BLOCK)
                    def _():
                        task_gather(i + 1, ahead - 2 * N_GATHER).start()
                task_gather(i, n).wait()
                if n < N_GATHER:
                    u_compute(i, n, rows[n % N_ROW_BUF])
                else:
                    v_compute(i, n - N_GATHER, rows[n % N_ROW_BUF])

        pltpu.sync_copy(out_blk, o_hbm.at[pl.ds(t0, TOK_BLOCK)])


def peer_eval_call(h2, idx, g, u_tab, v_tab):
    N = h2.shape[0]
    per_w = N // SC_WORKERS
    assert N % (SC_WORKERS * TOK_BLOCK) == 0
    assert (2 * N_GATHER) % N_ROW_BUF == 0 and LOOKAHEAD < N_ROW_BUF
    mesh = plsc.VectorSubcoreMesh(core_axis_name="c", subcore_axis_name="s",
                                  num_cores=SC_CORES, num_subcores=SC_WORKERS // SC_CORES)
    return pl.kernel(
        functools.partial(_peer_eval_body, per_w),
        out_type=jax.ShapeDtypeStruct((N, D_MODEL), F32),
        mesh=mesh,
        scratch_types=[
            pltpu.VMEM((TOK_BLOCK, N_GATHER, ROWS_PER_GATHER), jnp.int32),
            pltpu.VMEM((TOK_BLOCK, HK), F32),
            pltpu.VMEM((TOK_BLOCK, D_MODEL), F32),
            pltpu.VMEM((TOK_BLOCK, D_MODEL), F32),
            pltpu.VMEM((HK,), F32),
        ] + [pltpu.VMEM((ROWS_PER_GATHER, D_MODEL), F32)] * N_ROW_BUF + [pltpu.SemaphoreType.DMA] * N_ROW_BUF,
        compiler_params=pltpu.CompilerParams(needs_layout_passes=False),
        name="peer_eval",
    )(h2, idx.reshape(N, N_GATHER, ROWS_PER_GATHER), g, u_tab, v_tab)


N_EXPERTS = N_KEYS * N_KEYS
E_HALF = N_EXPERTS // 2
TC_TOK_TILE = 16
TC_EVAL_TOKENS = 1024
SUBLANES = 8
TC_EVAL_VMEM = 58 * 1024 * 1024


def _tc_eval_kernel(idx_s, hi_ref, g_ref, h_ref, a_ref, u_hbm, v_hbm, o_ref,
                    tab, actw, outp, slab_hi0, slab_hi1, slab_lo0, slab_lo1, wb_all, sem):
    p = pl.program_id(0)
    i = pl.program_id(1)
    slab_hi = (slab_hi0, slab_hi1)
    slab_lo = (slab_lo0, slab_lo1)
    for ph, (src, half) in enumerate(((u_hbm, 0), (u_hbm, 1), (v_hbm, 0), (v_hbm, 1))):
        @pl.when(jnp.logical_and(p == ph, i == 0))
        def _():
            cp = pltpu.make_async_copy(src.at[pl.ds(half * E_HALF, E_HALF)], tab, sem)
            cp.start()
            cp.wait()

    lane = lax.broadcasted_iota(jnp.int32, (HK, LANES), 1)
    row = lax.broadcasted_iota(jnp.int32, (HK, LANES), 0)
    in_half = hi_ref[0] == (p % 2)

    @pl.when(p < 2)
    def _():
        def products(t, b):
            h = h_ref[t]
            for k2 in range(HK // 2):
                prod = jnp.concatenate([tab[idx_s[t, 2 * k2]] * h, tab[idx_s[t, 2 * k2 + 1]] * h], axis=0)
                hi = prod.astype(BF16)
                rows = slice(2 * SUBLANES * k2, 2 * SUBLANES * (k2 + 1))
                slab_hi[b][rows, :] = hi
                slab_lo[b][rows, :] = (prod - hi.astype(F32)).astype(BF16)

        def reduce(t, b, acts):
            part = _dot(a_ref[...], slab_hi[b][...]) + _dot(a_ref[...], slab_lo[b][...])
            return jnp.where(lane == t, jnp.sum(part, axis=1, keepdims=True), acts)

        def tok_pair(j, acts):
            products(2 * j + 1, 1)
            acts = reduce(2 * j, 0, acts)
            products(2 * j + 2, 0)
            return reduce(2 * j + 1, 1, acts)

        products(0, 0)
        acts = lax.fori_loop(0, TC_TOK_TILE // 2 - 1, tok_pair, jnp.zeros((HK, LANES), F32))
        products(TC_TOK_TILE - 1, 1)
        acts = reduce(TC_TOK_TILE - 2, 0, acts)
        acts = reduce(TC_TOK_TILE - 1, 1, acts)
        part = jnp.where(in_half, acts, 0.0)

        @pl.when(p == 0)
        def _():
            actw[i] = part

        @pl.when(p == 1)
        def _():
            actw[i] = g_ref[0] * jax.nn.gelu(actw[i] + part)

    @pl.when(p >= 2)
    def _():
        wm = jnp.where(in_half, actw[i], 0.0)

        for t in range(TC_TOK_TILE):
            wb_all[t] = jnp.broadcast_to(wm[:, t:t + 1], (HK, LANES))

        def tok(t, _):
            accs = [jnp.zeros((SUBLANES, LANES), F32)] * 4
            for k in range(HK):
                w = jnp.broadcast_to(wb_all[t, pl.ds(k, 1), :], (SUBLANES, LANES))
                accs[k % 4] = accs[k % 4] + w * tab[idx_s[t, k]]
            acc = (accs[0] + accs[1]) + (accs[2] + accs[3])
            tok_id = i * TC_TOK_TILE + t

            @pl.when(p == 2)
            def _():
                outp[tok_id] = acc

            @pl.when(p == 3)
            def _():
                o_ref[t] = outp[tok_id] + acc

            return 0

        lax.fori_loop(0, TC_TOK_TILE, tok, 0)


def peer_eval_tc_call(h2, idx, g, u_tab, v_tab):
    nt = h2.shape[0]
    n_tiles = nt // TC_TOK_TILE
    assert nt % TC_TOK_TILE == 0 and u_tab.shape[0] == N_EXPERTS

    def pairs_on_sublanes(a, fill):
        a = a.reshape(n_tiles, TC_TOK_TILE, HK).transpose(0, 2, 1)
        return jnp.pad(a, ((0, 0), (0, 0), (0, LANES - TC_TOK_TILE)), constant_values=fill)

    sum_rows = (np.arange(HK)[:, None] == np.arange(HK * SUBLANES)[None, :] // SUBLANES).astype(np.float32)
    tile3 = lambda: pl.BlockSpec((1, HK, LANES), lambda p, i: (i, 0, 0))
    out = pl.pallas_call(
        _tc_eval_kernel, grid=(4, n_tiles),
        in_specs=[pl.BlockSpec((TC_TOK_TILE, HK), lambda p, i: (i, 0), memory_space=pltpu.SMEM),
                  tile3(), tile3(),
                  pl.BlockSpec((TC_TOK_TILE, SUBLANES, LANES), lambda p, i: (i, 0, 0)),
                  pl.BlockSpec((HK, HK * SUBLANES), lambda p, i: (0, 0)),
                  pl.BlockSpec(memory_space=pl.ANY), pl.BlockSpec(memory_space=pl.ANY)],
        out_specs=pl.BlockSpec((TC_TOK_TILE, SUBLANES, LANES), lambda p, i: (jnp.where(p == 3, i, 0), 0, 0)),
        out_shape=jax.ShapeDtypeStruct((nt, SUBLANES, LANES), F32),
        scratch_shapes=[pltpu.VMEM((E_HALF, SUBLANES, LANES), F32),
                        pltpu.VMEM((n_tiles, HK, LANES), F32),
                        pltpu.VMEM((nt, SUBLANES, LANES), F32),
                        pltpu.VMEM((HK * SUBLANES, LANES), BF16), pltpu.VMEM((HK * SUBLANES, LANES), BF16),
                        pltpu.VMEM((HK * SUBLANES, LANES), BF16), pltpu.VMEM((HK * SUBLANES, LANES), BF16),
                        pltpu.VMEM((TC_TOK_TILE, HK, LANES), F32),
                        pltpu.SemaphoreType.DMA],
        compiler_params=pltpu.CompilerParams(dimension_semantics=("arbitrary", "arbitrary"),
                                             vmem_limit_bytes=TC_EVAL_VMEM),
        name="peer_eval_tc",
    )(idx & (E_HALF - 1), pairs_on_sublanes(idx // E_HALF, 2), pairs_on_sublanes(g, 0.0),
      h2.reshape(nt, SUBLANES, LANES), jnp.asarray(sum_rows, BF16),
      u_tab.reshape(N_EXPERTS, SUBLANES, LANES), v_tab.reshape(N_EXPERTS, SUBLANES, LANES))
    return out.reshape(nt, D_MODEL)


def _final_kernel(x_ref, y_ref, gt_ref, g_ref, o_ref):
    x = x_ref[...] + gt_ref[0] * y_ref[...]
    ms = jnp.mean(x * x, axis=-1, keepdims=True)
    o_ref[...] = x * lax.rsqrt(ms + EPS) * g_ref[...]


def final_call(x, y, mod, gf, T, tm=512):
    N = x.shape[0]
    row = pl.BlockSpec((tm, D_MODEL), lambda i: (i, 0))
    return pl.pallas_call(
        _final_kernel, grid=(N // tm,),
        in_specs=[row, row, _mod_spec(tm, T, 5), pl.BlockSpec((1, D_MODEL), lambda i: (0, 0))],
        out_specs=row, out_shape=jax.ShapeDtypeStruct((N, D_MODEL), F32),
        compiler_params=_cparams(("parallel",)), name="final_norm",
    )(x, y, mod, gf)


def kernel(x, c, ada_w, ada_b, norm1_g, norm2_g, w_in, sgu_w, sgu_b, out_norm_a, out_norm_b, w_out,
           peer_wq, peer_k1, peer_k2, peer_u, peer_v, final_g):
    B, T, Dm = x.shape
    L = ada_w.shape[0]
    N = B * T
    mods = ada_mod(c, ada_w, ada_b).reshape(L, B * 6, 1, Dm)
    Bg = B // BATCH_GROUPS
    gmods = [[mods[l, g * Bg * 6:(g + 1) * Bg * 6] for g in range(BATCH_GROUPS)] for l in range(L)]
    xs = [x[g * Bg:(g + 1) * Bg].reshape(Bg * T, Dm) for g in range(BATCH_GROUPS)]
    ys = [None] * BATCH_GROUPS
    for l in range(L):
        w_in_bf, w_out_bf, wq_bf = w_in[l].astype(BF16), w_out[l].astype(BF16), peer_wq[l].astype(BF16)
        bias_full = jnp.repeat(sgu_b[l].T, GROUP, axis=1)
        for g in range(BATCH_GROUPS):
            mod = gmods[l][g]
            outs = proj_call(xs[g], ys[g], gmods[l - 1][g] if l else None, mod, norm1_g[l].reshape(1, Dm), w_in_bf, T)
            if l:
                xs[g], outs = outs[0], outs[1:]
            ua, va, q, k, v = outs
            ya = gmlp_call(ua, va, sgu_w[l], bias_full, out_norm_a[l].reshape(1, D_A))
            yb = attn_call(q, k, v, out_norm_b[l].reshape(1, D_B), Bg, T)
            xs[g], h2 = out_call(ya, yb, xs[g], w_out_bf, mod, norm2_g[l].reshape(1, Dm), T)
            idx, gate = peerq_call(h2, wq_bf, peer_k1[l], peer_k2[l])
            n_sc = h2.shape[0] - TC_EVAL_TOKENS
            y_sc = peer_eval_call(h2[:n_sc], idx[:n_sc], gate[:n_sc], peer_u[l], peer_v[l])
            y_tc = peer_eval_tc_call(h2[n_sc:], idx[n_sc:], gate[n_sc:], peer_u[l], peer_v[l])
            ys[g] = jnp.concatenate([y_sc, y_tc], axis=0)
    outs = [final_call(xs[g], ys[g], gmods[L - 1][g], final_g.reshape(1, Dm), T) for g in range(BATCH_GROUPS)]
    return jnp.concatenate(outs, axis=0).reshape(B, T, Dm)
```

```python
import functools
import math

import numpy as np
import jax
import jax.numpy as jnp
from jax import lax
from jax.experimental import pallas as pl
from jax.experimental.pallas import tpu as pltpu
from jax.experimental.pallas import tpu_sc as plsc

F32 = jnp.float32
BF16 = jnp.bfloat16
HIGHEST = lax.Precision.HIGHEST

D_MODEL = 1024
D_A = 512
D_B = 512
GROUP = 64
CHUNK = 128
N_KEYS = 128
PEER_HEADS = 8
PEER_TOPK = 16
HK = PEER_HEADS * PEER_TOPK
EPS = 1e-6
LANES = 128
VMEM_LIMIT = 48 * 1024 * 1024
BATCH_GROUPS = 4

_STAIR = [(a, b) for a in range(PEER_TOPK) for b in range(PEER_TOPK) if (a + 1) * (b + 1) <= PEER_TOPK]
N_STAIR = len(_STAIR)


def _cparams(sem):
    return pltpu.CompilerParams(dimension_semantics=sem, vmem_limit_bytes=VMEM_LIMIT)


def _dot(a, b):
    return jnp.dot(a, b, preferred_element_type=F32)


def _block_diag_mean(n, group):
    i = np.arange(n)
    return jnp.asarray((i[:, None] // group == i[None, :] // group).astype(np.float32) / group, BF16)


def _ada_kernel(c_ref, w_ref, b_ref, o_ref):
    c = c_ref[...]
    ca = c * jax.nn.sigmoid(c)
    o_ref[0] = jnp.dot(ca, w_ref[0], precision=HIGHEST, preferred_element_type=F32) + b_ref[0]


def ada_mod(c, ada_w, ada_b):
    L, Dm, E = ada_w.shape
    Bc = c.shape[0]
    tn = 1536
    return pl.pallas_call(
        _ada_kernel,
        grid=(L, E // tn),
        in_specs=[pl.BlockSpec((Bc, Dm), lambda l, j: (0, 0)),
                  pl.BlockSpec((1, Dm, tn), lambda l, j: (l, 0, j)),
                  pl.BlockSpec((1, 1, tn), lambda l, j: (l, 0, j))],
        out_specs=pl.BlockSpec((1, Bc, tn), lambda l, j: (l, 0, j)),
        out_shape=jax.ShapeDtypeStruct((L, Bc, E), F32),
        compiler_params=_cparams(("parallel", "parallel")),
        name="ada_mod",
    )(c, ada_w, ada_b.reshape(L, 1, E))


def _mod_spec(tm, T, j):
    return pl.BlockSpec((1, 1, D_MODEL), lambda i: (((i * tm) // T) * 6 + j, 0, 0))


def _rms_mod(x, g, sc, sh):
    ms = jnp.mean(x * x, axis=-1, keepdims=True)
    return (x * lax.rsqrt(ms + EPS) * g) * (1.0 + sc) + sh


def _proj_kernel(has_y, *refs):
    if has_y:
        x_ref, y_ref, gt_ref, g_ref, sc_ref, sh_ref, w_ref, xo_ref, ua_ref, va_ref, q_ref, k_ref, v_ref = refs
        x = x_ref[...] + gt_ref[0] * y_ref[...]
        xo_ref[...] = x
    else:
        x_ref, g_ref, sc_ref, sh_ref, w_ref, ua_ref, va_ref, q_ref, k_ref, v_ref = refs
        x = x_ref[...]
    h = _rms_mod(x, g_ref[...], sc_ref[0], sh_ref[0]).astype(BF16)
    outs = (ua_ref, va_ref, q_ref, k_ref, v_ref)
    for j, o_ref in enumerate(outs):
        p = _dot(h, w_ref[:, j * D_A:(j + 1) * D_A])
        if j < 2:
            p = jax.nn.gelu(p)
        o_ref[...] = p.astype(o_ref.dtype)


def proj_call(x, y, mod_prev, mod, g1, w_in_bf, T, tm=512):
    N = x.shape[0]
    has_y = y is not None
    row = pl.BlockSpec((tm, D_MODEL), lambda i: (i, 0))
    half = pl.BlockSpec((tm, D_A), lambda i: (i, 0))
    in_specs = [row]
    args = [x]
    if has_y:
        in_specs += [row, _mod_spec(tm, T, 5)]
        args += [y, mod_prev]
    in_specs += [pl.BlockSpec((1, D_MODEL), lambda i: (0, 0)), _mod_spec(tm, T, 1), _mod_spec(tm, T, 0),
                 pl.BlockSpec(w_in_bf.shape, lambda i: (0, 0))]
    args += [g1, mod, mod, w_in_bf]
    out_specs = [half] * 5
    out_shape = [jax.ShapeDtypeStruct((N, D_A), F32)] * 2 + [jax.ShapeDtypeStruct((N, D_A), BF16)] * 3
    if has_y:
        out_specs = [row] + out_specs
        out_shape = [jax.ShapeDtypeStruct((N, D_MODEL), F32)] + out_shape
    return pl.pallas_call(
        functools.partial(_proj_kernel, has_y),
        grid=(N // tm,), in_specs=in_specs, out_specs=out_specs, out_shape=out_shape,
        compiler_params=_cparams(("parallel",)), name="proj",
    )(*args)


def _gmlp_kernel(ua_ref, va_ref, w_ref, bias_ref, a_ref, ga_ref, o_ref):
    A = a_ref[...]
    row = lax.broadcasted_iota(jnp.int32, (CHUNK, CHUNK), 0)
    col = lax.broadcasted_iota(jnp.int32, (CHUNK, CHUNK), 1)
    causal = row >= col
    first_group = col < GROUP
    ws = [jnp.where(causal, w_ref[g], 0.0).astype(BF16) for g in range(D_A // GROUP)]
    bias = bias_ref[...]
    ga = ga_ref[...]
    for c in range(ua_ref.shape[0] // CHUNK):
        rows = slice(c * CHUNK, (c + 1) * CHUNK)
        v = va_ref[rows, :]
        v_hi = v.astype(BF16)
        v_lo = (v - v_hi.astype(F32)).astype(BF16)
        d = v - (_dot(v_hi, A) + _dot(v_lo, A))
        var = _dot((d * d).astype(BF16), A)
        vn = (d * lax.rsqrt(var + EPS)).astype(BF16)
        parts = []
        for p in range(D_A // LANES):
            vp = vn[:, p * LANES:(p + 1) * LANES]
            parts.append(jnp.where(first_group, _dot(ws[2 * p], vp), _dot(ws[2 * p + 1], vp)))
        s = jnp.concatenate(parts, axis=1) + bias
        y = ua_ref[rows, :] * s
        ms = _dot((y * y).astype(BF16), A)
        o_ref[rows, :] = (y * lax.rsqrt(ms + EPS) * ga).astype(BF16)


def gmlp_call(ua, va, sgu_w, bias_full, ga, tm=512):
    N = ua.shape[0]
    half = pl.BlockSpec((tm, D_A), lambda i: (i, 0))
    return pl.pallas_call(
        _gmlp_kernel, grid=(N // tm,),
        in_specs=[half, half,
                  pl.BlockSpec(sgu_w.shape, lambda i: (0, 0, 0)),
                  pl.BlockSpec((CHUNK, D_A), lambda i: (0, 0)),
                  pl.BlockSpec((D_A, D_A), lambda i: (0, 0)),
                  pl.BlockSpec((1, D_A), lambda i: (0, 0))],
        out_specs=half, out_shape=jax.ShapeDtypeStruct((N, D_A), BF16),
        compiler_params=_cparams(("parallel",)), name="gmlp",
    )(ua, va, sgu_w, bias_full, _block_diag_mean(D_A, GROUP), ga)


EXP_UNDERFLOW = -104.0


def _attn_kernel(q_ref, k_ref, v_ref, mo_ref, a_ref, gb_ref, o_ref, *scratch):
    i = pl.program_id(1)
    n_pair = D_B // LANES
    lane = lax.broadcasted_iota(jnp.int32, (CHUNK, LANES), 1)
    row = lax.broadcasted_iota(jnp.int32, (CHUNK, LANES), 0)
    head_lanes = (lane < GROUP, lane >= GROUP)
    MO = mo_ref[...]
    scale = 1.0 / math.sqrt(GROUP)
    acc_refs, carry_refs, qs_refs = scratch[:n_pair], scratch[n_pair:2 * n_pair], scratch[2 * n_pair:]
    for p in range(n_pair):
        acc_refs[p][...] = jnp.zeros_like(acc_refs[p])
        carry_refs[p][...] = jnp.zeros_like(carry_refs[p])
        q = q_ref[:, p * LANES:(p + 1) * LANES]
        qs_refs[p][...] = jnp.concatenate([jnp.where(hl, q, jnp.zeros_like(q)) for hl in head_lanes], axis=0)

    def cond(state):
        jj, cmax = state
        return jnp.logical_and(jj <= i, cmax > EXP_UNDERFLOW)

    def body(state):
        jj, _ = state
        j = i - jj
        start = pl.multiple_of(j * CHUNK, CHUNK)
        mask1 = (j * CHUNK + lane) < (i * CHUNK + row)
        mask = jnp.concatenate([mask1, mask1], axis=0)
        cmax = jnp.full((2 * CHUNK, LANES), -jnp.inf, F32)
        slabs = range(n_pair)
        cols = [slice(p * LANES, (p + 1) * LANES) for p in slabs]
        zs = [lax.dot_general(qs_refs[p][...], k_ref[pl.ds(start, CHUNK), cols[p]], (((1,), (1,)), ((), ())),
                              preferred_element_type=F32) * scale for p in slabs]
        lszs = [jnp.minimum(z, 0.0) - jnp.log1p(jnp.exp(-jnp.abs(z))) for z in zs]
        trs = []
        for p in slabs:
            L = jnp.where(mask, lszs[p] - zs[p], 0.0)
            L_hi = L.astype(BF16)
            L_lo = (L - L_hi.astype(F32)).astype(BF16)
            tr = _dot(jnp.concatenate([L_hi, L_lo], axis=0), MO)
            trs.append(tr[:2 * CHUNK] + tr[2 * CHUNK:])
        for p in slabs:
            c = carry_refs[p][...]
            a = jnp.where(mask, jnp.exp(lszs[p] + trs[p][:, :LANES] + c), 0.0).astype(BF16)
            a_cat = jnp.concatenate([a[:CHUNK], a[CHUNK:]], axis=1)
            vs = v_ref[pl.ds(start, CHUNK), cols[p]]
            v_cat = jnp.concatenate([jnp.where(hl, vs, jnp.zeros_like(vs)) for hl in head_lanes], axis=0)
            acc_refs[p][...] += _dot(a_cat, v_cat)
            c = c + trs[p][:, LANES:]
            carry_refs[p][...] = c
            cmax = jnp.maximum(cmax, c)
        return jj + 1, jnp.max(cmax)

    lax.while_loop(cond, body, (jnp.int32(0), jnp.float32(0.0)))
    for p in range(n_pair):
        cols = slice(p * LANES, (p + 1) * LANES)
        acc = acc_refs[p][...]
        ms = _dot((acc * acc).astype(BF16), a_ref[...])
        o_ref[:, cols] = (acc * lax.rsqrt(ms + EPS) * gb_ref[:, cols]).astype(BF16)


def attn_call(q, k, v, gb, B, T):
    N = q.shape[0]
    nq = T // CHUNK
    i = np.arange(CHUNK)
    MO = jnp.asarray(np.concatenate([(i[:, None] > i[None, :]), np.ones((CHUNK, LANES), bool)], axis=1)
                     .astype(np.float32), BF16)
    kv = pl.BlockSpec((T, D_B), lambda b, i: (b, 0))
    qo = pl.BlockSpec((CHUNK, D_B), lambda b, i: (b * nq + i, 0))
    n_pair = D_B // LANES
    return pl.pallas_call(
        _attn_kernel, grid=(B, nq),
        in_specs=[qo, kv, kv, pl.BlockSpec((CHUNK, 2 * LANES), lambda b, i: (0, 0)),
                  pl.BlockSpec((CHUNK, LANES), lambda b, i: (0, 0)), pl.BlockSpec((1, D_B), lambda b, i: (0, 0))],
        out_specs=qo,
        out_shape=jax.ShapeDtypeStruct((N, D_B), BF16),
        scratch_shapes=([pltpu.VMEM((CHUNK, LANES), F32)] * n_pair + [pltpu.VMEM((2 * CHUNK, LANES), F32)] * n_pair
                        + [pltpu.VMEM((2 * CHUNK, LANES), BF16)] * n_pair),
        compiler_params=_cparams(("parallel", "arbitrary")), name="stickbreak_attn",
    )(q, k, v, MO, _block_diag_mean(LANES, GROUP), gb)


def _out_kernel(ya_ref, yb_ref, x_ref, w_ref, gt_ref, g2_ref, sc_ref, sh_ref, xo_ref, h2_ref):
    o = _dot(ya_ref[...], w_ref[:D_A, :]) + _dot(yb_ref[...], w_ref[D_A:, :])
    x = x_ref[...] + gt_ref[0] * o
    xo_ref[...] = x
    h2_ref[...] = _rms_mod(x, g2_ref[...], sc_ref[0], sh_ref[0])


def out_call(ya, yb, x, w_out_bf, mod, g2, T, tm=512):
    N = x.shape[0]
    row = pl.BlockSpec((tm, D_MODEL), lambda i: (i, 0))
    half = pl.BlockSpec((tm, D_A), lambda i: (i, 0))
    return pl.pallas_call(
        _out_kernel, grid=(N // tm,),
        in_specs=[half, half, row, pl.BlockSpec(w_out_bf.shape, lambda i: (0, 0)), _mod_spec(tm, T, 2),
                  pl.BlockSpec((1, D_MODEL), lambda i: (0, 0)), _mod_spec(tm, T, 4), _mod_spec(tm, T, 3)],
        out_specs=[row, row], out_shape=[jax.ShapeDtypeStruct((N, D_MODEL), F32)] * 2,
        compiler_params=_cparams(("parallel",)), name="out_proj",
    )(ya, yb, x, w_out_bf, mod, g2, mod, mod)


N_CAND_ROWS = 64


def _topk_cols(s, payload=None):
    n_rows, tb = s.shape
    row_f = lax.broadcasted_iota(jnp.int32, (n_rows, tb), 0).astype(F32)
    rank = lax.broadcasted_iota(jnp.int32, (PEER_TOPK, tb), 0)
    neg = jnp.float32(-jnp.inf)

    def body(r, carry):
        s, vals, second = carry
        m = jnp.max(s, axis=0, keepdims=True)
        pos = jnp.min(jnp.where(s == m, row_f, float(n_rows)), axis=0, keepdims=True)
        hit = row_f == pos
        out = pos if payload is None else jnp.max(jnp.where(hit, payload, -1.0), axis=0, keepdims=True)
        dst = rank == r
        return jnp.where(hit, neg, s), jnp.where(dst, m, vals), jnp.where(dst, out, second)

    zeros = jnp.zeros((PEER_TOPK, tb), F32)
    _, vals, second = lax.fori_loop(0, PEER_TOPK, body, (s, zeros, zeros))
    return vals, second


def _peerq_kernel(h2_ref, wq_ref, k1_ref, k2_ref, sel_ref, sele_ref, idx_ref, g_ref, q_scr, it_scr, gt_scr):
    h = pl.program_id(1)
    tm = h2_ref.shape[0]

    @pl.when(h == 0)
    def _():
        q_scr[...] = _dot(h2_ref[...].astype(BF16), wq_ref[...]).astype(BF16)

    nt = (((1,), (1,)), ((), ()))
    vals, keys = [], []
    for tb in range(tm // LANES):
        qh = q_scr[tb * LANES:(tb + 1) * LANES, pl.ds(pl.multiple_of(h * LANES, LANES), LANES)]
        s12 = jnp.concatenate([lax.dot_general(k1_ref[...], qh, nt, preferred_element_type=F32),
                               lax.dot_general(k2_ref[...], qh, nt, preferred_element_type=F32)], axis=1)
        v12, i12 = _topk_cols(s12)
        vals.append(jnp.concatenate([v12[:, :LANES], v12[:, LANES:]], axis=0))
        keys.append(jnp.concatenate([i12[:, :LANES], i12[:, LANES:]], axis=0))
    vals = jnp.concatenate(vals, axis=1)
    keys = jnp.concatenate(keys, axis=1)
    cand = jnp.dot(sel_ref[...], vals, precision=HIGHEST, preferred_element_type=F32)
    cand_row = lax.broadcasted_iota(jnp.int32, cand.shape, 0)
    cand = jnp.where(cand_row < N_STAIR, cand, -jnp.inf)
    ecand = jnp.dot(sele_ref[...], keys, precision=HIGHEST, preferred_element_type=F32)
    sc, ex = _topk_cols(cand, ecand)
    p = jnp.exp(sc - sc[0:1, :])
    out_rows = pl.ds(pl.multiple_of(h * PEER_TOPK, PEER_TOPK), PEER_TOPK)
    it_scr[out_rows, :] = ex
    gt_scr[out_rows, :] = p / jnp.sum(p, axis=0, keepdims=True)

    @pl.when(h == PEER_HEADS - 1)
    def _():
        idx_ref[...] = it_scr[...].T.astype(jnp.int32)
        g_ref[...] = gt_scr[...].T


def peerq_call(h2, wq_bf, k1, k2, tm=256):
    N = h2.shape[0]
    half = N_KEYS // 2
    k1p = jnp.concatenate([k1, jnp.zeros((N_KEYS, half), F32)], axis=1).astype(BF16)
    k2p = jnp.concatenate([jnp.zeros((N_KEYS, half), F32), k2], axis=1).astype(BF16)
    sel = np.zeros((N_CAND_ROWS, 2 * PEER_TOPK), np.float32)
    sele = np.zeros((N_CAND_ROWS, 2 * PEER_TOPK), np.float32)
    for c, (a, b) in enumerate(_STAIR):
        sel[c, a] = sel[c, PEER_TOPK + b] = 1.0
        sele[c, a] = float(N_KEYS)
        sele[c, PEER_TOPK + b] = 1.0
    const = lambda shape: pl.BlockSpec(shape, lambda i, h: (0, 0))
    row = pl.BlockSpec((tm, HK), lambda i, h: (i, 0))
    return pl.pallas_call(
        _peerq_kernel, grid=(N // tm, PEER_HEADS),
        in_specs=[pl.BlockSpec((tm, D_MODEL), lambda i, h: (i, 0)), const(wq_bf.shape), const((N_KEYS, LANES)),
                  const((N_KEYS, LANES)), const(sel.shape), const(sele.shape)],
        out_specs=[row, row],
        out_shape=[jax.ShapeDtypeStruct((N, HK), jnp.int32), jax.ShapeDtypeStruct((N, HK), F32)],
        scratch_shapes=[pltpu.VMEM((tm, D_MODEL), BF16), pltpu.VMEM((HK, tm), F32), pltpu.VMEM((HK, tm), F32)],
        compiler_params=_cparams(("parallel", "arbitrary")), name="peer_retrieve",
    )(h2, wq_bf, k1p, k2p, jnp.asarray(sel), jnp.asarray(sele))


SC_LANES = 16
SC_CORES = 2
SC_WORKERS = 32
ROWS_PER_GATHER = 16
N_ROW_BUF = 4
LOOKAHEAD = N_ROW_BUF - 1
TOK_BLOCK = 16
ROW_BLOCK = 8
U_UNROLL = 2
COL_BLOCK = 16
N_DCHUNK = D_MODEL // SC_LANES
N_GATHER = HK // ROWS_PER_GATHER


def _sc_gelu(x):
    y = math.sqrt(2.0 / math.pi) * (x + 0.044715 * (x * x * x))
    t = 1.0 - 2.0 / (jnp.exp(2.0 * y) + 1.0)
    return 0.5 * x * (1.0 + t)


def _peer_eval_body(per_w, h_hbm, idx_hbm, g_hbm, u_hbm, v_hbm, o_hbm,
                    idx_blk, g_blk, h_blk, out_blk, w_v, *bufs):
    wid = lax.axis_index("s") * SC_CORES + lax.axis_index("c")
    rows = bufs[:N_ROW_BUF]
    sems = bufs[N_ROW_BUF:]
    lane = lax.iota(jnp.int32, SC_LANES)
    zeros = jnp.zeros((SC_LANES,), F32)

    def gather(table, i, j, buf):
        return pltpu.make_async_copy(table.at[idx_blk.at[i, j]], rows[buf], sems[buf])

    def u_compute(i, n, rbuf):
        for half in range(ROWS_PER_GATHER // SC_LANES):
            actv = zeros
            for rg in range(SC_LANES // ROW_BLOCK):
                r0 = half * SC_LANES + rg * ROW_BLOCK

                def c_body(cb, accs):
                    accs = list(accs)
                    for cc in range(U_UNROLL):
                        sl = pl.ds(pl.multiple_of((cb * U_UNROLL + cc) * SC_LANES, SC_LANES), SC_LANES)
                        hc = h_blk[i, sl]
                        for r in range(ROW_BLOCK):
                            accs[r] = accs[r] + rbuf[r0 + r, sl] * hc
                    return tuple(accs)

                accs = lax.fori_loop(0, N_DCHUNK // U_UNROLL, c_body, (zeros,) * ROW_BLOCK)
                for r in range(ROW_BLOCK):
                    actv = jnp.where(lane == rg * ROW_BLOCK + r, jnp.sum(accs[r]), actv)
            sl = pl.ds(n * ROWS_PER_GATHER + half * SC_LANES, SC_LANES)
            w_v[sl] = g_blk[i, sl] * _sc_gelu(actv)

    def v_compute(i, m, rbuf):
        for cb in range(N_DCHUNK // COL_BLOCK):
            def r_body(r, accs):
                wk = plsc.load_gather(w_v, [jnp.full((SC_LANES,), m * ROWS_PER_GATHER, jnp.int32) + r])
                return tuple(accs[cc] + wk * rbuf[r, pl.ds((cb * COL_BLOCK + cc) * SC_LANES, SC_LANES)]
                             for cc in range(COL_BLOCK))

            accs = lax.fori_loop(0, ROWS_PER_GATHER, r_body, (zeros,) * COL_BLOCK)
            for cc in range(COL_BLOCK):
                sl = pl.ds((cb * COL_BLOCK + cc) * SC_LANES, SC_LANES)
                if m == 0:
                    out_blk[i, sl] = accs[cc]
                else:
                    plsc.addupdate(out_blk.at[i, sl], accs[cc])

    @pl.loop(0, per_w // TOK_BLOCK)
    def _(bi):
        t0 = wid * per_w + bi * TOK_BLOCK
        pltpu.sync_copy(idx_hbm.at[pl.ds(t0, TOK_BLOCK)], idx_blk)
        pltpu.sync_copy(g_hbm.at[pl.ds(t0, TOK_BLOCK)], g_blk)
        pltpu.sync_copy(h_hbm.at[pl.ds(t0, TOK_BLOCK)], h_blk)
        def task_gather(i, n):
            table = u_hbm if n < N_GATHER else v_hbm
            return gather(table, i, n % N_GATHER, n % N_ROW_BUF)

        for n in range(LOOKAHEAD):
            task_gather(0, n).start()

        @pl.loop(0, TOK_BLOCK)
        def _(i):
            for n in range(2 * N_GATHER):
                ahead = n + LOOKAHEAD
                if ahead < 2 * N_GATHER:
                    task_gather(i, ahead).start()
                else:
                    @pl.when(i + 1 < TOK_BLOCK)
                    def _():
                        task_gather(i + 1, ahead - 2 * N_GATHER).start()
                task_gather(i, n).wait()
                if n < N_GATHER:
                    u_compute(i, n, rows[n % N_ROW_BUF])
                else:
                    v_compute(i, n - N_GATHER, rows[n % N_ROW_BUF])

        pltpu.sync_copy(out_blk, o_hbm.at[pl.ds(t0, TOK_BLOCK)])


def peer_eval_call(h2, idx, g, u_tab, v_tab, n_tok=None):
    N = h2.shape[0] if n_tok is None else n_tok
    per_w = N // SC_WORKERS
    assert N % (SC_WORKERS * TOK_BLOCK) == 0
    assert (2 * N_GATHER) % N_ROW_BUF == 0 and LOOKAHEAD < N_ROW_BUF
    mesh = plsc.VectorSubcoreMesh(core_axis_name="c", subcore_axis_name="s",
                                  num_cores=SC_CORES, num_subcores=SC_WORKERS // SC_CORES)
    return pl.kernel(
        functools.partial(_peer_eval_body, per_w),
        out_type=jax.ShapeDtypeStruct((N, D_MODEL), F32),
        mesh=mesh,
        scratch_types=[
            pltpu.VMEM((TOK_BLOCK, N_GATHER, ROWS_PER_GATHER), jnp.int32),
            pltpu.VMEM((TOK_BLOCK, HK), F32),
            pltpu.VMEM((TOK_BLOCK, D_MODEL), F32),
            pltpu.VMEM((TOK_BLOCK, D_MODEL), F32),
            pltpu.VMEM((HK,), F32),
        ] + [pltpu.VMEM((ROWS_PER_GATHER, D_MODEL), F32)] * N_ROW_BUF + [pltpu.SemaphoreType.DMA] * N_ROW_BUF,
        compiler_params=pltpu.CompilerParams(needs_layout_passes=False),
        name="peer_eval",
    )(h2, idx.reshape(h2.shape[0], N_GATHER, ROWS_PER_GATHER), g, u_tab, v_tab)


N_EXPERTS = N_KEYS * N_KEYS
E_HALF = N_EXPERTS // 2
TC_TOK_TILE = 16
TC_EVAL_TOKENS = 1536
SUBLANES = 8
TC_EVAL_VMEM = 58 * 1024 * 1024


def _tc_eval_kernel(idx_s, hi_ref, g_ref, h_ref, u_hbm, v_hbm, o_ref, tab, actw, outp, rsum, wb_all, stage, sem):
    p = pl.program_id(0)
    i = pl.program_id(1)
    for ph, (src, half) in enumerate(((u_hbm, 0), (u_hbm, 1), (v_hbm, 0), (v_hbm, 1))):
        @pl.when(jnp.logical_and(p == ph, i == 0))
        def _():
            cp = pltpu.make_async_copy(src.at[pl.ds(half * E_HALF, E_HALF)], tab, sem)
            cp.start()
            cp.wait()

    lane = lax.broadcasted_iota(jnp.int32, (HK, LANES), 1)
    row = lax.broadcasted_iota(jnp.int32, (HK, LANES), 0)
    in_half = hi_ref[0] == (p % 2)

    @pl.when(p < 2)
    def _():
        acts = jnp.zeros((HK, LANES), F32)
        for t in range(TC_TOK_TILE):
            h = h_ref[t]
            for k in range(HK):
                rsum[t, pl.ds(k, 1), :] = jnp.sum(tab[idx_s[t, k]] * h, axis=0, keepdims=True)
            acts = jnp.where(lane == t, jnp.sum(rsum[t], axis=1, keepdims=True), acts)
        part = jnp.where(in_half, acts, 0.0)

        @pl.when(p == 0)
        def _():
            actw[i] = part

        @pl.when(p == 1)
        def _():
            actw[i] = g_ref[0] * jax.nn.gelu(actw[i] + part)

    @pl.when(p >= 2)
    def _():
        wm = jnp.where(in_half, actw[i], 0.0)

        for t in range(TC_TOK_TILE):
            wb_all[t] = jnp.broadcast_to(wm[:, t:t + 1], (HK, LANES))

        for t in range(TC_TOK_TILE):
            accs = [jnp.zeros((SUBLANES, LANES), F32)] * 4
            for k in range(HK):
                w = jnp.broadcast_to(wb_all[t, pl.ds(k, 1), :], (SUBLANES, LANES))
                accs[k % 4] = accs[k % 4] + w * tab[idx_s[t, k]]
            stage[t] = (accs[0] + accs[1]) + (accs[2] + accs[3])
        tile_toks = pl.ds(pl.multiple_of(i * TC_TOK_TILE, TC_TOK_TILE), TC_TOK_TILE)

        @pl.when(p == 2)
        def _():
            outp[tile_toks] = stage[...]

        @pl.when(p == 3)
        def _():
            o_ref[...] = outp[tile_toks] + stage[...]


def peer_eval_tc_call(h2, idx, g, u_tab, v_tab, first_tok=0):
    n_all = h2.shape[0]
    nt = n_all - first_tok
    n_tiles = nt // TC_TOK_TILE
    first_tile = first_tok // TC_TOK_TILE
    assert nt % TC_TOK_TILE == 0 and first_tok % TC_TOK_TILE == 0 and u_tab.shape[0] == N_EXPERTS
    idx, g = idx[first_tok:], g[first_tok:]

    def pairs_on_sublanes(a, fill):
        a = a.reshape(n_tiles, TC_TOK_TILE, HK).transpose(0, 2, 1)
        return jnp.pad(a, ((0, 0), (0, 0), (0, LANES - TC_TOK_TILE)), constant_values=fill)

    tile3 = lambda: pl.BlockSpec((1, HK, LANES), lambda p, i: (i, 0, 0))
    out = pl.pallas_call(
        _tc_eval_kernel, grid=(4, n_tiles),
        in_specs=[pl.BlockSpec((TC_TOK_TILE, HK), lambda p, i: (i, 0), memory_space=pltpu.SMEM),
                  tile3(), tile3(),
                  pl.BlockSpec((TC_TOK_TILE, SUBLANES, LANES), lambda p, i: (first_tile + i, 0, 0)),
                  pl.BlockSpec(memory_space=pl.ANY), pl.BlockSpec(memory_space=pl.ANY)],
        out_specs=pl.BlockSpec((TC_TOK_TILE, SUBLANES, LANES), lambda p, i: (jnp.where(p == 3, i, 0), 0, 0)),
        out_shape=jax.ShapeDtypeStruct((nt, SUBLANES, LANES), F32),
        scratch_shapes=[pltpu.VMEM((E_HALF, SUBLANES, LANES), F32),
                        pltpu.VMEM((n_tiles, HK, LANES), F32),
                        pltpu.VMEM((nt, SUBLANES, LANES), F32),
                        pltpu.VMEM((TC_TOK_TILE, HK, LANES), F32),
                        pltpu.VMEM((TC_TOK_TILE, HK, LANES), F32),
                        pltpu.VMEM((TC_TOK_TILE, SUBLANES, LANES), F32),
                        pltpu.SemaphoreType.DMA],
        compiler_params=pltpu.CompilerParams(dimension_semantics=("arbitrary", "arbitrary"),
                                             vmem_limit_bytes=TC_EVAL_VMEM),
        name="peer_eval_tc",
    )(idx & (E_HALF - 1), pairs_on_sublanes(idx // E_HALF, 2), pairs_on_sublanes(g, 0.0),
      h2.reshape(n_all, SUBLANES, LANES),
      u_tab.reshape(N_EXPERTS, SUBLANES, LANES), v_tab.reshape(N_EXPERTS, SUBLANES, LANES))
    return out.reshape(nt, D_MODEL)


def _final_kernel(x_ref, y_ref, gt_ref, g_ref, o_ref):
    x = x_ref[...] + gt_ref[0] * y_ref[...]
    ms = jnp.mean(x * x, axis=-1, keepdims=True)
    o_ref[...] = x * lax.rsqrt(ms + EPS) * g_ref[...]


def final_call(x, y, mod, gf, T, tm=512):
    N = x.shape[0]
    row = pl.BlockSpec((tm, D_MODEL), lambda i: (i, 0))
    return pl.pallas_call(
        _final_kernel, grid=(N // tm,),
        in_specs=[row, row, _mod_spec(tm, T, 5), pl.BlockSpec((1, D_MODEL), lambda i: (0, 0))],
        out_specs=row, out_shape=jax.ShapeDtypeStruct((N, D_MODEL), F32),
        compiler_params=_cparams(("parallel",)), name="final_norm",
    )(x, y, mod, gf)


def kernel(x, c, ada_w, ada_b, norm1_g, norm2_g, w_in, sgu_w, sgu_b, out_norm_a, out_norm_b, w_out,
           peer_wq, peer_k1, peer_k2, peer_u, peer_v, final_g):
    B, T, Dm = x.shape
    L = ada_w.shape[0]
    N = B * T
    mods = ada_mod(c, ada_w, ada_b).reshape(L, B * 6, 1, Dm)
    Bg = B // BATCH_GROUPS
    gmods = [[mods[l, g * Bg * 6:(g + 1) * Bg * 6] for g in range(BATCH_GROUPS)] for l in range(L)]
    xs = [x[g * Bg:(g + 1) * Bg].reshape(Bg * T, Dm) for g in range(BATCH_GROUPS)]
    ys = [None] * BATCH_GROUPS
    for l in range(L):
        w_in_bf, w_out_bf, wq_bf = w_in[l].astype(BF16), w_out[l].astype(BF16), peer_wq[l].astype(BF16)
        bias_full = jnp.repeat(sgu_b[l].T, GROUP, axis=1)
        for g in range(BATCH_GROUPS):
            mod = gmods[l][g]
            outs = proj_call(xs[g], ys[g], gmods[l - 1][g] if l else None, mod, norm1_g[l].reshape(1, Dm), w_in_bf, T)
            if l:
                xs[g], outs = outs[0], outs[1:]
            ua, va, q, k, v = outs
            ya = gmlp_call(ua, va, sgu_w[l], bias_full, out_norm_a[l].reshape(1, D_A))
            yb = attn_call(q, k, v, out_norm_b[l].reshape(1, D_B), Bg, T)
            xs[g], h2 = out_call(ya, yb, xs[g], w_out_bf, mod, norm2_g[l].reshape(1, Dm), T)
            idx, gate = peerq_call(h2, wq_bf, peer_k1[l], peer_k2[l])
            n_sc = h2.shape[0] - TC_EVAL_TOKENS
            y_sc = peer_eval_call(h2, idx, gate, peer_u[l], peer_v[l], n_tok=n_sc)
            y_tc = peer_eval_tc_call(h2, idx, gate, peer_u[l], peer_v[l], first_tok=n_sc)
            ys[g] = jnp.concatenate([y_sc, y_tc], axis=0)
    outs = [final_call(xs[g], ys[g], gmods[L - 1][g], final_g.reshape(1, Dm), T) for g in range(BATCH_GROUPS)]
    return jnp.concatenate(outs, axis=0).reshape(B, T, Dm)
```

```python
import functools
import math

import numpy as np
import jax
import jax.numpy as jnp
from jax import lax
from jax.experimental import pallas as pl
from jax.experimental.pallas import tpu as pltpu
from jax.experimental.pallas import tpu_sc as plsc

F32 = jnp.float32
BF16 = jnp.bfloat16
HIGHEST = lax.Precision.HIGHEST

D_MODEL = 1024
D_A = 512
D_B = 512
GROUP = 64
CHUNK = 128
N_KEYS = 128
PEER_HEADS = 8
PEER_TOPK = 16
HK = PEER_HEADS * PEER_TOPK
EPS = 1e-6
LANES = 128
VMEM_LIMIT = 48 * 1024 * 1024
BATCH_GROUPS = 4

_STAIR = [(a, b) for a in range(PEER_TOPK) for b in range(PEER_TOPK) if (a + 1) * (b + 1) <= PEER_TOPK]
N_STAIR = len(_STAIR)


def _cparams(sem):
    return pltpu.CompilerParams(dimension_semantics=sem, vmem_limit_bytes=VMEM_LIMIT)


def _dot(a, b):
    return jnp.dot(a, b, preferred_element_type=F32)


def _block_diag_mean(n, group):
    i = np.arange(n)
    return jnp.asarray((i[:, None] // group == i[None, :] // group).astype(np.float32) / group, BF16)


def _ada_kernel(c_ref, w_ref, b_ref, o_ref):
    c = c_ref[...]
    ca = c * jax.nn.sigmoid(c)
    o_ref[0] = jnp.dot(ca, w_ref[0], precision=HIGHEST, preferred_element_type=F32) + b_ref[0]


def ada_mod(c, ada_w, ada_b):
    L, Dm, E = ada_w.shape
    Bc = c.shape[0]
    tn = 1536
    return pl.pallas_call(
        _ada_kernel,
        grid=(L, E // tn),
        in_specs=[pl.BlockSpec((Bc, Dm), lambda l, j: (0, 0)),
                  pl.BlockSpec((1, Dm, tn), lambda l, j: (l, 0, j)),
                  pl.BlockSpec((1, 1, tn), lambda l, j: (l, 0, j))],
        out_specs=pl.BlockSpec((1, Bc, tn), lambda l, j: (l, 0, j)),
        out_shape=jax.ShapeDtypeStruct((L, Bc, E), F32),
        compiler_params=_cparams(("parallel", "parallel")),
        name="ada_mod",
    )(c, ada_w, ada_b.reshape(L, 1, E))


def _mod_spec(tm, T, j):
    return pl.BlockSpec((1, 1, D_MODEL), lambda i: (((i * tm) // T) * 6 + j, 0, 0))


def _rms_mod(x, g, sc, sh):
    ms = jnp.mean(x * x, axis=-1, keepdims=True)
    return (x * lax.rsqrt(ms + EPS) * g) * (1.0 + sc) + sh


def _proj_kernel(has_y, *refs):
    if has_y:
        x_ref, y_ref, gt_ref, g_ref, sc_ref, sh_ref, w_ref, xo_ref, ua_ref, va_ref, q_ref, k_ref, v_ref = refs
        x = x_ref[...] + gt_ref[0] * y_ref[...]
        xo_ref[...] = x
    else:
        x_ref, g_ref, sc_ref, sh_ref, w_ref, ua_ref, va_ref, q_ref, k_ref, v_ref = refs
        x = x_ref[...]
    h = _rms_mod(x, g_ref[...], sc_ref[0], sh_ref[0]).astype(BF16)
    outs = (ua_ref, va_ref, q_ref, k_ref, v_ref)
    for j, o_ref in enumerate(outs):
        p = _dot(h, w_ref[:, j * D_A:(j + 1) * D_A])
        if j < 2:
            p = jax.nn.gelu(p)
        o_ref[...] = p.astype(o_ref.dtype)


def proj_call(x, y, mod_prev, mod, g1, w_in_bf, T, tm=512, n_tok=None, x_first_tok=0):
    N = x.shape[0] if n_tok is None else n_tok
    has_y = y is not None
    row = pl.BlockSpec((tm, D_MODEL), lambda i: (i, 0))
    half = pl.BlockSpec((tm, D_A), lambda i: (i, 0))
    in_specs = [pl.BlockSpec((tm, D_MODEL), lambda i: (x_first_tok // tm + i, 0))]
    args = [x]
    if has_y:
        in_specs += [row, _mod_spec(tm, T, 5)]
        args += [y, mod_prev]
    in_specs += [pl.BlockSpec((1, D_MODEL), lambda i: (0, 0)), _mod_spec(tm, T, 1), _mod_spec(tm, T, 0),
                 pl.BlockSpec(w_in_bf.shape, lambda i: (0, 0))]
    args += [g1, mod, mod, w_in_bf]
    out_specs = [half] * 5
    out_shape = [jax.ShapeDtypeStruct((N, D_A), F32)] * 2 + [jax.ShapeDtypeStruct((N, D_A), BF16)] * 3
    if has_y:
        out_specs = [row] + out_specs
        out_shape = [jax.ShapeDtypeStruct((N, D_MODEL), F32)] + out_shape
    return pl.pallas_call(
        functools.partial(_proj_kernel, has_y),
        grid=(N // tm,), in_specs=in_specs, out_specs=out_specs, out_shape=out_shape,
        compiler_params=_cparams(("parallel",)), name="proj",
    )(*args)


def _gmlp_kernel(ua_ref, va_ref, w_ref, bias_ref, a_ref, ga_ref, o_ref):
    A = a_ref[...]
    row = lax.broadcasted_iota(jnp.int32, (CHUNK, CHUNK), 0)
    col = lax.broadcasted_iota(jnp.int32, (CHUNK, CHUNK), 1)
    causal = row >= col
    first_group = col < GROUP
    ws = [jnp.where(causal, w_ref[g], 0.0).astype(BF16) for g in range(D_A // GROUP)]
    bias = bias_ref[...]
    ga = ga_ref[...]
    for c in range(ua_ref.shape[0] // CHUNK):
        rows = slice(c * CHUNK, (c + 1) * CHUNK)
        v = va_ref[rows, :]
        v_hi = v.astype(BF16)
        v_lo = (v - v_hi.astype(F32)).astype(BF16)
        d = v - (_dot(v_hi, A) + _dot(v_lo, A))
        var = _dot((d * d).astype(BF16), A)
        vn = (d * lax.rsqrt(var + EPS)).astype(BF16)
        parts = []
        for p in range(D_A // LANES):
            vp = vn[:, p * LANES:(p + 1) * LANES]
            parts.append(jnp.where(first_group, _dot(ws[2 * p], vp), _dot(ws[2 * p + 1], vp)))
        s = jnp.concatenate(parts, axis=1) + bias
        y = ua_ref[rows, :] * s
        ms = _dot((y * y).astype(BF16), A)
        o_ref[rows, :] = (y * lax.rsqrt(ms + EPS) * ga).astype(BF16)


def gmlp_call(ua, va, sgu_w, bias_full, ga, tm=512):
    N = ua.shape[0]
    half = pl.BlockSpec((tm, D_A), lambda i: (i, 0))
    return pl.pallas_call(
        _gmlp_kernel, grid=(N // tm,),
        in_specs=[half, half,
                  pl.BlockSpec(sgu_w.shape, lambda i: (0, 0, 0)),
                  pl.BlockSpec((CHUNK, D_A), lambda i: (0, 0)),
                  pl.BlockSpec((D_A, D_A), lambda i: (0, 0)),
                  pl.BlockSpec((1, D_A), lambda i: (0, 0))],
        out_specs=half, out_shape=jax.ShapeDtypeStruct((N, D_A), BF16),
        compiler_params=_cparams(("parallel",)), name="gmlp",
    )(ua, va, sgu_w, bias_full, _block_diag_mean(D_A, GROUP), ga)


EXP_UNDERFLOW = -104.0


def _attn_kernel(q_ref, k_ref, v_ref, mo_ref, a_ref, gb_ref, o_ref, *scratch):
    i = pl.program_id(1)
    n_pair = D_B // LANES
    lane = lax.broadcasted_iota(jnp.int32, (CHUNK, LANES), 1)
    row = lax.broadcasted_iota(jnp.int32, (CHUNK, LANES), 0)
    head_lanes = (lane < GROUP, lane >= GROUP)
    MO = mo_ref[...]
    scale = 1.0 / math.sqrt(GROUP)
    acc_refs, carry_refs, qs_refs = scratch[:n_pair], scratch[n_pair:2 * n_pair], scratch[2 * n_pair:]
    for p in range(n_pair):
        acc_refs[p][...] = jnp.zeros_like(acc_refs[p])
        carry_refs[p][...] = jnp.zeros_like(carry_refs[p])
        q = q_ref[:, p * LANES:(p + 1) * LANES]
        qs_refs[p][...] = jnp.concatenate([jnp.where(hl, q, jnp.zeros_like(q)) for hl in head_lanes], axis=0)

    def cond(state):
        jj, cmax = state
        return jnp.logical_and(jj <= i, cmax > EXP_UNDERFLOW)

    def body(state):
        jj, _ = state
        j = i - jj
        start = pl.multiple_of(j * CHUNK, CHUNK)
        mask1 = (j * CHUNK + lane) < (i * CHUNK + row)
        mask = jnp.concatenate([mask1, mask1], axis=0)
        cmax = jnp.full((2 * CHUNK, LANES), -jnp.inf, F32)
        slabs = range(n_pair)
        cols = [slice(p * LANES, (p + 1) * LANES) for p in slabs]
        zs = [lax.dot_general(qs_refs[p][...], k_ref[pl.ds(start, CHUNK), cols[p]], (((1,), (1,)), ((), ())),
                              preferred_element_type=F32) * scale for p in slabs]
        lszs = [jnp.minimum(z, 0.0) - jnp.log1p(jnp.exp(-jnp.abs(z))) for z in zs]
        trs = []
        for p in slabs:
            L = jnp.where(mask, lszs[p] - zs[p], 0.0)
            L_hi = L.astype(BF16)
            L_lo = (L - L_hi.astype(F32)).astype(BF16)
            tr = _dot(jnp.concatenate([L_hi, L_lo], axis=0), MO)
            trs.append(tr[:2 * CHUNK] + tr[2 * CHUNK:])
        for p in slabs:
            c = carry_refs[p][...]
            a = jnp.where(mask, jnp.exp(lszs[p] + trs[p][:, :LANES] + c), 0.0).astype(BF16)
            a_cat = jnp.concatenate([a[:CHUNK], a[CHUNK:]], axis=1)
            vs = v_ref[pl.ds(start, CHUNK), cols[p]]
            v_cat = jnp.concatenate([jnp.where(hl, vs, jnp.zeros_like(vs)) for hl in head_lanes], axis=0)
            acc_refs[p][...] += _dot(a_cat, v_cat)
            c = c + trs[p][:, LANES:]
            carry_refs[p][...] = c
            cmax = jnp.maximum(cmax, c)
        return jj + 1, jnp.max(cmax)

    lax.while_loop(cond, body, (jnp.int32(0), jnp.float32(0.0)))
    for p in range(n_pair):
        cols = slice(p * LANES, (p + 1) * LANES)
        acc = acc_refs[p][...]
        ms = _dot((acc * acc).astype(BF16), a_ref[...])
        o_ref[:, cols] = (acc * lax.rsqrt(ms + EPS) * gb_ref[:, cols]).astype(BF16)


def attn_call(q, k, v, gb, B, T):
    N = q.shape[0]
    nq = T // CHUNK
    i = np.arange(CHUNK)
    MO = jnp.asarray(np.concatenate([(i[:, None] > i[None, :]), np.ones((CHUNK, LANES), bool)], axis=1)
                     .astype(np.float32), BF16)
    kv = pl.BlockSpec((T, D_B), lambda b, i: (b, 0))
    qo = pl.BlockSpec((CHUNK, D_B), lambda b, i: (b * nq + i, 0))
    n_pair = D_B // LANES
    return pl.pallas_call(
        _attn_kernel, grid=(B, nq),
        in_specs=[qo, kv, kv, pl.BlockSpec((CHUNK, 2 * LANES), lambda b, i: (0, 0)),
                  pl.BlockSpec((CHUNK, LANES), lambda b, i: (0, 0)), pl.BlockSpec((1, D_B), lambda b, i: (0, 0))],
        out_specs=qo,
        out_shape=jax.ShapeDtypeStruct((N, D_B), BF16),
        scratch_shapes=([pltpu.VMEM((CHUNK, LANES), F32)] * n_pair + [pltpu.VMEM((2 * CHUNK, LANES), F32)] * n_pair
                        + [pltpu.VMEM((2 * CHUNK, LANES), BF16)] * n_pair),
        compiler_params=_cparams(("parallel", "arbitrary")), name="stickbreak_attn",
    )(q, k, v, MO, _block_diag_mean(LANES, GROUP), gb)


def _out_kernel(ya_ref, yb_ref, x_ref, w_ref, gt_ref, g2_ref, sc_ref, sh_ref, xo_ref, h2_ref):
    o = _dot(ya_ref[...], w_ref[:D_A, :]) + _dot(yb_ref[...], w_ref[D_A:, :])
    x = x_ref[...] + gt_ref[0] * o
    xo_ref[...] = x
    h2_ref[...] = _rms_mod(x, g2_ref[...], sc_ref[0], sh_ref[0])


def out_call(ya, yb, x, w_out_bf, mod, g2, T, tm=512, x_first_tok=0):
    N = ya.shape[0]
    row = pl.BlockSpec((tm, D_MODEL), lambda i: (i, 0))
    x_row = pl.BlockSpec((tm, D_MODEL), lambda i: (x_first_tok // tm + i, 0))
    half = pl.BlockSpec((tm, D_A), lambda i: (i, 0))
    return pl.pallas_call(
        _out_kernel, grid=(N // tm,),
        in_specs=[half, half, x_row, pl.BlockSpec(w_out_bf.shape, lambda i: (0, 0)), _mod_spec(tm, T, 2),
                  pl.BlockSpec((1, D_MODEL), lambda i: (0, 0)), _mod_spec(tm, T, 4), _mod_spec(tm, T, 3)],
        out_specs=[row, row], out_shape=[jax.ShapeDtypeStruct((N, D_MODEL), F32)] * 2,
        compiler_params=_cparams(("parallel",)), name="out_proj",
    )(ya, yb, x, w_out_bf, mod, g2, mod, mod)


N_CAND_ROWS = 64


def _topk_cols(s, payload=None):
    n_rows, tb = s.shape
    row_f = lax.broadcasted_iota(jnp.int32, (n_rows, tb), 0).astype(F32)
    rank = lax.broadcasted_iota(jnp.int32, (PEER_TOPK, tb), 0)
    neg = jnp.float32(-jnp.inf)

    def body(r, carry):
        s, vals, second = carry
        m = jnp.max(s, axis=0, keepdims=True)
        pos = jnp.min(jnp.where(s == m, row_f, float(n_rows)), axis=0, keepdims=True)
        hit = row_f == pos
        out = pos if payload is None else jnp.max(jnp.where(hit, payload, -1.0), axis=0, keepdims=True)
        dst = rank == r
        return jnp.where(hit, neg, s), jnp.where(dst, m, vals), jnp.where(dst, out, second)

    zeros = jnp.zeros((PEER_TOPK, tb), F32)
    _, vals, second = lax.fori_loop(0, PEER_TOPK, body, (s, zeros, zeros))
    return vals, second


def _peerq_kernel(h2_ref, wq_ref, k1_ref, k2_ref, sel_ref, sele_ref, idx_ref, g_ref, q_scr, it_scr, gt_scr):
    h = pl.program_id(1)
    tm = h2_ref.shape[0]

    @pl.when(h == 0)
    def _():
        q_scr[...] = _dot(h2_ref[...].astype(BF16), wq_ref[...]).astype(BF16)

    nt = (((1,), (1,)), ((), ()))
    vals, keys = [], []
    for tb in range(tm // LANES):
        qh = q_scr[tb * LANES:(tb + 1) * LANES, pl.ds(pl.multiple_of(h * LANES, LANES), LANES)]
        s12 = jnp.concatenate([lax.dot_general(k1_ref[...], qh, nt, preferred_element_type=F32),
                               lax.dot_general(k2_ref[...], qh, nt, preferred_element_type=F32)], axis=1)
        v12, i12 = _topk_cols(s12)
        vals.append(jnp.concatenate([v12[:, :LANES], v12[:, LANES:]], axis=0))
        keys.append(jnp.concatenate([i12[:, :LANES], i12[:, LANES:]], axis=0))
    vals = jnp.concatenate(vals, axis=1)
    keys = jnp.concatenate(keys, axis=1)
    cand = jnp.dot(sel_ref[...], vals, precision=HIGHEST, preferred_element_type=F32)
    cand_row = lax.broadcasted_iota(jnp.int32, cand.shape, 0)
    cand = jnp.where(cand_row < N_STAIR, cand, -jnp.inf)
    ecand = jnp.dot(sele_ref[...], keys, precision=HIGHEST, preferred_element_type=F32)
    sc, ex = _topk_cols(cand, ecand)
    p = jnp.exp(sc - sc[0:1, :])
    out_rows = pl.ds(pl.multiple_of(h * PEER_TOPK, PEER_TOPK), PEER_TOPK)
    it_scr[out_rows, :] = ex
    gt_scr[out_rows, :] = p / jnp.sum(p, axis=0, keepdims=True)

    @pl.when(h == PEER_HEADS - 1)
    def _():
        idx_ref[...] = it_scr[...].T.astype(jnp.int32)
        g_ref[...] = gt_scr[...].T


def peerq_call(h2, wq_bf, k1, k2, tm=256):
    N = h2.shape[0]
    half = N_KEYS // 2
    k1p = jnp.concatenate([k1, jnp.zeros((N_KEYS, half), F32)], axis=1).astype(BF16)
    k2p = jnp.concatenate([jnp.zeros((N_KEYS, half), F32), k2], axis=1).astype(BF16)
    sel = np.zeros((N_CAND_ROWS, 2 * PEER_TOPK), np.float32)
    sele = np.zeros((N_CAND_ROWS, 2 * PEER_TOPK), np.float32)
    for c, (a, b) in enumerate(_STAIR):
        sel[c, a] = sel[c, PEER_TOPK + b] = 1.0
        sele[c, a] = float(N_KEYS)
        sele[c, PEER_TOPK + b] = 1.0
    const = lambda shape: pl.BlockSpec(shape, lambda i, h: (0, 0))
    row = pl.BlockSpec((tm, HK), lambda i, h: (i, 0))
    return pl.pallas_call(
        _peerq_kernel, grid=(N // tm, PEER_HEADS),
        in_specs=[pl.BlockSpec((tm, D_MODEL), lambda i, h: (i, 0)), const(wq_bf.shape), const((N_KEYS, LANES)),
                  const((N_KEYS, LANES)), const(sel.shape), const(sele.shape)],
        out_specs=[row, row],
        out_shape=[jax.ShapeDtypeStruct((N, HK), jnp.int32), jax.ShapeDtypeStruct((N, HK), F32)],
        scratch_shapes=[pltpu.VMEM((tm, D_MODEL), BF16), pltpu.VMEM((HK, tm), F32), pltpu.VMEM((HK, tm), F32)],
        compiler_params=_cparams(("parallel", "arbitrary")), name="peer_retrieve",
    )(h2, wq_bf, k1p, k2p, jnp.asarray(sel), jnp.asarray(sele))


SC_LANES = 16
SC_CORES = 2
SC_WORKERS = 32
ROWS_PER_GATHER = 16
N_ROW_BUF = 4
LOOKAHEAD = N_ROW_BUF - 1
TOK_BLOCK = 16
ROW_BLOCK = 8
U_UNROLL = 2
COL_BLOCK = 16
N_DCHUNK = D_MODEL // SC_LANES
N_GATHER = HK // ROWS_PER_GATHER


def _sc_gelu(x):
    y = math.sqrt(2.0 / math.pi) * (x + 0.044715 * (x * x * x))
    t = 1.0 - 2.0 / (jnp.exp(2.0 * y) + 1.0)
    return 0.5 * x * (1.0 + t)


def _peer_eval_body(per_w, h_hbm, idx_hbm, g_hbm, u_hbm, v_hbm, o_hbm,
                    idx_blk, g_blk, h_blk, out_blk, w_v, *bufs):
    wid = lax.axis_index("s") * SC_CORES + lax.axis_index("c")
    rows = bufs[:N_ROW_BUF]
    sems = bufs[N_ROW_BUF:]
    lane = lax.iota(jnp.int32, SC_LANES)
    zeros = jnp.zeros((SC_LANES,), F32)

    def gather(table, i, j, buf):
        return pltpu.make_async_copy(table.at[idx_blk.at[i, j]], rows[buf], sems[buf])

    def u_compute(i, n, rbuf):
        for half in range(ROWS_PER_GATHER // SC_LANES):
            actv = zeros
            for rg in range(SC_LANES // ROW_BLOCK):
                r0 = half * SC_LANES + rg * ROW_BLOCK

                def c_body(cb, accs):
                    accs = list(accs)
                    for cc in range(U_UNROLL):
                        sl = pl.ds(pl.multiple_of((cb * U_UNROLL + cc) * SC_LANES, SC_LANES), SC_LANES)
                        hc = h_blk[i, sl]
                        for r in range(ROW_BLOCK):
                            accs[r] = accs[r] + rbuf[r0 + r, sl] * hc
                    return tuple(accs)

                accs = lax.fori_loop(0, N_DCHUNK // U_UNROLL, c_body, (zeros,) * ROW_BLOCK)
                for r in range(ROW_BLOCK):
                    actv = jnp.where(lane == rg * ROW_BLOCK + r, jnp.sum(accs[r]), actv)
            sl = pl.ds(n * ROWS_PER_GATHER + half * SC_LANES, SC_LANES)
            w_v[sl] = g_blk[i, sl] * _sc_gelu(actv)

    def v_compute(i, m, rbuf):
        for cb in range(N_DCHUNK // COL_BLOCK):
            def r_body(r, accs):
                wk = plsc.load_gather(w_v, [jnp.full((SC_LANES,), m * ROWS_PER_GATHER, jnp.int32) + r])
                return tuple(accs[cc] + wk * rbuf[r, pl.ds((cb * COL_BLOCK + cc) * SC_LANES, SC_LANES)]
                             for cc in range(COL_BLOCK))

            accs = lax.fori_loop(0, ROWS_PER_GATHER, r_body, (zeros,) * COL_BLOCK)
            for cc in range(COL_BLOCK):
                sl = pl.ds((cb * COL_BLOCK + cc) * SC_LANES, SC_LANES)
                if m == 0:
                    out_blk[i, sl] = accs[cc]
                else:
                    plsc.addupdate(out_blk.at[i, sl], accs[cc])

    @pl.loop(0, per_w // TOK_BLOCK)
    def _(bi):
        t0 = wid * per_w + bi * TOK_BLOCK
        pltpu.sync_copy(idx_hbm.at[pl.ds(t0, TOK_BLOCK)], idx_blk)
        pltpu.sync_copy(g_hbm.at[pl.ds(t0, TOK_BLOCK)], g_blk)
        pltpu.sync_copy(h_hbm.at[pl.ds(t0, TOK_BLOCK)], h_blk)
        def task_gather(i, n):
            table = u_hbm if n < N_GATHER else v_hbm
            return gather(table, i, n % N_GATHER, n % N_ROW_BUF)

        for n in range(LOOKAHEAD):
            task_gather(0, n).start()

        @pl.loop(0, TOK_BLOCK)
        def _(i):
            for n in range(2 * N_GATHER):
                ahead = n + LOOKAHEAD
                if ahead < 2 * N_GATHER:
                    task_gather(i, ahead).start()
                else:
                    @pl.when(i + 1 < TOK_BLOCK)
                    def _():
                        task_gather(i + 1, ahead - 2 * N_GATHER).start()
                task_gather(i, n).wait()
                if n < N_GATHER:
                    u_compute(i, n, rows[n % N_ROW_BUF])
                else:
                    v_compute(i, n - N_GATHER, rows[n % N_ROW_BUF])

        pltpu.sync_copy(out_blk, o_hbm.at[pl.ds(t0, TOK_BLOCK)])


def peer_eval_call(h2, idx, g, u_tab, v_tab, n_tok=None):
    N = h2.shape[0] if n_tok is None else n_tok
    per_w = N // SC_WORKERS
    assert N % (SC_WORKERS * TOK_BLOCK) == 0
    assert (2 * N_GATHER) % N_ROW_BUF == 0 and LOOKAHEAD < N_ROW_BUF
    mesh = plsc.VectorSubcoreMesh(core_axis_name="c", subcore_axis_name="s",
                                  num_cores=SC_CORES, num_subcores=SC_WORKERS // SC_CORES)
    return pl.kernel(
        functools.partial(_peer_eval_body, per_w),
        out_type=jax.ShapeDtypeStruct((N, D_MODEL), F32),
        mesh=mesh,
        scratch_types=[
            pltpu.VMEM((TOK_BLOCK, N_GATHER, ROWS_PER_GATHER), jnp.int32),
            pltpu.VMEM((TOK_BLOCK, HK), F32),
            pltpu.VMEM((TOK_BLOCK, D_MODEL), F32),
            pltpu.VMEM((TOK_BLOCK, D_MODEL), F32),
            pltpu.VMEM((HK,), F32),
        ] + [pltpu.VMEM((ROWS_PER_GATHER, D_MODEL), F32)] * N_ROW_BUF + [pltpu.SemaphoreType.DMA] * N_ROW_BUF,
        compiler_params=pltpu.CompilerParams(needs_layout_passes=False),
        name="peer_eval",
    )(h2, idx.reshape(h2.shape[0], N_GATHER, ROWS_PER_GATHER), g, u_tab, v_tab)


N_EXPERTS = N_KEYS * N_KEYS
E_HALF = N_EXPERTS // 2
TC_TOK_TILE = 16
TC_EVAL_TOKENS = (1536, 2048)
SUBLANES = 8
TC_EVAL_VMEM = 58 * 1024 * 1024


def _tc_eval_kernel(idx_s, hi_ref, g_ref, h_ref, u_hbm, v_hbm, o_ref, tab, actw, outp, rsum, wb_all, stage, sem):
    p = pl.program_id(0)
    i = pl.program_id(1)
    for ph, (src, half) in enumerate(((u_hbm, 0), (u_hbm, 1), (v_hbm, 0), (v_hbm, 1))):
        @pl.when(jnp.logical_and(p == ph, i == 0))
        def _():
            cp = pltpu.make_async_copy(src.at[pl.ds(half * E_HALF, E_HALF)], tab, sem)
            cp.start()
            cp.wait()

    lane = lax.broadcasted_iota(jnp.int32, (HK, LANES), 1)
    row = lax.broadcasted_iota(jnp.int32, (HK, LANES), 0)
    in_half = hi_ref[0] == (p % 2)

    @pl.when(p < 2)
    def _():
        acts = jnp.zeros((HK, LANES), F32)
        for t in range(TC_TOK_TILE):
            h = h_ref[t]
            for k in range(HK):
                rsum[t, pl.ds(k, 1), :] = jnp.sum(tab[idx_s[t, k]] * h, axis=0, keepdims=True)
            acts = jnp.where(lane == t, jnp.sum(rsum[t], axis=1, keepdims=True), acts)
        part = jnp.where(in_half, acts, 0.0)

        @pl.when(p == 0)
        def _():
            actw[i] = part

        @pl.when(p == 1)
        def _():
            actw[i] = g_ref[0] * jax.nn.gelu(actw[i] + part)

    @pl.when(p >= 2)
    def _():
        wm = jnp.where(in_half, actw[i], 0.0)

        for t in range(TC_TOK_TILE):
            wb_all[t] = jnp.broadcast_to(wm[:, t:t + 1], (HK, LANES))

        for t in range(TC_TOK_TILE):
            accs = [jnp.zeros((SUBLANES, LANES), F32)] * 4
            for k in range(HK):
                w = jnp.broadcast_to(wb_all[t, pl.ds(k, 1), :], (SUBLANES, LANES))
                accs[k % 4] = accs[k % 4] + w * tab[idx_s[t, k]]
            stage[t] = (accs[0] + accs[1]) + (accs[2] + accs[3])
        tile_toks = pl.ds(pl.multiple_of(i * TC_TOK_TILE, TC_TOK_TILE), TC_TOK_TILE)

        @pl.when(p == 2)
        def _():
            outp[tile_toks] = stage[...]

        @pl.when(p == 3)
        def _():
            o_ref[...] = outp[tile_toks] + stage[...]


def peer_eval_tc_call(h2, idx, g, u_tab, v_tab, first_tok=0):
    n_all = h2.shape[0]
    nt = n_all - first_tok
    n_tiles = nt // TC_TOK_TILE
    first_tile = first_tok // TC_TOK_TILE
    assert nt % TC_TOK_TILE == 0 and first_tok % TC_TOK_TILE == 0 and u_tab.shape[0] == N_EXPERTS
    idx, g = idx[first_tok:], g[first_tok:]

    def pairs_on_sublanes(a, fill):
        a = a.reshape(n_tiles, TC_TOK_TILE, HK).transpose(0, 2, 1)
        return jnp.pad(a, ((0, 0), (0, 0), (0, LANES - TC_TOK_TILE)), constant_values=fill)

    tile3 = lambda: pl.BlockSpec((1, HK, LANES), lambda p, i: (i, 0, 0))
    out = pl.pallas_call(
        _tc_eval_kernel, grid=(4, n_tiles),
        in_specs=[pl.BlockSpec((TC_TOK_TILE, HK), lambda p, i: (i, 0), memory_space=pltpu.SMEM),
                  tile3(), tile3(),
                  pl.BlockSpec((TC_TOK_TILE, SUBLANES, LANES), lambda p, i: (first_tile + i, 0, 0)),
                  pl.BlockSpec(memory_space=pl.ANY), pl.BlockSpec(memory_space=pl.ANY)],
        out_specs=pl.BlockSpec((TC_TOK_TILE, SUBLANES, LANES), lambda p, i: (jnp.where(p == 3, i, 0), 0, 0)),
        out_shape=jax.ShapeDtypeStruct((nt, SUBLANES, LANES), F32),
        scratch_shapes=[pltpu.VMEM((E_HALF, SUBLANES, LANES), F32),
                        pltpu.VMEM((n_tiles, HK, LANES), F32),
                        pltpu.VMEM((nt, SUBLANES, LANES), F32),
                        pltpu.VMEM((TC_TOK_TILE, HK, LANES), F32),
                        pltpu.VMEM((TC_TOK_TILE, HK, LANES), F32),
                        pltpu.VMEM((TC_TOK_TILE, SUBLANES, LANES), F32),
                        pltpu.SemaphoreType.DMA],
        compiler_params=pltpu.CompilerParams(dimension_semantics=("arbitrary", "arbitrary"),
                                             vmem_limit_bytes=TC_EVAL_VMEM),
        name="peer_eval_tc",
    )(idx & (E_HALF - 1), pairs_on_sublanes(idx // E_HALF, 2), pairs_on_sublanes(g, 0.0),
      h2.reshape(n_all, SUBLANES, LANES),
      u_tab.reshape(N_EXPERTS, SUBLANES, LANES), v_tab.reshape(N_EXPERTS, SUBLANES, LANES))
    return out.reshape(nt, D_MODEL)


def _final_kernel(x_ref, y_ref, gt_ref, g_ref, o_ref):
    x = x_ref[...] + gt_ref[0] * y_ref[...]
    ms = jnp.mean(x * x, axis=-1, keepdims=True)
    o_ref[...] = x * lax.rsqrt(ms + EPS) * g_ref[...]


def final_call(x, y, mod, gf, T, tm=512):
    N = x.shape[0]
    row = pl.BlockSpec((tm, D_MODEL), lambda i: (i, 0))
    return pl.pallas_call(
        _final_kernel, grid=(N // tm,),
        in_specs=[row, row, _mod_spec(tm, T, 5), pl.BlockSpec((1, D_MODEL), lambda i: (0, 0))],
        out_specs=row, out_shape=jax.ShapeDtypeStruct((N, D_MODEL), F32),
        compiler_params=_cparams(("parallel",)), name="final_norm",
    )(x, y, mod, gf)


def kernel(x, c, ada_w, ada_b, norm1_g, norm2_g, w_in, sgu_w, sgu_b, out_norm_a, out_norm_b, w_out,
           peer_wq, peer_k1, peer_k2, peer_u, peer_v, final_g):
    B, T, Dm = x.shape
    L = ada_w.shape[0]
    N = B * T
    mods = ada_mod(c, ada_w, ada_b).reshape(L, B * 6, 1, Dm)
    Bg = B // BATCH_GROUPS
    gmods = [[mods[l, g * Bg * 6:(g + 1) * Bg * 6] for g in range(BATCH_GROUPS)] for l in range(L)]
    Ng = Bg * T
    xs = [x.reshape(N, Dm)] * BATCH_GROUPS
    ys = [None] * BATCH_GROUPS
    for l in range(L):
        w_in_bf, w_out_bf, wq_bf = w_in[l].astype(BF16), w_out[l].astype(BF16), peer_wq[l].astype(BF16)
        bias_full = jnp.repeat(sgu_b[l].T, GROUP, axis=1)
        for g in range(BATCH_GROUPS):
            mod = gmods[l][g]
            x_first = g * Ng if l == 0 else 0
            outs = proj_call(xs[g], ys[g], gmods[l - 1][g] if l else None, mod, norm1_g[l].reshape(1, Dm), w_in_bf, T,
                             n_tok=Ng, x_first_tok=x_first)
            if l:
                xs[g], outs = outs[0], outs[1:]
            ua, va, q, k, v = outs
            ya = gmlp_call(ua, va, sgu_w[l], bias_full, out_norm_a[l].reshape(1, D_A))
            yb = attn_call(q, k, v, out_norm_b[l].reshape(1, D_B), Bg, T)
            xs[g], h2 = out_call(ya, yb, xs[g], w_out_bf, mod, norm2_g[l].reshape(1, Dm), T, x_first_tok=x_first)
            idx, gate = peerq_call(h2, wq_bf, peer_k1[l], peer_k2[l])
            n_sc = Ng - TC_EVAL_TOKENS[l]
            y_sc = peer_eval_call(h2, idx, gate, peer_u[l], peer_v[l], n_tok=n_sc)
            y_tc = peer_eval_tc_call(h2, idx, gate, peer_u[l], peer_v[l], first_tok=n_sc)
            ys[g] = jnp.concatenate([y_sc, y_tc], axis=0)
    outs = [final_call(xs[g], ys[g], gmods[L - 1][g], final_g.reshape(1, Dm), T) for g in range(BATCH_GROUPS)]
    return jnp.concatenate(outs, axis=0).reshape(B, T, Dm)
```

```python
import functools
import math

import numpy as np
import jax
import jax.numpy as jnp
from jax import lax
from jax.experimental import pallas as pl
from jax.experimental.pallas import tpu as pltpu
from jax.experimental.pallas import tpu_sc as plsc

F32 = jnp.float32
BF16 = jnp.bfloat16
HIGHEST = lax.Precision.HIGHEST

D_MODEL = 1024
D_A = 512
D_B = 512
GROUP = 64
CHUNK = 128
N_KEYS = 128
PEER_HEADS = 8
PEER_TOPK = 16
HK = PEER_HEADS * PEER_TOPK
EPS = 1e-6
LANES = 128
VMEM_LIMIT = 48 * 1024 * 1024
BATCH_GROUPS = 4

_STAIR = [(a, b) for a in range(PEER_TOPK) for b in range(PEER_TOPK) if (a + 1) * (b + 1) <= PEER_TOPK]
N_STAIR = len(_STAIR)


def _cparams(sem):
    return pltpu.CompilerParams(dimension_semantics=sem, vmem_limit_bytes=VMEM_LIMIT)


def _dot(a, b):
    return jnp.dot(a, b, preferred_element_type=F32)


def _block_diag_mean(n, group):
    i = np.arange(n)
    return jnp.asarray((i[:, None] // group == i[None, :] // group).astype(np.float32) / group, BF16)


def _ada_kernel(c_ref, w_ref, b_ref, o_ref):
    c = c_ref[...]
    ca = c * jax.nn.sigmoid(c)
    o_ref[0] = jnp.dot(ca, w_ref[0], precision=HIGHEST, preferred_element_type=F32) + b_ref[0]


def ada_mod(c, ada_w, ada_b):
    L, Dm, E = ada_w.shape
    Bc = c.shape[0]
    tn = 1536
    return pl.pallas_call(
        _ada_kernel,
        grid=(L, E // tn),
        in_specs=[pl.BlockSpec((Bc, Dm), lambda l, j: (0, 0)),
                  pl.BlockSpec((1, Dm, tn), lambda l, j: (l, 0, j)),
                  pl.BlockSpec((1, 1, tn), lambda l, j: (l, 0, j))],
        out_specs=pl.BlockSpec((1, Bc, tn), lambda l, j: (l, 0, j)),
        out_shape=jax.ShapeDtypeStruct((L, Bc, E), F32),
        compiler_params=_cparams(("parallel", "parallel")),
        name="ada_mod",
    )(c, ada_w, ada_b.reshape(L, 1, E))


def _mod_spec(tm, T, j):
    return pl.BlockSpec((1, 1, D_MODEL), lambda i: (((i * tm) // T) * 6 + j, 0, 0))


def _rms_mod(x, g, sc, sh):
    ms = jnp.mean(x * x, axis=-1, keepdims=True)
    return (x * lax.rsqrt(ms + EPS) * g) * (1.0 + sc) + sh


def _proj_kernel(has_y, *refs):
    if has_y:
        x_ref, y_ref, gt_ref, g_ref, sc_ref, sh_ref, w_ref, xo_ref, ua_ref, va_ref, q_ref, k_ref, v_ref = refs
        x = x_ref[...] + gt_ref[0] * y_ref[...]
        xo_ref[...] = x
    else:
        x_ref, g_ref, sc_ref, sh_ref, w_ref, ua_ref, va_ref, q_ref, k_ref, v_ref = refs
        x = x_ref[...]
    h = _rms_mod(x, g_ref[...], sc_ref[0], sh_ref[0]).astype(BF16)
    outs = (ua_ref, va_ref, q_ref, k_ref, v_ref)
    for j, o_ref in enumerate(outs):
        p = _dot(h, w_ref[:, j * D_A:(j + 1) * D_A])
        if j < 2:
            p = jax.nn.gelu(p)
        o_ref[...] = p.astype(o_ref.dtype)


def proj_call(x, y, mod_prev, mod, g1, w_in_bf, T, tm=512, n_tok=None, x_first_tok=0):
    N = x.shape[0] if n_tok is None else n_tok
    has_y = y is not None
    row = pl.BlockSpec((tm, D_MODEL), lambda i: (i, 0))
    half = pl.BlockSpec((tm, D_A), lambda i: (i, 0))
    in_specs = [pl.BlockSpec((tm, D_MODEL), lambda i: (x_first_tok // tm + i, 0))]
    args = [x]
    if has_y:
        in_specs += [row, _mod_spec(tm, T, 5)]
        args += [y, mod_prev]
    in_specs += [pl.BlockSpec((1, D_MODEL), lambda i: (0, 0)), _mod_spec(tm, T, 1), _mod_spec(tm, T, 0),
                 pl.BlockSpec(w_in_bf.shape, lambda i: (0, 0))]
    args += [g1, mod, mod, w_in_bf]
    out_specs = [half] * 5
    out_shape = [jax.ShapeDtypeStruct((N, D_A), F32)] * 2 + [jax.ShapeDtypeStruct((N, D_A), BF16)] * 3
    if has_y:
        out_specs = [row] + out_specs
        out_shape = [jax.ShapeDtypeStruct((N, D_MODEL), F32)] + out_shape
    return pl.pallas_call(
        functools.partial(_proj_kernel, has_y),
        grid=(N // tm,), in_specs=in_specs, out_specs=out_specs, out_shape=out_shape,
        compiler_params=_cparams(("parallel",)), name="proj",
    )(*args)


def _gmlp_kernel(ua_ref, va_ref, w_ref, bias_ref, a_ref, ga_ref, o_ref):
    A = a_ref[...]
    row = lax.broadcasted_iota(jnp.int32, (CHUNK, CHUNK), 0)
    col = lax.broadcasted_iota(jnp.int32, (CHUNK, CHUNK), 1)
    causal = row >= col
    first_group = col < GROUP
    ws = [jnp.where(causal, w_ref[g], 0.0).astype(BF16) for g in range(D_A // GROUP)]
    bias = bias_ref[...]
    ga = ga_ref[...]
    for c in range(ua_ref.shape[0] // CHUNK):
        rows = slice(c * CHUNK, (c + 1) * CHUNK)
        v = va_ref[rows, :]
        v_hi = v.astype(BF16)
        v_lo = (v - v_hi.astype(F32)).astype(BF16)
        d = v - (_dot(v_hi, A) + _dot(v_lo, A))
        var = _dot((d * d).astype(BF16), A)
        vn = (d * lax.rsqrt(var + EPS)).astype(BF16)
        parts = []
        for p in range(D_A // LANES):
            vp = vn[:, p * LANES:(p + 1) * LANES]
            parts.append(jnp.where(first_group, _dot(ws[2 * p], vp), _dot(ws[2 * p + 1], vp)))
        s = jnp.concatenate(parts, axis=1) + bias
        y = ua_ref[rows, :] * s
        ms = _dot((y * y).astype(BF16), A)
        o_ref[rows, :] = (y * lax.rsqrt(ms + EPS) * ga).astype(BF16)


def gmlp_call(ua, va, sgu_w, bias_full, ga, tm=512):
    N = ua.shape[0]
    half = pl.BlockSpec((tm, D_A), lambda i: (i, 0))
    return pl.pallas_call(
        _gmlp_kernel, grid=(N // tm,),
        in_specs=[half, half,
                  pl.BlockSpec(sgu_w.shape, lambda i: (0, 0, 0)),
                  pl.BlockSpec((CHUNK, D_A), lambda i: (0, 0)),
                  pl.BlockSpec((D_A, D_A), lambda i: (0, 0)),
                  pl.BlockSpec((1, D_A), lambda i: (0, 0))],
        out_specs=half, out_shape=jax.ShapeDtypeStruct((N, D_A), BF16),
        compiler_params=_cparams(("parallel",)), name="gmlp",
    )(ua, va, sgu_w, bias_full, _block_diag_mean(D_A, GROUP), ga)


EXP_UNDERFLOW = -104.0


def _attn_kernel(q_ref, k_ref, v_ref, mo_ref, a_ref, gb_ref, o_ref, *scratch):
    i = pl.program_id(1)
    n_pair = D_B // LANES
    lane = lax.broadcasted_iota(jnp.int32, (CHUNK, LANES), 1)
    row = lax.broadcasted_iota(jnp.int32, (CHUNK, LANES), 0)
    head_lanes = (lane < GROUP, lane >= GROUP)
    MO = mo_ref[...]
    scale = 1.0 / math.sqrt(GROUP)
    acc_refs, carry_refs, qs_refs = scratch[:n_pair], scratch[n_pair:2 * n_pair], scratch[2 * n_pair:]
    for p in range(n_pair):
        acc_refs[p][...] = jnp.zeros_like(acc_refs[p])
        carry_refs[p][...] = jnp.zeros_like(carry_refs[p])
        q = q_ref[:, p * LANES:(p + 1) * LANES]
        qs_refs[p][...] = jnp.concatenate([jnp.where(hl, q, jnp.zeros_like(q)) for hl in head_lanes], axis=0)

    def cond(state):
        jj, cmax = state
        return jnp.logical_and(jj <= i, cmax > EXP_UNDERFLOW)

    def body(state):
        jj, _ = state
        j = i - jj
        start = pl.multiple_of(j * CHUNK, CHUNK)
        mask1 = (j * CHUNK + lane) < (i * CHUNK + row)
        mask = jnp.concatenate([mask1, mask1], axis=0)
        cmax = jnp.full((2 * CHUNK, LANES), -jnp.inf, F32)
        slabs = range(n_pair)
        cols = [slice(p * LANES, (p + 1) * LANES) for p in slabs]
        zs = [lax.dot_general(qs_refs[p][...], k_ref[pl.ds(start, CHUNK), cols[p]], (((1,), (1,)), ((), ())),
                              preferred_element_type=F32) * scale for p in slabs]
        lszs = [jnp.minimum(z, 0.0) - jnp.log1p(jnp.exp(-jnp.abs(z))) for z in zs]
        trs = []
        for p in slabs:
            L = jnp.where(mask, lszs[p] - zs[p], 0.0)
            L_hi = L.astype(BF16)
            L_lo = (L - L_hi.astype(F32)).astype(BF16)
            tr = _dot(jnp.concatenate([L_hi, L_lo], axis=0), MO)
            trs.append(tr[:2 * CHUNK] + tr[2 * CHUNK:])
        for p in slabs:
            c = carry_refs[p][...]
            a = jnp.where(mask, jnp.exp(lszs[p] + trs[p][:, :LANES] + c), 0.0).astype(BF16)
            a_cat = jnp.concatenate([a[:CHUNK], a[CHUNK:]], axis=1)
            vs = v_ref[pl.ds(start, CHUNK), cols[p]]
            v_cat = jnp.concatenate([jnp.where(hl, vs, jnp.zeros_like(vs)) for hl in head_lanes], axis=0)
            acc_refs[p][...] += _dot(a_cat, v_cat)
            c = c + trs[p][:, LANES:]
            carry_refs[p][...] = c
            cmax = jnp.maximum(cmax, c)
        return jj + 1, jnp.max(cmax)

    lax.while_loop(cond, body, (jnp.int32(0), jnp.float32(0.0)))
    for p in range(n_pair):
        cols = slice(p * LANES, (p + 1) * LANES)
        acc = acc_refs[p][...]
        ms = _dot((acc * acc).astype(BF16), a_ref[...])
        o_ref[:, cols] = (acc * lax.rsqrt(ms + EPS) * gb_ref[:, cols]).astype(BF16)


def attn_call(q, k, v, gb, B, T):
    N = q.shape[0]
    nq = T // CHUNK
    i = np.arange(CHUNK)
    MO = jnp.asarray(np.concatenate([(i[:, None] > i[None, :]), np.ones((CHUNK, LANES), bool)], axis=1)
                     .astype(np.float32), BF16)
    kv = pl.BlockSpec((T, D_B), lambda b, i: (b, 0))
    qo = pl.BlockSpec((CHUNK, D_B), lambda b, i: (b * nq + i, 0))
    n_pair = D_B // LANES
    return pl.pallas_call(
        _attn_kernel, grid=(B, nq),
        in_specs=[qo, kv, kv, pl.BlockSpec((CHUNK, 2 * LANES), lambda b, i: (0, 0)),
                  pl.BlockSpec((CHUNK, LANES), lambda b, i: (0, 0)), pl.BlockSpec((1, D_B), lambda b, i: (0, 0))],
        out_specs=qo,
        out_shape=jax.ShapeDtypeStruct((N, D_B), BF16),
        scratch_shapes=([pltpu.VMEM((CHUNK, LANES), F32)] * n_pair + [pltpu.VMEM((2 * CHUNK, LANES), F32)] * n_pair
                        + [pltpu.VMEM((2 * CHUNK, LANES), BF16)] * n_pair),
        compiler_params=_cparams(("parallel", "arbitrary")), name="stickbreak_attn",
    )(q, k, v, MO, _block_diag_mean(LANES, GROUP), gb)


def _out_kernel(ya_ref, yb_ref, x_ref, w_ref, gt_ref, g2_ref, sc_ref, sh_ref, xo_ref, h2_ref):
    o = _dot(ya_ref[...], w_ref[:D_A, :]) + _dot(yb_ref[...], w_ref[D_A:, :])
    x = x_ref[...] + gt_ref[0] * o
    xo_ref[...] = x
    h2_ref[...] = _rms_mod(x, g2_ref[...], sc_ref[0], sh_ref[0])


def out_call(ya, yb, x, w_out_bf, mod, g2, T, tm=512, x_first_tok=0):
    N = ya.shape[0]
    row = pl.BlockSpec((tm, D_MODEL), lambda i: (i, 0))
    x_row = pl.BlockSpec((tm, D_MODEL), lambda i: (x_first_tok // tm + i, 0))
    half = pl.BlockSpec((tm, D_A), lambda i: (i, 0))
    return pl.pallas_call(
        _out_kernel, grid=(N // tm,),
        in_specs=[half, half, x_row, pl.BlockSpec(w_out_bf.shape, lambda i: (0, 0)), _mod_spec(tm, T, 2),
                  pl.BlockSpec((1, D_MODEL), lambda i: (0, 0)), _mod_spec(tm, T, 4), _mod_spec(tm, T, 3)],
        out_specs=[row, row], out_shape=[jax.ShapeDtypeStruct((N, D_MODEL), F32)] * 2,
        compiler_params=_cparams(("parallel",)), name="out_proj",
    )(ya, yb, x, w_out_bf, mod, g2, mod, mod)


N_CAND_ROWS = 64


def _topk_cols(s, payload=None):
    n_rows, tb = s.shape
    row_f = lax.broadcasted_iota(jnp.int32, (n_rows, tb), 0).astype(F32)
    rank = lax.broadcasted_iota(jnp.int32, (PEER_TOPK, tb), 0)
    neg = jnp.float32(-jnp.inf)

    def body(r, carry):
        s, vals, second = carry
        m = jnp.max(s, axis=0, keepdims=True)
        pos = jnp.min(jnp.where(s == m, row_f, float(n_rows)), axis=0, keepdims=True)
        hit = row_f == pos
        out = pos if payload is None else jnp.max(jnp.where(hit, payload, -1.0), axis=0, keepdims=True)
        dst = rank == r
        return jnp.where(hit, neg, s), jnp.where(dst, m, vals), jnp.where(dst, out, second)

    zeros = jnp.zeros((PEER_TOPK, tb), F32)
    _, vals, second = lax.fori_loop(0, PEER_TOPK, body, (s, zeros, zeros))
    return vals, second


def _peerq_kernel(h2_ref, wq_ref, k1_ref, k2_ref, sel_ref, sele_ref, idx_ref, g_ref, q_scr, it_scr, gt_scr):
    h = pl.program_id(1)
    tm = h2_ref.shape[0]

    @pl.when(h == 0)
    def _():
        q_scr[...] = _dot(h2_ref[...].astype(BF16), wq_ref[...]).astype(BF16)

    nt = (((1,), (1,)), ((), ()))
    vals, keys = [], []
    for tb in range(tm // LANES):
        qh = q_scr[tb * LANES:(tb + 1) * LANES, pl.ds(pl.multiple_of(h * LANES, LANES), LANES)]
        s12 = jnp.concatenate([lax.dot_general(k1_ref[...], qh, nt, preferred_element_type=F32),
                               lax.dot_general(k2_ref[...], qh, nt, preferred_element_type=F32)], axis=1)
        v12, i12 = _topk_cols(s12)
        vals.append(jnp.concatenate([v12[:, :LANES], v12[:, LANES:]], axis=0))
        keys.append(jnp.concatenate([i12[:, :LANES], i12[:, LANES:]], axis=0))
    vals = jnp.concatenate(vals, axis=1)
    keys = jnp.concatenate(keys, axis=1)
    cand = jnp.dot(sel_ref[...], vals, precision=HIGHEST, preferred_element_type=F32)
    cand_row = lax.broadcasted_iota(jnp.int32, cand.shape, 0)
    cand = jnp.where(cand_row < N_STAIR, cand, -jnp.inf)
    ecand = jnp.dot(sele_ref[...], keys, precision=HIGHEST, preferred_element_type=F32)
    sc, ex = _topk_cols(cand, ecand)
    p = jnp.exp(sc - sc[0:1, :])
    out_rows = pl.ds(pl.multiple_of(h * PEER_TOPK, PEER_TOPK), PEER_TOPK)
    it_scr[out_rows, :] = ex
    gt_scr[out_rows, :] = p / jnp.sum(p, axis=0, keepdims=True)

    @pl.when(h == PEER_HEADS - 1)
    def _():
        idx_ref[...] = it_scr[...].T.astype(jnp.int32)
        g_ref[...] = gt_scr[...].T


def peerq_call(h2, wq_bf, k1, k2, tm=256):
    N = h2.shape[0]
    half = N_KEYS // 2
    k1p = jnp.concatenate([k1, jnp.zeros((N_KEYS, half), F32)], axis=1).astype(BF16)
    k2p = jnp.concatenate([jnp.zeros((N_KEYS, half), F32), k2], axis=1).astype(BF16)
    sel = np.zeros((N_CAND_ROWS, 2 * PEER_TOPK), np.float32)
    sele = np.zeros((N_CAND_ROWS, 2 * PEER_TOPK), np.float32)
    for c, (a, b) in enumerate(_STAIR):
        sel[c, a] = sel[c, PEER_TOPK + b] = 1.0
        sele[c, a] = float(N_KEYS)
        sele[c, PEER_TOPK + b] = 1.0
    const = lambda shape: pl.BlockSpec(shape, lambda i, h: (0, 0))
    row = pl.BlockSpec((tm, HK), lambda i, h: (i, 0))
    return pl.pallas_call(
        _peerq_kernel, grid=(N // tm, PEER_HEADS),
        in_specs=[pl.BlockSpec((tm, D_MODEL), lambda i, h: (i, 0)), const(wq_bf.shape), const((N_KEYS, LANES)),
                  const((N_KEYS, LANES)), const(sel.shape), const(sele.shape)],
        out_specs=[row, row],
        out_shape=[jax.ShapeDtypeStruct((N, HK), jnp.int32), jax.ShapeDtypeStruct((N, HK), F32)],
        scratch_shapes=[pltpu.VMEM((tm, D_MODEL), BF16), pltpu.VMEM((HK, tm), F32), pltpu.VMEM((HK, tm), F32)],
        compiler_params=_cparams(("parallel", "arbitrary")), name="peer_retrieve",
    )(h2, wq_bf, k1p, k2p, jnp.asarray(sel), jnp.asarray(sele))


SC_LANES = 16
SC_CORES = 2
SC_WORKERS = 32
ROWS_PER_GATHER = 16
N_ROW_BUF = 4
LOOKAHEAD = N_ROW_BUF - 1
TOK_BLOCK = 8
ROW_BLOCK = 8
U_UNROLL = 2
COL_BLOCK = 16
N_DCHUNK = D_MODEL // SC_LANES
N_GATHER = HK // ROWS_PER_GATHER


def _sc_gelu(x):
    y = math.sqrt(2.0 / math.pi) * (x + 0.044715 * (x * x * x))
    t = 1.0 - 2.0 / (jnp.exp(2.0 * y) + 1.0)
    return 0.5 * x * (1.0 + t)


def _peer_eval_body(per_w, h_hbm, idx_hbm, g_hbm, u_hbm, v_hbm, o_hbm,
                    idx_blk, g_blk, h_blk, out_blk, w_v, *bufs):
    wid = lax.axis_index("s") * SC_CORES + lax.axis_index("c")
    rows = bufs[:N_ROW_BUF]
    sems = bufs[N_ROW_BUF:]
    lane = lax.iota(jnp.int32, SC_LANES)
    zeros = jnp.zeros((SC_LANES,), F32)

    def gather(table, i, j, buf):
        return pltpu.make_async_copy(table.at[idx_blk.at[i, j]], rows[buf], sems[buf])

    def u_compute(i, n, rbuf):
        for half in range(ROWS_PER_GATHER // SC_LANES):
            actv = zeros
            for rg in range(SC_LANES // ROW_BLOCK):
                r0 = half * SC_LANES + rg * ROW_BLOCK

                def c_body(cb, accs):
                    accs = list(accs)
                    for cc in range(U_UNROLL):
                        sl = pl.ds(pl.multiple_of((cb * U_UNROLL + cc) * SC_LANES, SC_LANES), SC_LANES)
                        hc = h_blk[i, sl]
                        for r in range(ROW_BLOCK):
                            accs[r] = accs[r] + rbuf[r0 + r, sl] * hc
                    return tuple(accs)

                accs = lax.fori_loop(0, N_DCHUNK // U_UNROLL, c_body, (zeros,) * ROW_BLOCK)
                for r in range(ROW_BLOCK):
                    actv = jnp.where(lane == rg * ROW_BLOCK + r, jnp.sum(accs[r]), actv)
            sl = pl.ds(n * ROWS_PER_GATHER + half * SC_LANES, SC_LANES)
            w_v[sl] = g_blk[i, sl] * _sc_gelu(actv)

    def v_compute(i, m, rbuf):
        for cb in range(N_DCHUNK // COL_BLOCK):
            def r_body(r, accs):
                wk = plsc.load_gather(w_v, [jnp.full((SC_LANES,), m * ROWS_PER_GATHER, jnp.int32) + r])
                return tuple(accs[cc] + wk * rbuf[r, pl.ds((cb * COL_BLOCK + cc) * SC_LANES, SC_LANES)]
                             for cc in range(COL_BLOCK))

            accs = lax.fori_loop(0, ROWS_PER_GATHER, r_body, (zeros,) * COL_BLOCK)
            for cc in range(COL_BLOCK):
                sl = pl.ds((cb * COL_BLOCK + cc) * SC_LANES, SC_LANES)
                if m == 0:
                    out_blk[i, sl] = accs[cc]
                else:
                    plsc.addupdate(out_blk.at[i, sl], accs[cc])

    @pl.loop(0, per_w // TOK_BLOCK)
    def _(bi):
        t0 = wid * per_w + bi * TOK_BLOCK
        pltpu.sync_copy(idx_hbm.at[pl.ds(t0, TOK_BLOCK)], idx_blk)
        pltpu.sync_copy(g_hbm.at[pl.ds(t0, TOK_BLOCK)], g_blk)
        pltpu.sync_copy(h_hbm.at[pl.ds(t0, TOK_BLOCK)], h_blk)
        def task_gather(i, n):
            table = u_hbm if n < N_GATHER else v_hbm
            return gather(table, i, n % N_GATHER, n % N_ROW_BUF)

        for n in range(LOOKAHEAD):
            task_gather(0, n).start()

        @pl.loop(0, TOK_BLOCK)
        def _(i):
            for n in range(2 * N_GATHER):
                ahead = n + LOOKAHEAD
                if ahead < 2 * N_GATHER:
                    task_gather(i, ahead).start()
                else:
                    @pl.when(i + 1 < TOK_BLOCK)
                    def _():
                        task_gather(i + 1, ahead - 2 * N_GATHER).start()
                task_gather(i, n).wait()
                if n < N_GATHER:
                    u_compute(i, n, rows[n % N_ROW_BUF])
                else:
                    v_compute(i, n - N_GATHER, rows[n % N_ROW_BUF])

        pltpu.sync_copy(out_blk, o_hbm.at[pl.ds(t0, TOK_BLOCK)])


def peer_eval_call(h2, idx, g, u_tab, v_tab, n_tok=None):
    N = h2.shape[0] if n_tok is None else n_tok
    per_w = N // SC_WORKERS
    assert N % (SC_WORKERS * TOK_BLOCK) == 0
    assert (2 * N_GATHER) % N_ROW_BUF == 0 and LOOKAHEAD < N_ROW_BUF
    mesh = plsc.VectorSubcoreMesh(core_axis_name="c", subcore_axis_name="s",
                                  num_cores=SC_CORES, num_subcores=SC_WORKERS // SC_CORES)
    return pl.kernel(
        functools.partial(_peer_eval_body, per_w),
        out_type=jax.ShapeDtypeStruct((N, D_MODEL), F32),
        mesh=mesh,
        scratch_types=[
            pltpu.VMEM((TOK_BLOCK, N_GATHER, ROWS_PER_GATHER), jnp.int32),
            pltpu.VMEM((TOK_BLOCK, HK), F32),
            pltpu.VMEM((TOK_BLOCK, D_MODEL), F32),
            pltpu.VMEM((TOK_BLOCK, D_MODEL), F32),
            pltpu.VMEM((HK,), F32),
        ] + [pltpu.VMEM((ROWS_PER_GATHER, D_MODEL), F32)] * N_ROW_BUF + [pltpu.SemaphoreType.DMA] * N_ROW_BUF,
        compiler_params=pltpu.CompilerParams(needs_layout_passes=False),
        name="peer_eval",
    )(h2, idx.reshape(h2.shape[0], N_GATHER, ROWS_PER_GATHER), g, u_tab, v_tab)


N_EXPERTS = N_KEYS * N_KEYS
E_HALF = N_EXPERTS // 2
TC_TOK_TILE = 16
TC_EVAL_TOKENS = (1536, 2304)
SUBLANES = 8
TC_EVAL_VMEM = 58 * 1024 * 1024


def _tc_eval_kernel(idx_s, hi_ref, g_ref, h_ref, u_hbm, v_hbm, o_ref, tab, actw, outp, rsum, wb_all, stage, sem):
    p = pl.program_id(0)
    i = pl.program_id(1)
    for ph, (src, half) in enumerate(((u_hbm, 0), (u_hbm, 1), (v_hbm, 0), (v_hbm, 1))):
        @pl.when(jnp.logical_and(p == ph, i == 0))
        def _():
            cp = pltpu.make_async_copy(src.at[pl.ds(half * E_HALF, E_HALF)], tab, sem)
            cp.start()
            cp.wait()

    lane = lax.broadcasted_iota(jnp.int32, (HK, LANES), 1)
    row = lax.broadcasted_iota(jnp.int32, (HK, LANES), 0)
    in_half = hi_ref[0] == (p % 2)

    @pl.when(p < 2)
    def _():
        acts = jnp.zeros((HK, LANES), F32)
        for t in range(TC_TOK_TILE):
            h = h_ref[t]
            for k in range(HK):
                rsum[t, pl.ds(k, 1), :] = jnp.sum(tab[idx_s[t, k]] * h, axis=0, keepdims=True)
            acts = jnp.where(lane == t, jnp.sum(rsum[t], axis=1, keepdims=True), acts)
        part = jnp.where(in_half, acts, 0.0)

        @pl.when(p == 0)
        def _():
            actw[i] = part

        @pl.when(p == 1)
        def _():
            actw[i] = g_ref[0] * jax.nn.gelu(actw[i] + part)

    @pl.when(p >= 2)
    def _():
        wm = jnp.where(in_half, actw[i], 0.0)

        for t in range(TC_TOK_TILE):
            wb_all[t] = jnp.broadcast_to(wm[:, t:t + 1], (HK, LANES))

        for t in range(TC_TOK_TILE):
            accs = [jnp.zeros((SUBLANES, LANES), F32)] * 4
            for k in range(HK):
                w = jnp.broadcast_to(wb_all[t, pl.ds(k, 1), :], (SUBLANES, LANES))
                accs[k % 4] = accs[k % 4] + w * tab[idx_s[t, k]]
            stage[t] = (accs[0] + accs[1]) + (accs[2] + accs[3])
        tile_toks = pl.ds(pl.multiple_of(i * TC_TOK_TILE, TC_TOK_TILE), TC_TOK_TILE)

        @pl.when(p == 2)
        def _():
            outp[tile_toks] = stage[...]

        @pl.when(p == 3)
        def _():
            o_ref[...] = outp[tile_toks] + stage[...]


def peer_eval_tc_call(h2, idx, g, u_tab, v_tab, first_tok=0):
    n_all = h2.shape[0]
    nt = n_all - first_tok
    n_tiles = nt // TC_TOK_TILE
    first_tile = first_tok // TC_TOK_TILE
    assert nt % TC_TOK_TILE == 0 and first_tok % TC_TOK_TILE == 0 and u_tab.shape[0] == N_EXPERTS
    idx, g = idx[first_tok:], g[first_tok:]

    def pairs_on_sublanes(a, fill):
        a = a.reshape(n_tiles, TC_TOK_TILE, HK).transpose(0, 2, 1)
        return jnp.pad(a, ((0, 0), (0, 0), (0, LANES - TC_TOK_TILE)), constant_values=fill)

    tile3 = lambda: pl.BlockSpec((1, HK, LANES), lambda p, i: (i, 0, 0))
    out = pl.pallas_call(
        _tc_eval_kernel, grid=(4, n_tiles),
        in_specs=[pl.BlockSpec((TC_TOK_TILE, HK), lambda p, i: (i, 0), memory_space=pltpu.SMEM),
                  tile3(), tile3(),
                  pl.BlockSpec((TC_TOK_TILE, SUBLANES, LANES), lambda p, i: (first_tile + i, 0, 0)),
                  pl.BlockSpec(memory_space=pl.ANY), pl.BlockSpec(memory_space=pl.ANY)],
        out_specs=pl.BlockSpec((TC_TOK_TILE, SUBLANES, LANES), lambda p, i: (jnp.where(p == 3, i, 0), 0, 0)),
        out_shape=jax.ShapeDtypeStruct((nt, SUBLANES, LANES), F32),
        scratch_shapes=[pltpu.VMEM((E_HALF, SUBLANES, LANES), F32),
                        pltpu.VMEM((n_tiles, HK, LANES), F32),
                        pltpu.VMEM((nt, SUBLANES, LANES), F32),
                        pltpu.VMEM((TC_TOK_TILE, HK, LANES), F32),
                        pltpu.VMEM((TC_TOK_TILE, HK, LANES), F32),
                        pltpu.VMEM((TC_TOK_TILE, SUBLANES, LANES), F32),
                        pltpu.SemaphoreType.DMA],
        compiler_params=pltpu.CompilerParams(dimension_semantics=("arbitrary", "arbitrary"),
                                             vmem_limit_bytes=TC_EVAL_VMEM),
        name="peer_eval_tc",
    )(idx & (E_HALF - 1), pairs_on_sublanes(idx // E_HALF, 2), pairs_on_sublanes(g, 0.0),
      h2.reshape(n_all, SUBLANES, LANES),
      u_tab.reshape(N_EXPERTS, SUBLANES, LANES), v_tab.reshape(N_EXPERTS, SUBLANES, LANES))
    return out.reshape(nt, D_MODEL)


def _final_kernel(x_ref, y_ref, gt_ref, g_ref, o_ref):
    x = x_ref[...] + gt_ref[0] * y_ref[...]
    ms = jnp.mean(x * x, axis=-1, keepdims=True)
    o_ref[...] = x * lax.rsqrt(ms + EPS) * g_ref[...]


def final_call(x, y, mod, gf, T, tm=512):
    N = x.shape[0]
    row = pl.BlockSpec((tm, D_MODEL), lambda i: (i, 0))
    return pl.pallas_call(
        _final_kernel, grid=(N // tm,),
        in_specs=[row, row, _mod_spec(tm, T, 5), pl.BlockSpec((1, D_MODEL), lambda i: (0, 0))],
        out_specs=row, out_shape=jax.ShapeDtypeStruct((N, D_MODEL), F32),
        compiler_params=_cparams(("parallel",)), name="final_norm",
    )(x, y, mod, gf)


def kernel(x, c, ada_w, ada_b, norm1_g, norm2_g, w_in, sgu_w, sgu_b, out_norm_a, out_norm_b, w_out,
           peer_wq, peer_k1, peer_k2, peer_u, peer_v, final_g):
    B, T, Dm = x.shape
    L = ada_w.shape[0]
    N = B * T
    mods = ada_mod(c, ada_w, ada_b).reshape(L, B * 6, 1, Dm)
    Bg = B // BATCH_GROUPS
    gmods = [[mods[l, g * Bg * 6:(g + 1) * Bg * 6] for g in range(BATCH_GROUPS)] for l in range(L)]
    Ng = Bg * T
    xs = [x.reshape(N, Dm)] * BATCH_GROUPS
    ys = [None] * BATCH_GROUPS
    for l in range(L):
        w_in_bf, w_out_bf, wq_bf = w_in[l].astype(BF16), w_out[l].astype(BF16), peer_wq[l].astype(BF16)
        bias_full = jnp.repeat(sgu_b[l].T, GROUP, axis=1)
        for g in range(BATCH_GROUPS):
            mod = gmods[l][g]
            x_first = g * Ng if l == 0 else 0
            outs = proj_call(xs[g], ys[g], gmods[l - 1][g] if l else None, mod, norm1_g[l].reshape(1, Dm), w_in_bf, T,
                             n_tok=Ng, x_first_tok=x_first)
            if l:
                xs[g], outs = outs[0], outs[1:]
            ua, va, q, k, v = outs
            ya = gmlp_call(ua, va, sgu_w[l], bias_full, out_norm_a[l].reshape(1, D_A))
            yb = attn_call(q, k, v, out_norm_b[l].reshape(1, D_B), Bg, T)
            xs[g], h2 = out_call(ya, yb, xs[g], w_out_bf, mod, norm2_g[l].reshape(1, Dm), T, x_first_tok=x_first)
            idx, gate = peerq_call(h2, wq_bf, peer_k1[l], peer_k2[l])
            n_sc = Ng - TC_EVAL_TOKENS[l]
            y_sc = peer_eval_call(h2, idx, gate, peer_u[l], peer_v[l], n_tok=n_sc)
            y_tc = peer_eval_tc_call(h2, idx, gate, peer_u[l], peer_v[l], first_tok=n_sc)
            ys[g] = jnp.concatenate([y_sc, y_tc], axis=0)
    outs = [final_call(xs[g], ys[g], gmods[L - 1][g], final_g.reshape(1, Dm), T) for g in range(BATCH_GROUPS)]
    return jnp.concatenate(outs, axis=0).reshape(B, T, Dm)
```

```python
import functools
import math

import numpy as np
import jax
import jax.numpy as jnp
from jax import lax
from jax.experimental import pallas as pl
from jax.experimental.pallas import tpu as pltpu
from jax.experimental.pallas import tpu_sc as plsc

F32 = jnp.float32
BF16 = jnp.bfloat16
HIGHEST = lax.Precision.HIGHEST

D_MODEL = 1024
D_A = 512
D_B = 512
GROUP = 64
CHUNK = 128
N_KEYS = 128
PEER_HEADS = 8
PEER_TOPK = 16
HK = PEER_HEADS * PEER_TOPK
EPS = 1e-6
LANES = 128
SUBLANES = 8
V7X_VMEM_BYTES = 64 * 1024 * 1024
VMEM_LIMIT = V7X_VMEM_BYTES * 3 // 4
BATCH_GROUPS = 4

_STAIR = [(a, b) for a in range(PEER_TOPK) for b in range(PEER_TOPK) if (a + 1) * (b + 1) <= PEER_TOPK]
N_STAIR = len(_STAIR)


def _cparams(sem):
    return pltpu.CompilerParams(dimension_semantics=sem, vmem_limit_bytes=VMEM_LIMIT)


def _dot(a, b):
    return jnp.dot(a, b, preferred_element_type=F32)


def _block_diag_mean(n, group):
    i = np.arange(n)
    return jnp.asarray((i[:, None] // group == i[None, :] // group).astype(np.float32) / group, BF16)


def _ada_kernel(c_ref, w_ref, b_ref, o_ref):
    c = c_ref[...]
    ca = c * jax.nn.sigmoid(c)
    o_ref[0] = jnp.dot(ca, w_ref[0], precision=HIGHEST, preferred_element_type=F32) + b_ref[0]


def ada_mod(c, ada_w, ada_b):
    L, Dm, E = ada_w.shape
    Bc = c.shape[0]
    tn = 1536
    return pl.pallas_call(
        _ada_kernel,
        grid=(L, E // tn),
        in_specs=[pl.BlockSpec((Bc, Dm), lambda l, j: (0, 0)),
                  pl.BlockSpec((1, Dm, tn), lambda l, j: (l, 0, j)),
                  pl.BlockSpec((1, 1, tn), lambda l, j: (l, 0, j))],
        out_specs=pl.BlockSpec((1, Bc, tn), lambda l, j: (l, 0, j)),
        out_shape=jax.ShapeDtypeStruct((L, Bc, E), F32),
        compiler_params=_cparams(("parallel", "parallel")),
        name="ada_mod",
    )(c, ada_w, ada_b.reshape(L, 1, E))


def _mod_spec(tm, T, j):
    return pl.BlockSpec((1, 1, D_MODEL), lambda i: (((i * tm) // T) * 6 + j, 0, 0))


def _rms_mod(x, g, sc, sh):
    ms = jnp.mean(x * x, axis=-1, keepdims=True)
    return (x * lax.rsqrt(ms + EPS) * g) * (1.0 + sc) + sh


def _proj_kernel(has_y, *refs):
    if has_y:
        x_ref, y_ref, gt_ref, g_ref, sc_ref, sh_ref, w_ref, xo_ref, ua_ref, va_ref, q_ref, k_ref, v_ref = refs
        x = x_ref[...] + gt_ref[0] * y_ref[...]
        xo_ref[...] = x
    else:
        x_ref, g_ref, sc_ref, sh_ref, w_ref, ua_ref, va_ref, q_ref, k_ref, v_ref = refs
        x = x_ref[...]
    h = _rms_mod(x, g_ref[...], sc_ref[0], sh_ref[0]).astype(BF16)
    outs = (ua_ref, va_ref, q_ref, k_ref, v_ref)
    for j, o_ref in enumerate(outs):
        p = _dot(h, w_ref[:, j * D_A:(j + 1) * D_A])
        if j < 2:
            p = jax.nn.gelu(p)
        o_ref[...] = p.astype(o_ref.dtype)


def proj_call(x, y, mod_prev, mod, g1, w_in_bf, T, tm=512, n_tok=None, x_first_tok=0):
    N = x.shape[0] if n_tok is None else n_tok
    has_y = y is not None
    row = pl.BlockSpec((tm, D_MODEL), lambda i: (i, 0))
    half = pl.BlockSpec((tm, D_A), lambda i: (i, 0))
    in_specs = [pl.BlockSpec((tm, D_MODEL), lambda i: (x_first_tok // tm + i, 0))]
    args = [x]
    if has_y:
        in_specs += [row, _mod_spec(tm, T, 5)]
        args += [y, mod_prev]
    in_specs += [pl.BlockSpec((1, D_MODEL), lambda i: (0, 0)), _mod_spec(tm, T, 1), _mod_spec(tm, T, 0),
                 pl.BlockSpec(w_in_bf.shape, lambda i: (0, 0))]
    args += [g1, mod, mod, w_in_bf]
    out_specs = [half] * 5
    out_shape = [jax.ShapeDtypeStruct((N, D_A), F32)] * 2 + [jax.ShapeDtypeStruct((N, D_A), BF16)] * 3
    if has_y:
        out_specs = [row] + out_specs
        out_shape = [jax.ShapeDtypeStruct((N, D_MODEL), F32)] + out_shape
    return pl.pallas_call(
        functools.partial(_proj_kernel, has_y),
        grid=(N // tm,), in_specs=in_specs, out_specs=out_specs, out_shape=out_shape,
        compiler_params=_cparams(("parallel",)), name="proj",
    )(*args)


def _gmlp_kernel(ua_ref, va_ref, w_ref, bias_ref, a_ref, ga_ref, o_ref):
    A = a_ref[...]
    row = lax.broadcasted_iota(jnp.int32, (CHUNK, CHUNK), 0)
    col = lax.broadcasted_iota(jnp.int32, (CHUNK, CHUNK), 1)
    causal = row >= col
    first_group = col < GROUP
    ws = [jnp.where(causal, w_ref[g], 0.0).astype(BF16) for g in range(D_A // GROUP)]
    bias = bias_ref[...]
    ga = ga_ref[...]
    for c in range(ua_ref.shape[0] // CHUNK):
        rows = slice(c * CHUNK, (c + 1) * CHUNK)
        v = va_ref[rows, :]
        v_hi = v.astype(BF16)
        v_lo = (v - v_hi.astype(F32)).astype(BF16)
        d = v - (_dot(v_hi, A) + _dot(v_lo, A))
        var = _dot((d * d).astype(BF16), A)
        vn = (d * lax.rsqrt(var + EPS)).astype(BF16)
        parts = []
        for p in range(D_A // LANES):
            vp = vn[:, p * LANES:(p + 1) * LANES]
            parts.append(jnp.where(first_group, _dot(ws[2 * p], vp), _dot(ws[2 * p + 1], vp)))
        s = jnp.concatenate(parts, axis=1) + bias
        y = ua_ref[rows, :] * s
        ms = _dot((y * y).astype(BF16), A)
        o_ref[rows, :] = (y * lax.rsqrt(ms + EPS) * ga).astype(BF16)


def gmlp_call(ua, va, sgu_w, bias_full, ga, tm=512):
    N = ua.shape[0]
    half = pl.BlockSpec((tm, D_A), lambda i: (i, 0))
    return pl.pallas_call(
        _gmlp_kernel, grid=(N // tm,),
        in_specs=[half, half,
                  pl.BlockSpec(sgu_w.shape, lambda i: (0, 0, 0)),
                  pl.BlockSpec((CHUNK, D_A), lambda i: (0, 0)),
                  pl.BlockSpec((D_A, D_A), lambda i: (0, 0)),
                  pl.BlockSpec((1, D_A), lambda i: (0, 0))],
        out_specs=half, out_shape=jax.ShapeDtypeStruct((N, D_A), BF16),
        compiler_params=_cparams(("parallel",)), name="gmlp",
    )(ua, va, sgu_w, bias_full, _block_diag_mean(D_A, GROUP), ga)


EXP_UNDERFLOW = -104.0


def _attn_kernel(q_ref, k_ref, v_ref, mo_ref, a_ref, gb_ref, o_ref, *scratch):
    i = pl.program_id(1)
    n_pair = D_B // LANES
    lane = lax.broadcasted_iota(jnp.int32, (CHUNK, LANES), 1)
    row = lax.broadcasted_iota(jnp.int32, (CHUNK, LANES), 0)
    head_lanes = (lane < GROUP, lane >= GROUP)
    MO = mo_ref[...]
    scale = 1.0 / math.sqrt(GROUP)
    acc_refs, carry_refs, qs_refs = scratch[:n_pair], scratch[n_pair:2 * n_pair], scratch[2 * n_pair:]
    for p in range(n_pair):
        acc_refs[p][...] = jnp.zeros_like(acc_refs[p])
        carry_refs[p][...] = jnp.zeros_like(carry_refs[p])
        q = q_ref[:, p * LANES:(p + 1) * LANES]
        qs_refs[p][...] = jnp.concatenate([jnp.where(hl, q, jnp.zeros_like(q)) for hl in head_lanes], axis=0)

    def cond(state):
        jj, cmax = state
        return jnp.logical_and(jj <= i, cmax > EXP_UNDERFLOW)

    def body(state):
        jj, _ = state
        j = i - jj
        start = pl.multiple_of(j * CHUNK, CHUNK)
        mask1 = (j * CHUNK + lane) < (i * CHUNK + row)
        mask = jnp.concatenate([mask1, mask1], axis=0)
        cmax = jnp.full((2 * CHUNK, LANES), -jnp.inf, F32)
        slabs = range(n_pair)
        cols = [slice(p * LANES, (p + 1) * LANES) for p in slabs]
        zs = [lax.dot_general(qs_refs[p][...], k_ref[pl.ds(start, CHUNK), cols[p]], (((1,), (1,)), ((), ())),
                              preferred_element_type=F32) * scale for p in slabs]
        lszs = [jnp.minimum(z, 0.0) - jnp.log1p(jnp.exp(-jnp.abs(z))) for z in zs]
        trs = []
        for p in slabs:
            L = jnp.where(mask, lszs[p] - zs[p], 0.0)
            L_hi = L.astype(BF16)
            L_lo = (L - L_hi.astype(F32)).astype(BF16)
            tr = _dot(jnp.concatenate([L_hi, L_lo], axis=0), MO)
            trs.append(tr[:2 * CHUNK] + tr[2 * CHUNK:])
        for p in slabs:
            c = carry_refs[p][...]
            a = jnp.where(mask, jnp.exp(lszs[p] + trs[p][:, :LANES] + c), 0.0).astype(BF16)
            a_cat = jnp.concatenate([a[:CHUNK], a[CHUNK:]], axis=1)
            vs = v_ref[pl.ds(start, CHUNK), cols[p]]
            v_cat = jnp.concatenate([jnp.where(hl, vs, jnp.zeros_like(vs)) for hl in head_lanes], axis=0)
            acc_refs[p][...] += _dot(a_cat, v_cat)
            c = c + trs[p][:, LANES:]
            carry_refs[p][...] = c
            cmax = jnp.maximum(cmax, c)
        return jj + 1, jnp.max(cmax)

    lax.while_loop(cond, body, (jnp.int32(0), jnp.float32(0.0)))
    for p in range(n_pair):
        cols = slice(p * LANES, (p + 1) * LANES)
        acc = acc_refs[p][...]
        ms = _dot((acc * acc).astype(BF16), a_ref[...])
        o_ref[:, cols] = (acc * lax.rsqrt(ms + EPS) * gb_ref[:, cols]).astype(BF16)


def attn_call(q, k, v, gb, B, T):
    N = q.shape[0]
    nq = T // CHUNK
    i = np.arange(CHUNK)
    MO = jnp.asarray(np.concatenate([(i[:, None] > i[None, :]), np.ones((CHUNK, LANES), bool)], axis=1)
                     .astype(np.float32), BF16)
    kv = pl.BlockSpec((T, D_B), lambda b, i: (b, 0))
    qo = pl.BlockSpec((CHUNK, D_B), lambda b, i: (b * nq + i, 0))
    n_pair = D_B // LANES
    return pl.pallas_call(
        _attn_kernel, grid=(B, nq),
        in_specs=[qo, kv, kv, pl.BlockSpec((CHUNK, 2 * LANES), lambda b, i: (0, 0)),
                  pl.BlockSpec((CHUNK, LANES), lambda b, i: (0, 0)), pl.BlockSpec((1, D_B), lambda b, i: (0, 0))],
        out_specs=qo,
        out_shape=jax.ShapeDtypeStruct((N, D_B), BF16),
        scratch_shapes=([pltpu.VMEM((CHUNK, LANES), F32)] * n_pair + [pltpu.VMEM((2 * CHUNK, LANES), F32)] * n_pair
                        + [pltpu.VMEM((2 * CHUNK, LANES), BF16)] * n_pair),
        compiler_params=_cparams(("parallel", "arbitrary")), name="stickbreak_attn",
    )(q, k, v, MO, _block_diag_mean(LANES, GROUP), gb)


def _out_kernel(ya_ref, yb_ref, x_ref, w_ref, gt_ref, g2_ref, sc_ref, sh_ref, xo_ref, h2_ref):
    o = _dot(ya_ref[...], w_ref[:D_A, :]) + _dot(yb_ref[...], w_ref[D_A:, :])
    x = x_ref[...] + gt_ref[0] * o
    xo_ref[...] = x
    h2_ref[...] = _rms_mod(x, g2_ref[...], sc_ref[0], sh_ref[0])


def out_call(ya, yb, x, w_out_bf, mod, g2, T, tm=512, x_first_tok=0):
    N = ya.shape[0]
    row = pl.BlockSpec((tm, D_MODEL), lambda i: (i, 0))
    x_row = pl.BlockSpec((tm, D_MODEL), lambda i: (x_first_tok // tm + i, 0))
    half = pl.BlockSpec((tm, D_A), lambda i: (i, 0))
    return pl.pallas_call(
        _out_kernel, grid=(N // tm,),
        in_specs=[half, half, x_row, pl.BlockSpec(w_out_bf.shape, lambda i: (0, 0)), _mod_spec(tm, T, 2),
                  pl.BlockSpec((1, D_MODEL), lambda i: (0, 0)), _mod_spec(tm, T, 4), _mod_spec(tm, T, 3)],
        out_specs=[row, row], out_shape=[jax.ShapeDtypeStruct((N, D_MODEL), F32)] * 2,
        compiler_params=_cparams(("parallel",)), name="out_proj",
    )(ya, yb, x, w_out_bf, mod, g2, mod, mod)


N_CAND_ROWS = -(-N_STAIR // SUBLANES) * SUBLANES


def _topk_cols(s, payload=None):
    n_rows, tb = s.shape
    row_f = lax.broadcasted_iota(jnp.int32, (n_rows, tb), 0).astype(F32)
    rank = lax.broadcasted_iota(jnp.int32, (PEER_TOPK, tb), 0)
    neg = jnp.float32(-jnp.inf)

    def body(r, carry):
        s, vals, second = carry
        m = jnp.max(s, axis=0, keepdims=True)
        pos = jnp.min(jnp.where(s == m, row_f, float(n_rows)), axis=0, keepdims=True)
        hit = row_f == pos
        out = pos if payload is None else jnp.max(jnp.where(hit, payload, -1.0), axis=0, keepdims=True)
        dst = rank == r
        return jnp.where(hit, neg, s), jnp.where(dst, m, vals), jnp.where(dst, out, second)

    zeros = jnp.zeros((PEER_TOPK, tb), F32)
    _, vals, second = lax.fori_loop(0, PEER_TOPK, body, (s, zeros, zeros))
    return vals, second


def _peerq_kernel(h2_ref, wq_ref, k1_ref, k2_ref, sel_ref, sele_ref, idx_ref, g_ref, q_scr, it_scr, gt_scr):
    h = pl.program_id(1)
    tm = h2_ref.shape[0]

    @pl.when(h == 0)
    def _():
        q_scr[...] = _dot(h2_ref[...].astype(BF16), wq_ref[...]).astype(BF16)

    nt = (((1,), (1,)), ((), ()))
    vals, keys = [], []
    for tb in range(tm // LANES):
        qh = q_scr[tb * LANES:(tb + 1) * LANES, pl.ds(pl.multiple_of(h * LANES, LANES), LANES)]
        s12 = jnp.concatenate([lax.dot_general(k1_ref[...], qh, nt, preferred_element_type=F32),
                               lax.dot_general(k2_ref[...], qh, nt, preferred_element_type=F32)], axis=1)
        v12, i12 = _topk_cols(s12)
        vals.append(jnp.concatenate([v12[:, :LANES], v12[:, LANES:]], axis=0))
        keys.append(jnp.concatenate([i12[:, :LANES], i12[:, LANES:]], axis=0))
    vals = jnp.concatenate(vals, axis=1)
    keys = jnp.concatenate(keys, axis=1)
    cand = jnp.dot(sel_ref[...], vals, precision=HIGHEST, preferred_element_type=F32)
    cand_row = lax.broadcasted_iota(jnp.int32, cand.shape, 0)
    cand = jnp.where(cand_row < N_STAIR, cand, -jnp.inf)
    ecand = jnp.dot(sele_ref[...], keys, precision=HIGHEST, preferred_element_type=F32)
    sc, ex = _topk_cols(cand, ecand)
    p = jnp.exp(sc - sc[0:1, :])
    out_rows = pl.ds(pl.multiple_of(h * PEER_TOPK, PEER_TOPK), PEER_TOPK)
    it_scr[out_rows, :] = ex
    gt_scr[out_rows, :] = p / jnp.sum(p, axis=0, keepdims=True)

    @pl.when(h == PEER_HEADS - 1)
    def _():
        idx_ref[...] = it_scr[...].T.astype(jnp.int32)
        g_ref[...] = gt_scr[...].T


def peerq_call(h2, wq_bf, k1, k2, tm=256):
    N = h2.shape[0]
    half = N_KEYS // 2
    k1p = jnp.concatenate([k1, jnp.zeros((N_KEYS, half), F32)], axis=1).astype(BF16)
    k2p = jnp.concatenate([jnp.zeros((N_KEYS, half), F32), k2], axis=1).astype(BF16)
    sel = np.zeros((N_CAND_ROWS, 2 * PEER_TOPK), np.float32)
    sele = np.zeros((N_CAND_ROWS, 2 * PEER_TOPK), np.float32)
    for c, (a, b) in enumerate(_STAIR):
        sel[c, a] = sel[c, PEER_TOPK + b] = 1.0
        sele[c, a] = float(N_KEYS)
        sele[c, PEER_TOPK + b] = 1.0
    const = lambda shape: pl.BlockSpec(shape, lambda i, h: (0, 0))
    row = pl.BlockSpec((tm, HK), lambda i, h: (i, 0))
    return pl.pallas_call(
        _peerq_kernel, grid=(N // tm, PEER_HEADS),
        in_specs=[pl.BlockSpec((tm, D_MODEL), lambda i, h: (i, 0)), const(wq_bf.shape), const((N_KEYS, LANES)),
                  const((N_KEYS, LANES)), const(sel.shape), const(sele.shape)],
        out_specs=[row, row],
        out_shape=[jax.ShapeDtypeStruct((N, HK), jnp.int32), jax.ShapeDtypeStruct((N, HK), F32)],
        scratch_shapes=[pltpu.VMEM((tm, D_MODEL), BF16), pltpu.VMEM((HK, tm), F32), pltpu.VMEM((HK, tm), F32)],
        compiler_params=_cparams(("parallel", "arbitrary")), name="peer_retrieve",
    )(h2, wq_bf, k1p, k2p, jnp.asarray(sel), jnp.asarray(sele))


SC_LANES = 16
SC_CORES = 2
SC_WORKERS = 32
ROWS_PER_GATHER = 16
N_ROW_BUF = 4
LOOKAHEAD = N_ROW_BUF - 1
TOK_BLOCK = 16
ROW_BLOCK = 8
U_UNROLL = 2
COL_BLOCK = 16
N_DCHUNK = D_MODEL // SC_LANES
N_GATHER = HK // ROWS_PER_GATHER


def _sc_gelu(x):
    y = math.sqrt(2.0 / math.pi) * (x + 0.044715 * (x * x * x))
    t = 1.0 - 2.0 / (jnp.exp(2.0 * y) + 1.0)
    return 0.5 * x * (1.0 + t)


def _peer_eval_body(per_w, h_hbm, idx_hbm, g_hbm, u_hbm, v_hbm, o_hbm,
                    idx_blk, g_blk, h_blk, out_blk, w_v, *bufs):
    wid = lax.axis_index("s") * SC_CORES + lax.axis_index("c")
    rows = bufs[:N_ROW_BUF]
    sems = bufs[N_ROW_BUF:]
    lane = lax.iota(jnp.int32, SC_LANES)
    zeros = jnp.zeros((SC_LANES,), F32)

    def gather(table, i, j, buf):
        return pltpu.make_async_copy(table.at[idx_blk.at[i, j]], rows[buf], sems[buf])

    def u_compute(i, n, rbuf):
        for half in range(ROWS_PER_GATHER // SC_LANES):
            actv = zeros
            for rg in range(SC_LANES // ROW_BLOCK):
                r0 = half * SC_LANES + rg * ROW_BLOCK

                def c_body(cb, accs):
                    accs = list(accs)
                    for cc in range(U_UNROLL):
                        sl = pl.ds(pl.multiple_of((cb * U_UNROLL + cc) * SC_LANES, SC_LANES), SC_LANES)
                        hc = h_blk[i, sl]
                        for r in range(ROW_BLOCK):
                            accs[r] = accs[r] + rbuf[r0 + r, sl] * hc
                    return tuple(accs)

                accs = lax.fori_loop(0, N_DCHUNK // U_UNROLL, c_body, (zeros,) * ROW_BLOCK)
                for r in range(ROW_BLOCK):
                    actv = jnp.where(lane == rg * ROW_BLOCK + r, jnp.sum(accs[r]), actv)
            sl = pl.ds(n * ROWS_PER_GATHER + half * SC_LANES, SC_LANES)
            w_v[sl] = g_blk[i, sl] * _sc_gelu(actv)

    def v_compute(i, m, rbuf):
        for cb in range(N_DCHUNK // COL_BLOCK):
            def r_body(r, accs):
                wk = plsc.load_gather(w_v, [jnp.full((SC_LANES,), m * ROWS_PER_GATHER, jnp.int32) + r])
                return tuple(accs[cc] + wk * rbuf[r, pl.ds((cb * COL_BLOCK + cc) * SC_LANES, SC_LANES)]
                             for cc in range(COL_BLOCK))

            accs = lax.fori_loop(0, ROWS_PER_GATHER, r_body, (zeros,) * COL_BLOCK)
            for cc in range(COL_BLOCK):
                sl = pl.ds((cb * COL_BLOCK + cc) * SC_LANES, SC_LANES)
                if m == 0:
                    out_blk[i, sl] = accs[cc]
                else:
                    plsc.addupdate(out_blk.at[i, sl], accs[cc])

    @pl.loop(0, per_w // TOK_BLOCK)
    def _(bi):
        t0 = wid * per_w + bi * TOK_BLOCK
        pltpu.sync_copy(idx_hbm.at[pl.ds(t0, TOK_BLOCK)], idx_blk)
        pltpu.sync_copy(g_hbm.at[pl.ds(t0, TOK_BLOCK)], g_blk)
        pltpu.sync_copy(h_hbm.at[pl.ds(t0, TOK_BLOCK)], h_blk)
        def task_gather(i, n):
            table = u_hbm if n < N_GATHER else v_hbm
            return gather(table, i, n % N_GATHER, n % N_ROW_BUF)

        for n in range(LOOKAHEAD):
            task_gather(0, n).start()

        @pl.loop(0, TOK_BLOCK)
        def _(i):
            for n in range(2 * N_GATHER):
                ahead = n + LOOKAHEAD
                if ahead < 2 * N_GATHER:
                    task_gather(i, ahead).start()
                else:
                    @pl.when(i + 1 < TOK_BLOCK)
                    def _():
                        task_gather(i + 1, ahead - 2 * N_GATHER).start()
                task_gather(i, n).wait()
                if n < N_GATHER:
                    u_compute(i, n, rows[n % N_ROW_BUF])
                else:
                    v_compute(i, n - N_GATHER, rows[n % N_ROW_BUF])

        pltpu.sync_copy(out_blk, o_hbm.at[pl.ds(t0, TOK_BLOCK)])


def peer_eval_call(h2, idx, g, u_tab, v_tab, n_tok=None):
    N = h2.shape[0] if n_tok is None else n_tok
    per_w = N // SC_WORKERS
    assert N % (SC_WORKERS * TOK_BLOCK) == 0
    assert (2 * N_GATHER) % N_ROW_BUF == 0 and LOOKAHEAD < N_ROW_BUF
    mesh = plsc.VectorSubcoreMesh(core_axis_name="c", subcore_axis_name="s",
                                  num_cores=SC_CORES, num_subcores=SC_WORKERS // SC_CORES)
    return pl.kernel(
        functools.partial(_peer_eval_body, per_w),
        out_type=jax.ShapeDtypeStruct((N, D_MODEL), F32),
        mesh=mesh,
        scratch_types=[
            pltpu.VMEM((TOK_BLOCK, N_GATHER, ROWS_PER_GATHER), jnp.int32),
            pltpu.VMEM((TOK_BLOCK, HK), F32),
            pltpu.VMEM((TOK_BLOCK, D_MODEL), F32),
            pltpu.VMEM((TOK_BLOCK, D_MODEL), F32),
            pltpu.VMEM((HK,), F32),
        ] + [pltpu.VMEM((ROWS_PER_GATHER, D_MODEL), F32)] * N_ROW_BUF + [pltpu.SemaphoreType.DMA] * N_ROW_BUF,
        compiler_params=pltpu.CompilerParams(needs_layout_passes=False),
        name="peer_eval",
    )(h2, idx.reshape(h2.shape[0], N_GATHER, ROWS_PER_GATHER), g, u_tab, v_tab)


N_EXPERTS = N_KEYS * N_KEYS
E_HALF = N_EXPERTS // 2
TC_TOK_TILE = 16
TC_EVAL_TOKENS = (1536, 2048)
COMPILER_SCRATCH_BYTES = 2 * 1024 * 1024


def _tc_eval_kernel(idx_s, hi_ref, g_ref, h_ref, u_hbm, v_hbm, o_ref, tab, actw, outp, rsum, wb_all, stage, sem):
    p = pl.program_id(0)
    i = pl.program_id(1)
    for ph, (src, half) in enumerate(((u_hbm, 0), (u_hbm, 1), (v_hbm, 0), (v_hbm, 1))):
        @pl.when(jnp.logical_and(p == ph, i == 0))
        def _():
            cp = pltpu.make_async_copy(src.at[pl.ds(half * E_HALF, E_HALF)], tab, sem)
            cp.start()
            cp.wait()

    lane = lax.broadcasted_iota(jnp.int32, (HK, LANES), 1)
    row = lax.broadcasted_iota(jnp.int32, (HK, LANES), 0)
    in_half = hi_ref[0] == (p % 2)

    @pl.when(p < 2)
    def _():
        acts = jnp.zeros((HK, LANES), F32)
        for t in range(TC_TOK_TILE):
            h = h_ref[t]
            for k in range(HK):
                rsum[t, pl.ds(k, 1), :] = jnp.sum(tab[idx_s[t, k]] * h, axis=0, keepdims=True)
            acts = jnp.where(lane == t, jnp.sum(rsum[t], axis=1, keepdims=True), acts)
        part = jnp.where(in_half, acts, 0.0)

        @pl.when(p == 0)
        def _():
            actw[i] = part

        @pl.when(p == 1)
        def _():
            actw[i] = g_ref[0] * jax.nn.gelu(actw[i] + part)

    @pl.when(p >= 2)
    def _():
        wm = jnp.where(in_half, actw[i], 0.0)

        for t in range(TC_TOK_TILE):
            wb_all[t] = jnp.broadcast_to(wm[:, t:t + 1], (HK, LANES))

        for t in range(TC_TOK_TILE):
            accs = [jnp.zeros((SUBLANES, LANES), F32)] * 4
            for k in range(HK):
                w = jnp.broadcast_to(wb_all[t, pl.ds(k, 1), :], (SUBLANES, LANES))
                accs[k % 4] = accs[k % 4] + w * tab[idx_s[t, k]]
            stage[t] = (accs[0] + accs[1]) + (accs[2] + accs[3])
        tile_toks = pl.ds(pl.multiple_of(i * TC_TOK_TILE, TC_TOK_TILE), TC_TOK_TILE)

        @pl.when(p == 2)
        def _():
            outp[tile_toks] = stage[...]

        @pl.when(p == 3)
        def _():
            o_ref[...] = outp[tile_toks] + stage[...]


def peer_eval_tc_call(h2, idx, g, u_tab, v_tab, first_tok=0):
    n_all = h2.shape[0]
    nt = n_all - first_tok
    n_tiles = nt // TC_TOK_TILE
    first_tile = first_tok // TC_TOK_TILE
    assert nt % TC_TOK_TILE == 0 and first_tok % TC_TOK_TILE == 0 and u_tab.shape[0] == N_EXPERTS
    idx, g = idx[first_tok:], g[first_tok:]

    def pairs_on_sublanes(a, fill):
        a = a.reshape(n_tiles, TC_TOK_TILE, HK).transpose(0, 2, 1)
        return jnp.pad(a, ((0, 0), (0, 0), (0, LANES - TC_TOK_TILE)), constant_values=fill)

    tile3 = lambda: pl.BlockSpec((1, HK, LANES), lambda p, i: (i, 0, 0))
    scratch_shapes = [(E_HALF, SUBLANES, LANES), (n_tiles, HK, LANES), (nt, SUBLANES, LANES),
                      (TC_TOK_TILE, HK, LANES), (TC_TOK_TILE, HK, LANES), (TC_TOK_TILE, SUBLANES, LANES)]
    block_shapes = [(HK, LANES), (HK, LANES), (TC_TOK_TILE, SUBLANES, LANES), (TC_TOK_TILE, SUBLANES, LANES)]
    f32_bytes = 4
    vmem_bytes = (sum(int(np.prod(s)) for s in scratch_shapes) + 2 * sum(int(np.prod(s)) for s in block_shapes)
                  ) * f32_bytes + COMPILER_SCRATCH_BYTES
    assert vmem_bytes <= V7X_VMEM_BYTES
    out = pl.pallas_call(
        _tc_eval_kernel, grid=(4, n_tiles),
        in_specs=[pl.BlockSpec((TC_TOK_TILE, HK), lambda p, i: (i, 0), memory_space=pltpu.SMEM),
                  tile3(), tile3(),
                  pl.BlockSpec((TC_TOK_TILE, SUBLANES, LANES), lambda p, i: (first_tile + i, 0, 0)),
                  pl.BlockSpec(memory_space=pl.ANY), pl.BlockSpec(memory_space=pl.ANY)],
        out_specs=pl.BlockSpec((TC_TOK_TILE, SUBLANES, LANES), lambda p, i: (jnp.where(p == 3, i, 0), 0, 0)),
        out_shape=jax.ShapeDtypeStruct((nt, SUBLANES, LANES), F32),
        scratch_shapes=[pltpu.VMEM(s, F32) for s in scratch_shapes] + [pltpu.SemaphoreType.DMA],
        compiler_params=pltpu.CompilerParams(dimension_semantics=("arbitrary", "arbitrary"),
                                             vmem_limit_bytes=vmem_bytes),
        name="peer_eval_tc",
    )(idx & (E_HALF - 1), pairs_on_sublanes(idx // E_HALF, 2), pairs_on_sublanes(g, 0.0),
      h2.reshape(n_all, SUBLANES, LANES),
      u_tab.reshape(N_EXPERTS, SUBLANES, LANES), v_tab.reshape(N_EXPERTS, SUBLANES, LANES))
    return out.reshape(nt, D_MODEL)


def _final_kernel(x_ref, y_ref, gt_ref, g_ref, o_ref):
    x = x_ref[...] + gt_ref[0] * y_ref[...]
    ms = jnp.mean(x * x, axis=-1, keepdims=True)
    o_ref[...] = x * lax.rsqrt(ms + EPS) * g_ref[...]


def final_call(x, y, mod, gf, T, tm=512):
    N = x.shape[0]
    row = pl.BlockSpec((tm, D_MODEL), lambda i: (i, 0))
    return pl.pallas_call(
        _final_kernel, grid=(N // tm,),
        in_specs=[row, row, _mod_spec(tm, T, 5), pl.BlockSpec((1, D_MODEL), lambda i: (0, 0))],
        out_specs=row, out_shape=jax.ShapeDtypeStruct((N, D_MODEL), F32),
        compiler_params=_cparams(("parallel",)), name="final_norm",
    )(x, y, mod, gf)


def kernel(x, c, ada_w, ada_b, norm1_g, norm2_g, w_in, sgu_w, sgu_b, out_norm_a, out_norm_b, w_out,
           peer_wq, peer_k1, peer_k2, peer_u, peer_v, final_g):
    B, T, Dm = x.shape
    L = ada_w.shape[0]
    N = B * T
    mods = ada_mod(c, ada_w, ada_b).reshape(L, B * 6, 1, Dm)
    assert B % BATCH_GROUPS == 0 and len(TC_EVAL_TOKENS) == L
    Bg = B // BATCH_GROUPS
    gmods = [[mods[l, g * Bg * 6:(g + 1) * Bg * 6] for g in range(BATCH_GROUPS)] for l in range(L)]
    Ng = Bg * T
    xs = [x.reshape(N, Dm)] * BATCH_GROUPS
    ys = [None] * BATCH_GROUPS
    for l in range(L):
        w_in_bf, w_out_bf, wq_bf = w_in[l].astype(BF16), w_out[l].astype(BF16), peer_wq[l].astype(BF16)
        bias_full = jnp.repeat(sgu_b[l].T, GROUP, axis=1)
        for g in range(BATCH_GROUPS):
            mod = gmods[l][g]
            x_first = g * Ng if l == 0 else 0
            outs = proj_call(xs[g], ys[g], gmods[l - 1][g] if l else None, mod, norm1_g[l].reshape(1, Dm), w_in_bf, T,
                             n_tok=Ng, x_first_tok=x_first)
            if l:
                xs[g], outs = outs[0], outs[1:]
            ua, va, q, k, v = outs
            ya = gmlp_call(ua, va, sgu_w[l], bias_full, out_norm_a[l].reshape(1, D_A))
            yb = attn_call(q, k, v, out_norm_b[l].reshape(1, D_B), Bg, T)
            xs[g], h2 = out_call(ya, yb, xs[g], w_out_bf, mod, norm2_g[l].reshape(1, Dm), T, x_first_tok=x_first)
            idx, gate = peerq_call(h2, wq_bf, peer_k1[l], peer_k2[l])
            n_sc = Ng - TC_EVAL_TOKENS[l]
            y_sc = peer_eval_call(h2, idx, gate, peer_u[l], peer_v[l], n_tok=n_sc)
            y_tc = peer_eval_tc_call(h2, idx, gate, peer_u[l], peer_v[l], first_tok=n_sc)
            ys[g] = jnp.concatenate([y_sc, y_tc], axis=0)
    outs = [final_call(xs[g], ys[g], gmods[L - 1][g], final_g.reshape(1, Dm), T) for g in range(BATCH_GROUPS)]
    return jnp.concatenate(outs, axis=0).reshape(B, T, Dm)
```

```python
import functools
import math

import numpy as np
import jax
import jax.numpy as jnp
from jax import lax
from jax.experimental import pallas as pl
from jax.experimental.pallas import tpu as pltpu
from jax.experimental.pallas import tpu_sc as plsc

F32 = jnp.float32
BF16 = jnp.bfloat16
HIGHEST = lax.Precision.HIGHEST

D_MODEL = 1024
D_A = 512
D_B = 512
GROUP = 64
CHUNK = 128
N_KEYS = 128
PEER_HEADS = 8
PEER_TOPK = 16
HK = PEER_HEADS * PEER_TOPK
EPS = 1e-6
LANES = 128
SUBLANES = 8
V7X_VMEM_BYTES = 64 * 1024 * 1024
VMEM_LIMIT = V7X_VMEM_BYTES * 3 // 4
GROUP_BATCHES = (2, 4, 5, 5)

_STAIR = [(a, b) for a in range(PEER_TOPK) for b in range(PEER_TOPK) if (a + 1) * (b + 1) <= PEER_TOPK]
N_STAIR = len(_STAIR)


def _cparams(sem):
    return pltpu.CompilerParams(dimension_semantics=sem, vmem_limit_bytes=VMEM_LIMIT)


def _dot(a, b):
    return jnp.dot(a, b, preferred_element_type=F32)


def _block_diag_mean(n, group):
    i = np.arange(n)
    return jnp.asarray((i[:, None] // group == i[None, :] // group).astype(np.float32) / group, BF16)


def _ada_kernel(c_ref, w_ref, b_ref, o_ref):
    c = c_ref[...]
    ca = c * jax.nn.sigmoid(c)
    o_ref[0] = jnp.dot(ca, w_ref[0], precision=HIGHEST, preferred_element_type=F32) + b_ref[0]


def ada_mod(c, ada_w, ada_b):
    L, Dm, E = ada_w.shape
    Bc = c.shape[0]
    tn = 1536
    return pl.pallas_call(
        _ada_kernel,
        grid=(L, E // tn),
        in_specs=[pl.BlockSpec((Bc, Dm), lambda l, j: (0, 0)),
                  pl.BlockSpec((1, Dm, tn), lambda l, j: (l, 0, j)),
                  pl.BlockSpec((1, 1, tn), lambda l, j: (l, 0, j))],
        out_specs=pl.BlockSpec((1, Bc, tn), lambda l, j: (l, 0, j)),
        out_shape=jax.ShapeDtypeStruct((L, Bc, E), F32),
        compiler_params=_cparams(("parallel", "parallel")),
        name="ada_mod",
    )(c, ada_w, ada_b.reshape(L, 1, E))


def _mod_spec(tm, T, j):
    return pl.BlockSpec((1, 1, D_MODEL), lambda i: (((i * tm) // T) * 6 + j, 0, 0))


def _rms_mod(x, g, sc, sh):
    ms = jnp.mean(x * x, axis=-1, keepdims=True)
    return (x * lax.rsqrt(ms + EPS) * g) * (1.0 + sc) + sh


def _proj_kernel(has_y, *refs):
    if has_y:
        x_ref, y_ref, gt_ref, g_ref, sc_ref, sh_ref, w_ref, xo_ref, ua_ref, va_ref, q_ref, k_ref, v_ref = refs
        x = x_ref[...] + gt_ref[0] * y_ref[...]
        xo_ref[...] = x
    else:
        x_ref, g_ref, sc_ref, sh_ref, w_ref, ua_ref, va_ref, q_ref, k_ref, v_ref = refs
        x = x_ref[...]
    h = _rms_mod(x, g_ref[...], sc_ref[0], sh_ref[0]).astype(BF16)
    outs = (ua_ref, va_ref, q_ref, k_ref, v_ref)
    for j, o_ref in enumerate(outs):
        p = _dot(h, w_ref[:, j * D_A:(j + 1) * D_A])
        if j < 2:
            p = jax.nn.gelu(p)
        o_ref[...] = p.astype(o_ref.dtype)


def proj_call(x, y, mod_prev, mod, g1, w_in_bf, T, tm=512, n_tok=None, x_first_tok=0):
    N = x.shape[0] if n_tok is None else n_tok
    has_y = y is not None
    row = pl.BlockSpec((tm, D_MODEL), lambda i: (i, 0))
    half = pl.BlockSpec((tm, D_A), lambda i: (i, 0))
    in_specs = [pl.BlockSpec((tm, D_MODEL), lambda i: (x_first_tok // tm + i, 0))]
    args = [x]
    if has_y:
        in_specs += [row, _mod_spec(tm, T, 5)]
        args += [y, mod_prev]
    in_specs += [pl.BlockSpec((1, D_MODEL), lambda i: (0, 0)), _mod_spec(tm, T, 1), _mod_spec(tm, T, 0),
                 pl.BlockSpec(w_in_bf.shape, lambda i: (0, 0))]
    args += [g1, mod, mod, w_in_bf]
    out_specs = [half] * 5
    out_shape = [jax.ShapeDtypeStruct((N, D_A), F32)] * 2 + [jax.ShapeDtypeStruct((N, D_A), BF16)] * 3
    if has_y:
        out_specs = [row] + out_specs
        out_shape = [jax.ShapeDtypeStruct((N, D_MODEL), F32)] + out_shape
    return pl.pallas_call(
        functools.partial(_proj_kernel, has_y),
        grid=(N // tm,), in_specs=in_specs, out_specs=out_specs, out_shape=out_shape,
        compiler_params=_cparams(("parallel",)), name="proj",
    )(*args)


def _gmlp_kernel(ua_ref, va_ref, w_ref, bias_ref, a_ref, ga_ref, o_ref):
    A = a_ref[...]
    row = lax.broadcasted_iota(jnp.int32, (CHUNK, CHUNK), 0)
    col = lax.broadcasted_iota(jnp.int32, (CHUNK, CHUNK), 1)
    causal = row >= col
    first_group = col < GROUP
    ws = [jnp.where(causal, w_ref[g], 0.0).astype(BF16) for g in range(D_A // GROUP)]
    bias = bias_ref[...]
    ga = ga_ref[...]
    for c in range(ua_ref.shape[0] // CHUNK):
        rows = slice(c * CHUNK, (c + 1) * CHUNK)
        v = va_ref[rows, :]
        v_hi = v.astype(BF16)
        v_lo = (v - v_hi.astype(F32)).astype(BF16)
        d = v - (_dot(v_hi, A) + _dot(v_lo, A))
        var = _dot((d * d).astype(BF16), A)
        vn = (d * lax.rsqrt(var + EPS)).astype(BF16)
        parts = []
        for p in range(D_A // LANES):
            vp = vn[:, p * LANES:(p + 1) * LANES]
            parts.append(jnp.where(first_group, _dot(ws[2 * p], vp), _dot(ws[2 * p + 1], vp)))
        s = jnp.concatenate(parts, axis=1) + bias
        y = ua_ref[rows, :] * s
        ms = _dot((y * y).astype(BF16), A)
        o_ref[rows, :] = (y * lax.rsqrt(ms + EPS) * ga).astype(BF16)


def gmlp_call(ua, va, sgu_w, bias_full, ga, tm=512):
    N = ua.shape[0]
    half = pl.BlockSpec((tm, D_A), lambda i: (i, 0))
    return pl.pallas_call(
        _gmlp_kernel, grid=(N // tm,),
        in_specs=[half, half,
                  pl.BlockSpec(sgu_w.shape, lambda i: (0, 0, 0)),
                  pl.BlockSpec((CHUNK, D_A), lambda i: (0, 0)),
                  pl.BlockSpec((D_A, D_A), lambda i: (0, 0)),
                  pl.BlockSpec((1, D_A), lambda i: (0, 0))],
        out_specs=half, out_shape=jax.ShapeDtypeStruct((N, D_A), BF16),
        compiler_params=_cparams(("parallel",)), name="gmlp",
    )(ua, va, sgu_w, bias_full, _block_diag_mean(D_A, GROUP), ga)


EXP_UNDERFLOW = -104.0


def _attn_kernel(q_ref, k_ref, v_ref, mo_ref, a_ref, gb_ref, o_ref, *scratch):
    i = pl.program_id(1)
    n_pair = D_B // LANES
    lane = lax.broadcasted_iota(jnp.int32, (CHUNK, LANES), 1)
    row = lax.broadcasted_iota(jnp.int32, (CHUNK, LANES), 0)
    head_lanes = (lane < GROUP, lane >= GROUP)
    MO = mo_ref[...]
    scale = 1.0 / math.sqrt(GROUP)
    acc_refs, carry_refs, qs_refs = scratch[:n_pair], scratch[n_pair:2 * n_pair], scratch[2 * n_pair:]
    for p in range(n_pair):
        acc_refs[p][...] = jnp.zeros_like(acc_refs[p])
        carry_refs[p][...] = jnp.zeros_like(carry_refs[p])
        q = q_ref[:, p * LANES:(p + 1) * LANES]
        qs_refs[p][...] = jnp.concatenate([jnp.where(hl, q, jnp.zeros_like(q)) for hl in head_lanes], axis=0)

    def cond(state):
        jj, cmax = state
        return jnp.logical_and(jj <= i, cmax > EXP_UNDERFLOW)

    def body(state):
        jj, _ = state
        j = i - jj
        start = pl.multiple_of(j * CHUNK, CHUNK)
        mask1 = (j * CHUNK + lane) < (i * CHUNK + row)
        mask = jnp.concatenate([mask1, mask1], axis=0)
        cmax = jnp.full((2 * CHUNK, LANES), -jnp.inf, F32)
        slabs = range(n_pair)
        cols = [slice(p * LANES, (p + 1) * LANES) for p in slabs]
        zs = [lax.dot_general(qs_refs[p][...], k_ref[pl.ds(start, CHUNK), cols[p]], (((1,), (1,)), ((), ())),
                              preferred_element_type=F32) * scale for p in slabs]
        lszs = [jnp.minimum(z, 0.0) - jnp.log1p(jnp.exp(-jnp.abs(z))) for z in zs]
        trs = []
        for p in slabs:
            L = jnp.where(mask, lszs[p] - zs[p], 0.0)
            L_hi = L.astype(BF16)
            L_lo = (L - L_hi.astype(F32)).astype(BF16)
            tr = _dot(jnp.concatenate([L_hi, L_lo], axis=0), MO)
            trs.append(tr[:2 * CHUNK] + tr[2 * CHUNK:])
        for p in slabs:
            c = carry_refs[p][...]
            a = jnp.where(mask, jnp.exp(lszs[p] + trs[p][:, :LANES] + c), 0.0).astype(BF16)
            a_cat = jnp.concatenate([a[:CHUNK], a[CHUNK:]], axis=1)
            vs = v_ref[pl.ds(start, CHUNK), cols[p]]
            v_cat = jnp.concatenate([jnp.where(hl, vs, jnp.zeros_like(vs)) for hl in head_lanes], axis=0)
            acc_refs[p][...] += _dot(a_cat, v_cat)
            c = c + trs[p][:, LANES:]
            carry_refs[p][...] = c
            cmax = jnp.maximum(cmax, c)
        return jj + 1, jnp.max(cmax)

    lax.while_loop(cond, body, (jnp.int32(0), jnp.float32(0.0)))
    for p in range(n_pair):
        cols = slice(p * LANES, (p + 1) * LANES)
        acc = acc_refs[p][...]
        ms = _dot((acc * acc).astype(BF16), a_ref[...])
        o_ref[:, cols] = (acc * lax.rsqrt(ms + EPS) * gb_ref[:, cols]).astype(BF16)


def attn_call(q, k, v, gb, B, T):
    N = q.shape[0]
    nq = T // CHUNK
    i = np.arange(CHUNK)
    MO = jnp.asarray(np.concatenate([(i[:, None] > i[None, :]), np.ones((CHUNK, LANES), bool)], axis=1)
                     .astype(np.float32), BF16)
    kv = pl.BlockSpec((T, D_B), lambda b, i: (b, 0))
    qo = pl.BlockSpec((CHUNK, D_B), lambda b, i: (b * nq + i, 0))
    n_pair = D_B // LANES
    return pl.pallas_call(
        _attn_kernel, grid=(B, nq),
        in_specs=[qo, kv, kv, pl.BlockSpec((CHUNK, 2 * LANES), lambda b, i: (0, 0)),
                  pl.BlockSpec((CHUNK, LANES), lambda b, i: (0, 0)), pl.BlockSpec((1, D_B), lambda b, i: (0, 0))],
        out_specs=qo,
        out_shape=jax.ShapeDtypeStruct((N, D_B), BF16),
        scratch_shapes=([pltpu.VMEM((CHUNK, LANES), F32)] * n_pair + [pltpu.VMEM((2 * CHUNK, LANES), F32)] * n_pair
                        + [pltpu.VMEM((2 * CHUNK, LANES), BF16)] * n_pair),
        compiler_params=_cparams(("parallel", "arbitrary")), name="stickbreak_attn",
    )(q, k, v, MO, _block_diag_mean(LANES, GROUP), gb)


def _out_kernel(ya_ref, yb_ref, x_ref, w_ref, gt_ref, g2_ref, sc_ref, sh_ref, xo_ref, h2_ref):
    o = _dot(ya_ref[...], w_ref[:D_A, :]) + _dot(yb_ref[...], w_ref[D_A:, :])
    x = x_ref[...] + gt_ref[0] * o
    xo_ref[...] = x
    h2_ref[...] = _rms_mod(x, g2_ref[...], sc_ref[0], sh_ref[0])


def out_call(ya, yb, x, w_out_bf, mod, g2, T, tm=512, x_first_tok=0):
    N = ya.shape[0]
    row = pl.BlockSpec((tm, D_MODEL), lambda i: (i, 0))
    x_row = pl.BlockSpec((tm, D_MODEL), lambda i: (x_first_tok // tm + i, 0))
    half = pl.BlockSpec((tm, D_A), lambda i: (i, 0))
    return pl.pallas_call(
        _out_kernel, grid=(N // tm,),
        in_specs=[half, half, x_row, pl.BlockSpec(w_out_bf.shape, lambda i: (0, 0)), _mod_spec(tm, T, 2),
                  pl.BlockSpec((1, D_MODEL), lambda i: (0, 0)), _mod_spec(tm, T, 4), _mod_spec(tm, T, 3)],
        out_specs=[row, row], out_shape=[jax.ShapeDtypeStruct((N, D_MODEL), F32)] * 2,
        compiler_params=_cparams(("parallel",)), name="out_proj",
    )(ya, yb, x, w_out_bf, mod, g2, mod, mod)


N_CAND_ROWS = -(-N_STAIR // SUBLANES) * SUBLANES


def _topk_cols(s, payload=None):
    n_rows, tb = s.shape
    row_f = lax.broadcasted_iota(jnp.int32, (n_rows, tb), 0).astype(F32)
    rank = lax.broadcasted_iota(jnp.int32, (PEER_TOPK, tb), 0)
    neg = jnp.float32(-jnp.inf)

    def body(r, carry):
        s, vals, second = carry
        m = jnp.max(s, axis=0, keepdims=True)
        pos = jnp.min(jnp.where(s == m, row_f, float(n_rows)), axis=0, keepdims=True)
        hit = row_f == pos
        out = pos if payload is None else jnp.max(jnp.where(hit, payload, -1.0), axis=0, keepdims=True)
        dst = rank == r
        return jnp.where(hit, neg, s), jnp.where(dst, m, vals), jnp.where(dst, out, second)

    zeros = jnp.zeros((PEER_TOPK, tb), F32)
    _, vals, second = lax.fori_loop(0, PEER_TOPK, body, (s, zeros, zeros))
    return vals, second


def _peerq_kernel(h2_ref, wq_ref, k1_ref, k2_ref, sel_ref, sele_ref, idx_ref, g_ref, q_scr, it_scr, gt_scr):
    h = pl.program_id(1)
    tm = h2_ref.shape[0]

    @pl.when(h == 0)
    def _():
        q_scr[...] = _dot(h2_ref[...].astype(BF16), wq_ref[...]).astype(BF16)

    nt = (((1,), (1,)), ((), ()))
    vals, keys = [], []
    for tb in range(tm // LANES):
        qh = q_scr[tb * LANES:(tb + 1) * LANES, pl.ds(pl.multiple_of(h * LANES, LANES), LANES)]
        s12 = jnp.concatenate([lax.dot_general(k1_ref[...], qh, nt, preferred_element_type=F32),
                               lax.dot_general(k2_ref[...], qh, nt, preferred_element_type=F32)], axis=1)
        v12, i12 = _topk_cols(s12)
        vals.append(jnp.concatenate([v12[:, :LANES], v12[:, LANES:]], axis=0))
        keys.append(jnp.concatenate([i12[:, :LANES], i12[:, LANES:]], axis=0))
    vals = jnp.concatenate(vals, axis=1)
    keys = jnp.concatenate(keys, axis=1)
    cand = jnp.dot(sel_ref[...], vals, precision=HIGHEST, preferred_element_type=F32)
    cand_row = lax.broadcasted_iota(jnp.int32, cand.shape, 0)
    cand = jnp.where(cand_row < N_STAIR, cand, -jnp.inf)
    ecand = jnp.dot(sele_ref[...], keys, precision=HIGHEST, preferred_element_type=F32)
    sc, ex = _topk_cols(cand, ecand)
    p = jnp.exp(sc - sc[0:1, :])
    out_rows = pl.ds(pl.multiple_of(h * PEER_TOPK, PEER_TOPK), PEER_TOPK)
    it_scr[out_rows, :] = ex
    gt_scr[out_rows, :] = p / jnp.sum(p, axis=0, keepdims=True)

    @pl.when(h == PEER_HEADS - 1)
    def _():
        idx_ref[...] = it_scr[...].T.astype(jnp.int32)
        g_ref[...] = gt_scr[...].T


def peerq_call(h2, wq_bf, k1, k2, tm=256):
    N = h2.shape[0]
    half = N_KEYS // 2
    k1p = jnp.concatenate([k1, jnp.zeros((N_KEYS, half), F32)], axis=1).astype(BF16)
    k2p = jnp.concatenate([jnp.zeros((N_KEYS, half), F32), k2], axis=1).astype(BF16)
    sel = np.zeros((N_CAND_ROWS, 2 * PEER_TOPK), np.float32)
    sele = np.zeros((N_CAND_ROWS, 2 * PEER_TOPK), np.float32)
    for c, (a, b) in enumerate(_STAIR):
        sel[c, a] = sel[c, PEER_TOPK + b] = 1.0
        sele[c, a] = float(N_KEYS)
        sele[c, PEER_TOPK + b] = 1.0
    const = lambda shape: pl.BlockSpec(shape, lambda i, h: (0, 0))
    row = pl.BlockSpec((tm, HK), lambda i, h: (i, 0))
    return pl.pallas_call(
        _peerq_kernel, grid=(N // tm, PEER_HEADS),
        in_specs=[pl.BlockSpec((tm, D_MODEL), lambda i, h: (i, 0)), const(wq_bf.shape), const((N_KEYS, LANES)),
                  const((N_KEYS, LANES)), const(sel.shape), const(sele.shape)],
        out_specs=[row, row],
        out_shape=[jax.ShapeDtypeStruct((N, HK), jnp.int32), jax.ShapeDtypeStruct((N, HK), F32)],
        scratch_shapes=[pltpu.VMEM((tm, D_MODEL), BF16), pltpu.VMEM((HK, tm), F32), pltpu.VMEM((HK, tm), F32)],
        compiler_params=_cparams(("parallel", "arbitrary")), name="peer_retrieve",
    )(h2, wq_bf, k1p, k2p, jnp.asarray(sel), jnp.asarray(sele))


SC_LANES = 16
SC_CORES = 2
SC_WORKERS = 32
ROWS_PER_GATHER = 16
N_ROW_BUF = 4
LOOKAHEAD = N_ROW_BUF - 1
TOK_BLOCK = 16
ROW_BLOCK = 8
U_UNROLL = 2
COL_BLOCK = 16
N_DCHUNK = D_MODEL // SC_LANES
N_GATHER = HK // ROWS_PER_GATHER


def _sc_gelu(x):
    y = math.sqrt(2.0 / math.pi) * (x + 0.044715 * (x * x * x))
    t = 1.0 - 2.0 / (jnp.exp(2.0 * y) + 1.0)
    return 0.5 * x * (1.0 + t)


def _peer_eval_body(per_w, h_hbm, idx_hbm, g_hbm, u_hbm, v_hbm, o_hbm,
                    idx_blk, g_blk, h_blk, out_blk, w_v, *bufs):
    wid = lax.axis_index("s") * SC_CORES + lax.axis_index("c")
    rows = bufs[:N_ROW_BUF]
    sems = bufs[N_ROW_BUF:]
    lane = lax.iota(jnp.int32, SC_LANES)
    zeros = jnp.zeros((SC_LANES,), F32)

    def gather(table, i, j, buf):
        return pltpu.make_async_copy(table.at[idx_blk.at[i, j]], rows[buf], sems[buf])

    def u_compute(i, n, rbuf):
        for half in range(ROWS_PER_GATHER // SC_LANES):
            actv = zeros
            for rg in range(SC_LANES // ROW_BLOCK):
                r0 = half * SC_LANES + rg * ROW_BLOCK

                def c_body(cb, accs):
                    accs = list(accs)
                    for cc in range(U_UNROLL):
                        sl = pl.ds(pl.multiple_of((cb * U_UNROLL + cc) * SC_LANES, SC_LANES), SC_LANES)
                        hc = h_blk[i, sl]
                        for r in range(ROW_BLOCK):
                            accs[r] = accs[r] + rbuf[r0 + r, sl] * hc
                    return tuple(accs)

                accs = lax.fori_loop(0, N_DCHUNK // U_UNROLL, c_body, (zeros,) * ROW_BLOCK)
                for r in range(ROW_BLOCK):
                    actv = jnp.where(lane == rg * ROW_BLOCK + r, jnp.sum(accs[r]), actv)
            sl = pl.ds(n * ROWS_PER_GATHER + half * SC_LANES, SC_LANES)
            w_v[sl] = g_blk[i, sl] * _sc_gelu(actv)

    def v_compute(i, m, rbuf):
        for cb in range(N_DCHUNK // COL_BLOCK):
            def r_body(r, accs):
                wk = plsc.load_gather(w_v, [jnp.full((SC_LANES,), m * ROWS_PER_GATHER, jnp.int32) + r])
                return tuple(accs[cc] + wk * rbuf[r, pl.ds((cb * COL_BLOCK + cc) * SC_LANES, SC_LANES)]
                             for cc in range(COL_BLOCK))

            accs = lax.fori_loop(0, ROWS_PER_GATHER, r_body, (zeros,) * COL_BLOCK)
            for cc in range(COL_BLOCK):
                sl = pl.ds((cb * COL_BLOCK + cc) * SC_LANES, SC_LANES)
                if m == 0:
                    out_blk[i, sl] = accs[cc]
                else:
                    plsc.addupdate(out_blk.at[i, sl], accs[cc])

    @pl.loop(0, per_w // TOK_BLOCK)
    def _(bi):
        t0 = wid * per_w + bi * TOK_BLOCK
        pltpu.sync_copy(idx_hbm.at[pl.ds(t0, TOK_BLOCK)], idx_blk)
        pltpu.sync_copy(g_hbm.at[pl.ds(t0, TOK_BLOCK)], g_blk)
        pltpu.sync_copy(h_hbm.at[pl.ds(t0, TOK_BLOCK)], h_blk)
        def task_gather(i, n):
            table = u_hbm if n < N_GATHER else v_hbm
            return gather(table, i, n % N_GATHER, n % N_ROW_BUF)

        for n in range(LOOKAHEAD):
            task_gather(0, n).start()

        @pl.loop(0, TOK_BLOCK)
        def _(i):
            for n in range(2 * N_GATHER):
                ahead = n + LOOKAHEAD
                if ahead < 2 * N_GATHER:
                    task_gather(i, ahead).start()
                else:
                    @pl.when(i + 1 < TOK_BLOCK)
                    def _():
                        task_gather(i + 1, ahead - 2 * N_GATHER).start()
                task_gather(i, n).wait()
                if n < N_GATHER:
                    u_compute(i, n, rows[n % N_ROW_BUF])
                else:
                    v_compute(i, n - N_GATHER, rows[n % N_ROW_BUF])

        pltpu.sync_copy(out_blk, o_hbm.at[pl.ds(t0, TOK_BLOCK)])


def peer_eval_call(h2, idx, g, u_tab, v_tab, n_tok=None, table_row0=0):
    N = h2.shape[0] if n_tok is None else n_tok
    per_w = N // SC_WORKERS
    assert N % (SC_WORKERS * TOK_BLOCK) == 0
    assert (2 * N_GATHER) % N_ROW_BUF == 0 and LOOKAHEAD < N_ROW_BUF
    mesh = plsc.VectorSubcoreMesh(core_axis_name="c", subcore_axis_name="s",
                                  num_cores=SC_CORES, num_subcores=SC_WORKERS // SC_CORES)
    return pl.kernel(
        functools.partial(_peer_eval_body, per_w),
        out_type=jax.ShapeDtypeStruct((N, D_MODEL), F32),
        mesh=mesh,
        scratch_types=[
            pltpu.VMEM((TOK_BLOCK, N_GATHER, ROWS_PER_GATHER), jnp.int32),
            pltpu.VMEM((TOK_BLOCK, HK), F32),
            pltpu.VMEM((TOK_BLOCK, D_MODEL), F32),
            pltpu.VMEM((TOK_BLOCK, D_MODEL), F32),
            pltpu.VMEM((HK,), F32),
        ] + [pltpu.VMEM((ROWS_PER_GATHER, D_MODEL), F32)] * N_ROW_BUF + [pltpu.SemaphoreType.DMA] * N_ROW_BUF,
        compiler_params=pltpu.CompilerParams(needs_layout_passes=False),
        name="peer_eval",
    )(h2, (idx + table_row0).reshape(h2.shape[0], N_GATHER, ROWS_PER_GATHER), g, u_tab, v_tab)


N_EXPERTS = N_KEYS * N_KEYS
E_HALF = N_EXPERTS // 2
TC_TOK_TILE = 16
TC_EVAL_SHARE = (0.1875, 0.25)
COMPILER_SCRATCH_BYTES = 2 * 1024 * 1024


def _tc_eval_kernel(table_row0, idx_s, hi_ref, g_ref, h_ref, u_hbm, v_hbm, o_ref,
                    tab, actw, outp, rsum, wb_all, stage, sem):
    p = pl.program_id(0)
    i = pl.program_id(1)
    for ph, (src, half) in enumerate(((u_hbm, 0), (u_hbm, 1), (v_hbm, 0), (v_hbm, 1))):
        @pl.when(jnp.logical_and(p == ph, i == 0))
        def _():
            cp = pltpu.make_async_copy(src.at[pl.ds(table_row0 + half * E_HALF, E_HALF)], tab, sem)
            cp.start()
            cp.wait()

    lane = lax.broadcasted_iota(jnp.int32, (HK, LANES), 1)
    row = lax.broadcasted_iota(jnp.int32, (HK, LANES), 0)
    in_half = hi_ref[0] == (p % 2)

    @pl.when(p < 2)
    def _():
        acts = jnp.zeros((HK, LANES), F32)
        for t in range(TC_TOK_TILE):
            h = h_ref[t]
            for k in range(HK):
                rsum[t, pl.ds(k, 1), :] = jnp.sum(tab[idx_s[t, k]] * h, axis=0, keepdims=True)
            acts = jnp.where(lane == t, jnp.sum(rsum[t], axis=1, keepdims=True), acts)
        part = jnp.where(in_half, acts, 0.0)

        @pl.when(p == 0)
        def _():
            actw[i] = part

        @pl.when(p == 1)
        def _():
            actw[i] = g_ref[0] * jax.nn.gelu(actw[i] + part)

    @pl.when(p >= 2)
    def _():
        wm = jnp.where(in_half, actw[i], 0.0)

        for t in range(TC_TOK_TILE):
            wb_all[t] = jnp.broadcast_to(wm[:, t:t + 1], (HK, LANES))

        for t in range(TC_TOK_TILE):
            accs = [jnp.zeros((SUBLANES, LANES), F32)] * 4
            for k in range(HK):
                w = jnp.broadcast_to(wb_all[t, pl.ds(k, 1), :], (SUBLANES, LANES))
                accs[k % 4] = accs[k % 4] + w * tab[idx_s[t, k]]
            stage[t] = (accs[0] + accs[1]) + (accs[2] + accs[3])
        tile_toks = pl.ds(pl.multiple_of(i * TC_TOK_TILE, TC_TOK_TILE), TC_TOK_TILE)

        @pl.when(p == 2)
        def _():
            outp[tile_toks] = stage[...]

        @pl.when(p == 3)
        def _():
            o_ref[...] = outp[tile_toks] + stage[...]


def peer_eval_tc_call(h2, idx, g, u_tab, v_tab, first_tok=0, table_row0=0):
    n_all = h2.shape[0]
    nt = n_all - first_tok
    n_tiles = nt // TC_TOK_TILE
    first_tile = first_tok // TC_TOK_TILE
    n_rows = u_tab.shape[0]
    assert nt % TC_TOK_TILE == 0 and first_tok % TC_TOK_TILE == 0 and table_row0 + N_EXPERTS <= n_rows
    idx, g = idx[first_tok:], g[first_tok:]

    def pairs_on_sublanes(a, fill):
        a = a.reshape(n_tiles, TC_TOK_TILE, HK).transpose(0, 2, 1)
        return jnp.pad(a, ((0, 0), (0, 0), (0, LANES - TC_TOK_TILE)), constant_values=fill)

    tile3 = lambda: pl.BlockSpec((1, HK, LANES), lambda p, i: (i, 0, 0))
    scratch_shapes = [(E_HALF, SUBLANES, LANES), (n_tiles, HK, LANES), (nt, SUBLANES, LANES),
                      (TC_TOK_TILE, HK, LANES), (TC_TOK_TILE, HK, LANES), (TC_TOK_TILE, SUBLANES, LANES)]
    block_shapes = [(HK, LANES), (HK, LANES), (TC_TOK_TILE, SUBLANES, LANES), (TC_TOK_TILE, SUBLANES, LANES)]
    f32_bytes = 4
    vmem_bytes = (sum(int(np.prod(s)) for s in scratch_shapes) + 2 * sum(int(np.prod(s)) for s in block_shapes)
                  ) * f32_bytes + COMPILER_SCRATCH_BYTES
    assert vmem_bytes <= V7X_VMEM_BYTES
    out = pl.pallas_call(
        functools.partial(_tc_eval_kernel, table_row0), grid=(4, n_tiles),
        in_specs=[pl.BlockSpec((TC_TOK_TILE, HK), lambda p, i: (i, 0), memory_space=pltpu.SMEM),
                  tile3(), tile3(),
                  pl.BlockSpec((TC_TOK_TILE, SUBLANES, LANES), lambda p, i: (first_tile + i, 0, 0)),
                  pl.BlockSpec(memory_space=pl.ANY), pl.BlockSpec(memory_space=pl.ANY)],
        out_specs=pl.BlockSpec((TC_TOK_TILE, SUBLANES, LANES), lambda p, i: (jnp.where(p == 3, i, 0), 0, 0)),
        out_shape=jax.ShapeDtypeStruct((nt, SUBLANES, LANES), F32),
        scratch_shapes=[pltpu.VMEM(s, F32) for s in scratch_shapes] + [pltpu.SemaphoreType.DMA],
        compiler_params=pltpu.CompilerParams(dimension_semantics=("arbitrary", "arbitrary"),
                                             vmem_limit_bytes=vmem_bytes),
        name="peer_eval_tc",
    )(idx & (E_HALF - 1), pairs_on_sublanes(idx // E_HALF, 2), pairs_on_sublanes(g, 0.0),
      h2.reshape(n_all, SUBLANES, LANES),
      u_tab.reshape(n_rows, SUBLANES, LANES), v_tab.reshape(n_rows, SUBLANES, LANES))
    return out.reshape(nt, D_MODEL)


def _final_kernel(x_ref, y_ref, gt_ref, g_ref, o_ref):
    x = x_ref[...] + gt_ref[0] * y_ref[...]
    ms = jnp.mean(x * x, axis=-1, keepdims=True)
    o_ref[...] = x * lax.rsqrt(ms + EPS) * g_ref[...]


def final_call(x, y, mod, gf, T, tm=512):
    N = x.shape[0]
    row = pl.BlockSpec((tm, D_MODEL), lambda i: (i, 0))
    return pl.pallas_call(
        _final_kernel, grid=(N // tm,),
        in_specs=[row, row, _mod_spec(tm, T, 5), pl.BlockSpec((1, D_MODEL), lambda i: (0, 0))],
        out_specs=row, out_shape=jax.ShapeDtypeStruct((N, D_MODEL), F32),
        compiler_params=_cparams(("parallel",)), name="final_norm",
    )(x, y, mod, gf)


def kernel(x, c, ada_w, ada_b, norm1_g, norm2_g, w_in, sgu_w, sgu_b, out_norm_a, out_norm_b, w_out,
           peer_wq, peer_k1, peer_k2, peer_u, peer_v, final_g):
    B, T, Dm = x.shape
    L = ada_w.shape[0]
    N = B * T
    mods = ada_mod(c, ada_w, ada_b).reshape(L, B * 6, 1, Dm)
    assert sum(GROUP_BATCHES) == B and len(TC_EVAL_SHARE) == L
    n_groups = len(GROUP_BATCHES)
    first_batch = [sum(GROUP_BATCHES[:g]) for g in range(n_groups)]
    gmods = [[mods[l, first_batch[g] * 6:(first_batch[g] + GROUP_BATCHES[g]) * 6] for g in range(n_groups)]
             for l in range(L)]
    xs = [x.reshape(N, Dm)] * n_groups
    ys = [None] * n_groups
    n_exp = peer_u.shape[1]
    u_all = peer_u.reshape(L * n_exp, Dm)
    v_all = peer_v.reshape(L * n_exp, Dm)
    sc_step = SC_WORKERS * TOK_BLOCK
    for l in range(L):
        w_in_bf, w_out_bf, wq_bf = w_in[l].astype(BF16), w_out[l].astype(BF16), peer_wq[l].astype(BF16)
        bias_full = jnp.repeat(sgu_b[l].T, GROUP, axis=1)
        for g in range(n_groups):
            Bg = GROUP_BATCHES[g]
            Ng = Bg * T
            mod = gmods[l][g]
            x_first = first_batch[g] * T if l == 0 else 0
            outs = proj_call(xs[g], ys[g], gmods[l - 1][g] if l else None, mod, norm1_g[l].reshape(1, Dm), w_in_bf, T,
                             n_tok=Ng, x_first_tok=x_first)
            if l:
                xs[g], outs = outs[0], outs[1:]
            ua, va, q, k, v = outs
            ya = gmlp_call(ua, va, sgu_w[l], bias_full, out_norm_a[l].reshape(1, D_A))
            yb = attn_call(q, k, v, out_norm_b[l].reshape(1, D_B), Bg, T)
            xs[g], h2 = out_call(ya, yb, xs[g], w_out_bf, mod, norm2_g[l].reshape(1, Dm), T, x_first_tok=x_first)
            idx, gate = peerq_call(h2, wq_bf, peer_k1[l], peer_k2[l])
            n_sc = int(Ng * (1.0 - TC_EVAL_SHARE[l])) // sc_step * sc_step
            y_sc = peer_eval_call(h2, idx, gate, u_all, v_all, n_tok=n_sc, table_row0=l * n_exp)
            y_tc = peer_eval_tc_call(h2, idx, gate, u_all, v_all, first_tok=n_sc, table_row0=l * n_exp)
            ys[g] = jnp.concatenate([y_sc, y_tc], axis=0)
    outs = [final_call(xs[g], ys[g], gmods[L - 1][g], final_g.reshape(1, Dm), T) for g in range(n_groups)]
    return jnp.concatenate(outs, axis=0).reshape(B, T, Dm)
```

```python
import functools
import math

import numpy as np
import jax
import jax.numpy as jnp
from jax import lax
from jax.experimental import pallas as pl
from jax.experimental.pallas import tpu as pltpu
from jax.experimental.pallas import tpu_sc as plsc

F32 = jnp.float32
BF16 = jnp.bfloat16
HIGHEST = lax.Precision.HIGHEST

D_MODEL = 1024
D_A = 512
D_B = 512
GROUP = 64
CHUNK = 128
N_KEYS = 128
PEER_HEADS = 8
PEER_TOPK = 16
HK = PEER_HEADS * PEER_TOPK
EPS = 1e-6
LANES = 128
SUBLANES = 8
V7X_VMEM_BYTES = 64 * 1024 * 1024
VMEM_LIMIT = V7X_VMEM_BYTES * 3 // 4
GROUP_BATCHES = (5, 5, 4, 2)

_STAIR = [(a, b) for a in range(PEER_TOPK) for b in range(PEER_TOPK) if (a + 1) * (b + 1) <= PEER_TOPK]
N_STAIR = len(_STAIR)


def _cparams(sem):
    return pltpu.CompilerParams(dimension_semantics=sem, vmem_limit_bytes=VMEM_LIMIT)


def _dot(a, b):
    return jnp.dot(a, b, preferred_element_type=F32)


def _block_diag_mean(n, group):
    i = np.arange(n)
    return jnp.asarray((i[:, None] // group == i[None, :] // group).astype(np.float32) / group, BF16)


def _ada_kernel(c_ref, w_ref, b_ref, o_ref):
    c = c_ref[...]
    ca = c * jax.nn.sigmoid(c)
    o_ref[0] = jnp.dot(ca, w_ref[0], precision=HIGHEST, preferred_element_type=F32) + b_ref[0]


def ada_mod(c, ada_w, ada_b):
    L, Dm, E = ada_w.shape
    Bc = c.shape[0]
    tn = 1536
    return pl.pallas_call(
        _ada_kernel,
        grid=(L, E // tn),
        in_specs=[pl.BlockSpec((Bc, Dm), lambda l, j: (0, 0)),
                  pl.BlockSpec((1, Dm, tn), lambda l, j: (l, 0, j)),
                  pl.BlockSpec((1, 1, tn), lambda l, j: (l, 0, j))],
        out_specs=pl.BlockSpec((1, Bc, tn), lambda l, j: (l, 0, j)),
        out_shape=jax.ShapeDtypeStruct((L, Bc, E), F32),
        compiler_params=_cparams(("parallel", "parallel")),
        name="ada_mod",
    )(c, ada_w, ada_b.reshape(L, 1, E))


def _mod_spec(tm, T, j):
    return pl.BlockSpec((1, 1, D_MODEL), lambda i: (((i * tm) // T) * 6 + j, 0, 0))


def _rms_mod(x, g, sc, sh):
    ms = jnp.mean(x * x, axis=-1, keepdims=True)
    return (x * lax.rsqrt(ms + EPS) * g) * (1.0 + sc) + sh


def _proj_kernel(has_y, *refs):
    if has_y:
        x_ref, y_ref, gt_ref, g_ref, sc_ref, sh_ref, w_ref, xo_ref, ua_ref, va_ref, q_ref, k_ref, v_ref = refs
        x = x_ref[...] + gt_ref[0] * y_ref[...]
        xo_ref[...] = x
    else:
        x_ref, g_ref, sc_ref, sh_ref, w_ref, ua_ref, va_ref, q_ref, k_ref, v_ref = refs
        x = x_ref[...]
    h = _rms_mod(x, g_ref[...], sc_ref[0], sh_ref[0]).astype(BF16)
    outs = (ua_ref, va_ref, q_ref, k_ref, v_ref)
    for j, o_ref in enumerate(outs):
        p = _dot(h, w_ref[:, j * D_A:(j + 1) * D_A])
        if j < 2:
            p = jax.nn.gelu(p)
        o_ref[...] = p.astype(o_ref.dtype)


def proj_call(x, y, mod_prev, mod, g1, w_in_bf, T, tm=512, n_tok=None, x_first_tok=0):
    N = x.shape[0] if n_tok is None else n_tok
    has_y = y is not None
    row = pl.BlockSpec((tm, D_MODEL), lambda i: (i, 0))
    half = pl.BlockSpec((tm, D_A), lambda i: (i, 0))
    in_specs = [pl.BlockSpec((tm, D_MODEL), lambda i: (x_first_tok // tm + i, 0))]
    args = [x]
    if has_y:
        in_specs += [row, _mod_spec(tm, T, 5)]
        args += [y, mod_prev]
    in_specs += [pl.BlockSpec((1, D_MODEL), lambda i: (0, 0)), _mod_spec(tm, T, 1), _mod_spec(tm, T, 0),
                 pl.BlockSpec(w_in_bf.shape, lambda i: (0, 0))]
    args += [g1, mod, mod, w_in_bf]
    out_specs = [half] * 5
    out_shape = [jax.ShapeDtypeStruct((N, D_A), F32)] * 2 + [jax.ShapeDtypeStruct((N, D_A), BF16)] * 3
    if has_y:
        out_specs = [row] + out_specs
        out_shape = [jax.ShapeDtypeStruct((N, D_MODEL), F32)] + out_shape
    return pl.pallas_call(
        functools.partial(_proj_kernel, has_y),
        grid=(N // tm,), in_specs=in_specs, out_specs=out_specs, out_shape=out_shape,
        compiler_params=_cparams(("parallel",)), name="proj",
    )(*args)


def _gmlp_kernel(ua_ref, va_ref, w_ref, bias_ref, a_ref, ga_ref, o_ref):
    A = a_ref[...]
    row = lax.broadcasted_iota(jnp.int32, (CHUNK, CHUNK), 0)
    col = lax.broadcasted_iota(jnp.int32, (CHUNK, CHUNK), 1)
    causal = row >= col
    first_group = col < GROUP
    ws = [jnp.where(causal, w_ref[g], 0.0).astype(BF16) for g in range(D_A // GROUP)]
    bias = bias_ref[...]
    ga = ga_ref[...]
    for c in range(ua_ref.shape[0] // CHUNK):
        rows = slice(c * CHUNK, (c + 1) * CHUNK)
        v = va_ref[rows, :]
        v_hi = v.astype(BF16)
        v_lo = (v - v_hi.astype(F32)).astype(BF16)
        d = v - (_dot(v_hi, A) + _dot(v_lo, A))
        var = _dot((d * d).astype(BF16), A)
        vn = (d * lax.rsqrt(var + EPS)).astype(BF16)
        parts = []
        for p in range(D_A // LANES):
            vp = vn[:, p * LANES:(p + 1) * LANES]
            parts.append(jnp.where(first_group, _dot(ws[2 * p], vp), _dot(ws[2 * p + 1], vp)))
        s = jnp.concatenate(parts, axis=1) + bias
        y = ua_ref[rows, :] * s
        ms = _dot((y * y).astype(BF16), A)
        o_ref[rows, :] = (y * lax.rsqrt(ms + EPS) * ga).astype(BF16)


def gmlp_call(ua, va, sgu_w, bias_full, ga, tm=512):
    N = ua.shape[0]
    half = pl.BlockSpec((tm, D_A), lambda i: (i, 0))
    return pl.pallas_call(
        _gmlp_kernel, grid=(N // tm,),
        in_specs=[half, half,
                  pl.BlockSpec(sgu_w.shape, lambda i: (0, 0, 0)),
                  pl.BlockSpec((CHUNK, D_A), lambda i: (0, 0)),
                  pl.BlockSpec((D_A, D_A), lambda i: (0, 0)),
                  pl.BlockSpec((1, D_A), lambda i: (0, 0))],
        out_specs=half, out_shape=jax.ShapeDtypeStruct((N, D_A), BF16),
        compiler_params=_cparams(("parallel",)), name="gmlp",
    )(ua, va, sgu_w, bias_full, _block_diag_mean(D_A, GROUP), ga)


EXP_UNDERFLOW = -104.0


def _attn_kernel(q_ref, k_ref, v_ref, mo_ref, a_ref, gb_ref, o_ref, *scratch):
    i = pl.program_id(1)
    n_pair = D_B // LANES
    lane = lax.broadcasted_iota(jnp.int32, (CHUNK, LANES), 1)
    row = lax.broadcasted_iota(jnp.int32, (CHUNK, LANES), 0)
    head_lanes = (lane < GROUP, lane >= GROUP)
    MO = mo_ref[...]
    scale = 1.0 / math.sqrt(GROUP)
    acc_refs, carry_refs, qs_refs = scratch[:n_pair], scratch[n_pair:2 * n_pair], scratch[2 * n_pair:]
    for p in range(n_pair):
        acc_refs[p][...] = jnp.zeros_like(acc_refs[p])
        carry_refs[p][...] = jnp.zeros_like(carry_refs[p])
        q = q_ref[:, p * LANES:(p + 1) * LANES]
        qs_refs[p][...] = jnp.concatenate([jnp.where(hl, q, jnp.zeros_like(q)) for hl in head_lanes], axis=0)

    def cond(state):
        jj, cmax = state
        return jnp.logical_and(jj <= i, cmax > EXP_UNDERFLOW)

    def body(state):
        jj, _ = state
        j = i - jj
        start = pl.multiple_of(j * CHUNK, CHUNK)
        mask1 = (j * CHUNK + lane) < (i * CHUNK + row)
        mask = jnp.concatenate([mask1, mask1], axis=0)
        cmax = jnp.full((2 * CHUNK, LANES), -jnp.inf, F32)
        slabs = range(n_pair)
        cols = [slice(p * LANES, (p + 1) * LANES) for p in slabs]
        zs = [lax.dot_general(qs_refs[p][...], k_ref[pl.ds(start, CHUNK), cols[p]], (((1,), (1,)), ((), ())),
                              preferred_element_type=F32) * scale for p in slabs]
        lszs = [jnp.minimum(z, 0.0) - jnp.log1p(jnp.exp(-jnp.abs(z))) for z in zs]
        trs = []
        for p in slabs:
            L = jnp.where(mask, lszs[p] - zs[p], 0.0)
            L_hi = L.astype(BF16)
            L_lo = (L - L_hi.astype(F32)).astype(BF16)
            tr = _dot(jnp.concatenate([L_hi, L_lo], axis=0), MO)
            trs.append(tr[:2 * CHUNK] + tr[2 * CHUNK:])
        for p in slabs:
            c = carry_refs[p][...]
            a = jnp.where(mask, jnp.exp(lszs[p] + trs[p][:, :LANES] + c), 0.0).astype(BF16)
            a_cat = jnp.concatenate([a[:CHUNK], a[CHUNK:]], axis=1)
            vs = v_ref[pl.ds(start, CHUNK), cols[p]]
            v_cat = jnp.concatenate([jnp.where(hl, vs, jnp.zeros_like(vs)) for hl in head_lanes], axis=0)
            acc_refs[p][...] += _dot(a_cat, v_cat)
            c = c + trs[p][:, LANES:]
            carry_refs[p][...] = c
            cmax = jnp.maximum(cmax, c)
        return jj + 1, jnp.max(cmax)

    lax.while_loop(cond, body, (jnp.int32(0), jnp.float32(0.0)))
    for p in range(n_pair):
        cols = slice(p * LANES, (p + 1) * LANES)
        acc = acc_refs[p][...]
        ms = _dot((acc * acc).astype(BF16), a_ref[...])
        o_ref[:, cols] = (acc * lax.rsqrt(ms + EPS) * gb_ref[:, cols]).astype(BF16)


def attn_call(q, k, v, gb, B, T):
    N = q.shape[0]
    nq = T // CHUNK
    i = np.arange(CHUNK)
    MO = jnp.asarray(np.concatenate([(i[:, None] > i[None, :]), np.ones((CHUNK, LANES), bool)], axis=1)
                     .astype(np.float32), BF16)
    kv = pl.BlockSpec((T, D_B), lambda b, i: (b, 0))
    qo = pl.BlockSpec((CHUNK, D_B), lambda b, i: (b * nq + i, 0))
    n_pair = D_B // LANES
    return pl.pallas_call(
        _attn_kernel, grid=(B, nq),
        in_specs=[qo, kv, kv, pl.BlockSpec((CHUNK, 2 * LANES), lambda b, i: (0, 0)),
                  pl.BlockSpec((CHUNK, LANES), lambda b, i: (0, 0)), pl.BlockSpec((1, D_B), lambda b, i: (0, 0))],
        out_specs=qo,
        out_shape=jax.ShapeDtypeStruct((N, D_B), BF16),
        scratch_shapes=([pltpu.VMEM((CHUNK, LANES), F32)] * n_pair + [pltpu.VMEM((2 * CHUNK, LANES), F32)] * n_pair
                        + [pltpu.VMEM((2 * CHUNK, LANES), BF16)] * n_pair),
        compiler_params=_cparams(("parallel", "arbitrary")), name="stickbreak_attn",
    )(q, k, v, MO, _block_diag_mean(LANES, GROUP), gb)


def _out_kernel(ya_ref, yb_ref, x_ref, w_ref, gt_ref, g2_ref, sc_ref, sh_ref, xo_ref, h2_ref):
    o = _dot(ya_ref[...], w_ref[:D_A, :]) + _dot(yb_ref[...], w_ref[D_A:, :])
    x = x_ref[...] + gt_ref[0] * o
    xo_ref[...] = x
    h2_ref[...] = _rms_mod(x, g2_ref[...], sc_ref[0], sh_ref[0])


def out_call(ya, yb, x, w_out_bf, mod, g2, T, tm=512, x_first_tok=0):
    N = ya.shape[0]
    row = pl.BlockSpec((tm, D_MODEL), lambda i: (i, 0))
    x_row = pl.BlockSpec((tm, D_MODEL), lambda i: (x_first_tok // tm + i, 0))
    half = pl.BlockSpec((tm, D_A), lambda i: (i, 0))
    return pl.pallas_call(
        _out_kernel, grid=(N // tm,),
        in_specs=[half, half, x_row, pl.BlockSpec(w_out_bf.shape, lambda i: (0, 0)), _mod_spec(tm, T, 2),
                  pl.BlockSpec((1, D_MODEL), lambda i: (0, 0)), _mod_spec(tm, T, 4), _mod_spec(tm, T, 3)],
        out_specs=[row, row], out_shape=[jax.ShapeDtypeStruct((N, D_MODEL), F32)] * 2,
        compiler_params=_cparams(("parallel",)), name="out_proj",
    )(ya, yb, x, w_out_bf, mod, g2, mod, mod)


N_CAND_ROWS = -(-N_STAIR // SUBLANES) * SUBLANES


def _topk_cols(s, payload=None):
    n_rows, tb = s.shape
    row_f = lax.broadcasted_iota(jnp.int32, (n_rows, tb), 0).astype(F32)
    rank = lax.broadcasted_iota(jnp.int32, (PEER_TOPK, tb), 0)
    neg = jnp.float32(-jnp.inf)

    def body(r, carry):
        s, vals, second = carry
        m = jnp.max(s, axis=0, keepdims=True)
        pos = jnp.min(jnp.where(s == m, row_f, float(n_rows)), axis=0, keepdims=True)
        hit = row_f == pos
        out = pos if payload is None else jnp.max(jnp.where(hit, payload, -1.0), axis=0, keepdims=True)
        dst = rank == r
        return jnp.where(hit, neg, s), jnp.where(dst, m, vals), jnp.where(dst, out, second)

    zeros = jnp.zeros((PEER_TOPK, tb), F32)
    _, vals, second = lax.fori_loop(0, PEER_TOPK, body, (s, zeros, zeros))
    return vals, second


def _peerq_kernel(h2_ref, wq_ref, k1_ref, k2_ref, sel_ref, sele_ref, idx_ref, g_ref, q_scr, it_scr, gt_scr):
    h = pl.program_id(1)
    tm = h2_ref.shape[0]

    @pl.when(h == 0)
    def _():
        q_scr[...] = _dot(h2_ref[...].astype(BF16), wq_ref[...]).astype(BF16)

    nt = (((1,), (1,)), ((), ()))
    vals, keys = [], []
    for tb in range(tm // LANES):
        qh = q_scr[tb * LANES:(tb + 1) * LANES, pl.ds(pl.multiple_of(h * LANES, LANES), LANES)]
        s12 = jnp.concatenate([lax.dot_general(k1_ref[...], qh, nt, preferred_element_type=F32),
                               lax.dot_general(k2_ref[...], qh, nt, preferred_element_type=F32)], axis=1)
        v12, i12 = _topk_cols(s12)
        vals.append(jnp.concatenate([v12[:, :LANES], v12[:, LANES:]], axis=0))
        keys.append(jnp.concatenate([i12[:, :LANES], i12[:, LANES:]], axis=0))
    vals = jnp.concatenate(vals, axis=1)
    keys = jnp.concatenate(keys, axis=1)
    cand = jnp.dot(sel_ref[...], vals, precision=HIGHEST, preferred_element_type=F32)
    cand_row = lax.broadcasted_iota(jnp.int32, cand.shape, 0)
    cand = jnp.where(cand_row < N_STAIR, cand, -jnp.inf)
    ecand = jnp.dot(sele_ref[...], keys, precision=HIGHEST, preferred_element_type=F32)
    sc, ex = _topk_cols(cand, ecand)
    p = jnp.exp(sc - sc[0:1, :])
    out_rows = pl.ds(pl.multiple_of(h * PEER_TOPK, PEER_TOPK), PEER_TOPK)
    it_scr[out_rows, :] = ex
    gt_scr[out_rows, :] = p / jnp.sum(p, axis=0, keepdims=True)

    @pl.when(h == PEER_HEADS - 1)
    def _():
        idx_ref[...] = it_scr[...].T.astype(jnp.int32)
        g_ref[...] = gt_scr[...].T


def peerq_call(h2, wq_bf, k1, k2, tm=256):
    N = h2.shape[0]
    half = N_KEYS // 2
    k1p = jnp.concatenate([k1, jnp.zeros((N_KEYS, half), F32)], axis=1).astype(BF16)
    k2p = jnp.concatenate([jnp.zeros((N_KEYS, half), F32), k2], axis=1).astype(BF16)
    sel = np.zeros((N_CAND_ROWS, 2 * PEER_TOPK), np.float32)
    sele = np.zeros((N_CAND_ROWS, 2 * PEER_TOPK), np.float32)
    for c, (a, b) in enumerate(_STAIR):
        sel[c, a] = sel[c, PEER_TOPK + b] = 1.0
        sele[c, a] = float(N_KEYS)
        sele[c, PEER_TOPK + b] = 1.0
    const = lambda shape: pl.BlockSpec(shape, lambda i, h: (0, 0))
    row = pl.BlockSpec((tm, HK), lambda i, h: (i, 0))
    return pl.pallas_call(
        _peerq_kernel, grid=(N // tm, PEER_HEADS),
        in_specs=[pl.BlockSpec((tm, D_MODEL), lambda i, h: (i, 0)), const(wq_bf.shape), const((N_KEYS, LANES)),
                  const((N_KEYS, LANES)), const(sel.shape), const(sele.shape)],
        out_specs=[row, row],
        out_shape=[jax.ShapeDtypeStruct((N, HK), jnp.int32), jax.ShapeDtypeStruct((N, HK), F32)],
        scratch_shapes=[pltpu.VMEM((tm, D_MODEL), BF16), pltpu.VMEM((HK, tm), F32), pltpu.VMEM((HK, tm), F32)],
        compiler_params=_cparams(("parallel", "arbitrary")), name="peer_retrieve",
    )(h2, wq_bf, k1p, k2p, jnp.asarray(sel), jnp.asarray(sele))


SC_LANES = 16
SC_CORES = 2
SC_WORKERS = 32
ROWS_PER_GATHER = 16
N_ROW_BUF = 4
LOOKAHEAD = N_ROW_BUF - 1
TOK_BLOCK = 16
ROW_BLOCK = 8
U_UNROLL = 2
COL_BLOCK = 16
N_DCHUNK = D_MODEL // SC_LANES
N_GATHER = HK // ROWS_PER_GATHER


def _sc_gelu(x):
    y = math.sqrt(2.0 / math.pi) * (x + 0.044715 * (x * x * x))
    t = 1.0 - 2.0 / (jnp.exp(2.0 * y) + 1.0)
    return 0.5 * x * (1.0 + t)


def _peer_eval_body(per_w, h_hbm, idx_hbm, g_hbm, u_hbm, v_hbm, o_hbm,
                    idx_blk, g_blk, h_blk, out_blk, w_v, *bufs):
    wid = lax.axis_index("s") * SC_CORES + lax.axis_index("c")
    rows = bufs[:N_ROW_BUF]
    sems = bufs[N_ROW_BUF:]
    lane = lax.iota(jnp.int32, SC_LANES)
    zeros = jnp.zeros((SC_LANES,), F32)

    def gather(table, i, j, buf):
        return pltpu.make_async_copy(table.at[idx_blk.at[i, j]], rows[buf], sems[buf])

    def u_compute(i, n, rbuf):
        for half in range(ROWS_PER_GATHER // SC_LANES):
            actv = zeros
            for rg in range(SC_LANES // ROW_BLOCK):
                r0 = half * SC_LANES + rg * ROW_BLOCK

                def c_body(cb, accs):
                    accs = list(accs)
                    for cc in range(U_UNROLL):
                        sl = pl.ds(pl.multiple_of((cb * U_UNROLL + cc) * SC_LANES, SC_LANES), SC_LANES)
                        hc = h_blk[i, sl]
                        for r in range(ROW_BLOCK):
                            accs[r] = accs[r] + rbuf[r0 + r, sl] * hc
                    return tuple(accs)

                accs = lax.fori_loop(0, N_DCHUNK // U_UNROLL, c_body, (zeros,) * ROW_BLOCK)
                for r in range(ROW_BLOCK):
                    actv = jnp.where(lane == rg * ROW_BLOCK + r, jnp.sum(accs[r]), actv)
            sl = pl.ds(n * ROWS_PER_GATHER + half * SC_LANES, SC_LANES)
            w_v[sl] = g_blk[i, sl] * _sc_gelu(actv)

    def v_compute(i, m, rbuf):
        for cb in range(N_DCHUNK // COL_BLOCK):
            def r_body(r, accs):
                wk = plsc.load_gather(w_v, [jnp.full((SC_LANES,), m * ROWS_PER_GATHER, jnp.int32) + r])
                return tuple(accs[cc] + wk * rbuf[r, pl.ds((cb * COL_BLOCK + cc) * SC_LANES, SC_LANES)]
                             for cc in range(COL_BLOCK))

            accs = lax.fori_loop(0, ROWS_PER_GATHER, r_body, (zeros,) * COL_BLOCK)
            for cc in range(COL_BLOCK):
                sl = pl.ds((cb * COL_BLOCK + cc) * SC_LANES, SC_LANES)
                if m == 0:
                    out_blk[i, sl] = accs[cc]
                else:
                    plsc.addupdate(out_blk.at[i, sl], accs[cc])

    @pl.loop(0, per_w // TOK_BLOCK)
    def _(bi):
        t0 = wid * per_w + bi * TOK_BLOCK
        pltpu.sync_copy(idx_hbm.at[pl.ds(t0, TOK_BLOCK)], idx_blk)
        pltpu.sync_copy(g_hbm.at[pl.ds(t0, TOK_BLOCK)], g_blk)
        pltpu.sync_copy(h_hbm.at[pl.ds(t0, TOK_BLOCK)], h_blk)
        def task_gather(i, n):
            table = u_hbm if n < N_GATHER else v_hbm
            return gather(table, i, n % N_GATHER, n % N_ROW_BUF)

        for n in range(LOOKAHEAD):
            task_gather(0, n).start()

        @pl.loop(0, TOK_BLOCK)
        def _(i):
            for n in range(2 * N_GATHER):
                ahead = n + LOOKAHEAD
                if ahead < 2 * N_GATHER:
                    task_gather(i, ahead).start()
                else:
                    @pl.when(i + 1 < TOK_BLOCK)
                    def _():
                        task_gather(i + 1, ahead - 2 * N_GATHER).start()
                task_gather(i, n).wait()
                if n < N_GATHER:
                    u_compute(i, n, rows[n % N_ROW_BUF])
                else:
                    v_compute(i, n - N_GATHER, rows[n % N_ROW_BUF])

        pltpu.sync_copy(out_blk, o_hbm.at[pl.ds(t0, TOK_BLOCK)])


def peer_eval_call(h2, idx, g, u_tab, v_tab, n_tok=None, table_row0=0):
    N = h2.shape[0] if n_tok is None else n_tok
    per_w = N // SC_WORKERS
    assert N % (SC_WORKERS * TOK_BLOCK) == 0
    assert (2 * N_GATHER) % N_ROW_BUF == 0 and LOOKAHEAD < N_ROW_BUF
    mesh = plsc.VectorSubcoreMesh(core_axis_name="c", subcore_axis_name="s",
                                  num_cores=SC_CORES, num_subcores=SC_WORKERS // SC_CORES)
    return pl.kernel(
        functools.partial(_peer_eval_body, per_w),
        out_type=jax.ShapeDtypeStruct((N, D_MODEL), F32),
        mesh=mesh,
        scratch_types=[
            pltpu.VMEM((TOK_BLOCK, N_GATHER, ROWS_PER_GATHER), jnp.int32),
            pltpu.VMEM((TOK_BLOCK, HK), F32),
            pltpu.VMEM((TOK_BLOCK, D_MODEL), F32),
            pltpu.VMEM((TOK_BLOCK, D_MODEL), F32),
            pltpu.VMEM((HK,), F32),
        ] + [pltpu.VMEM((ROWS_PER_GATHER, D_MODEL), F32)] * N_ROW_BUF + [pltpu.SemaphoreType.DMA] * N_ROW_BUF,
        compiler_params=pltpu.CompilerParams(needs_layout_passes=False),
        name="peer_eval",
    )(h2, (idx + table_row0).reshape(h2.shape[0], N_GATHER, ROWS_PER_GATHER), g, u_tab, v_tab)


N_EXPERTS = N_KEYS * N_KEYS
E_HALF = N_EXPERTS // 2
TC_TOK_TILE = 16
TC_EVAL_SHARE = (0.1875, 0.25)
COMPILER_SCRATCH_BYTES = 2 * 1024 * 1024


def _tc_eval_kernel(table_row0, idx_s, hi_ref, g_ref, h_ref, u_hbm, v_hbm, o_ref,
                    tab, actw, outp, rsum, wb_all, stage, sem):
    p = pl.program_id(0)
    i = pl.program_id(1)
    for ph, (src, half) in enumerate(((u_hbm, 0), (u_hbm, 1), (v_hbm, 0), (v_hbm, 1))):
        @pl.when(jnp.logical_and(p == ph, i == 0))
        def _():
            cp = pltpu.make_async_copy(src.at[pl.ds(table_row0 + half * E_HALF, E_HALF)], tab, sem)
            cp.start()
            cp.wait()

    lane = lax.broadcasted_iota(jnp.int32, (HK, LANES), 1)
    row = lax.broadcasted_iota(jnp.int32, (HK, LANES), 0)
    in_half = hi_ref[0] == (p % 2)

    @pl.when(p < 2)
    def _():
        acts = jnp.zeros((HK, LANES), F32)
        for t in range(TC_TOK_TILE):
            h = h_ref[t]
            for k in range(HK):
                rsum[t, pl.ds(k, 1), :] = jnp.sum(tab[idx_s[t, k]] * h, axis=0, keepdims=True)
            acts = jnp.where(lane == t, jnp.sum(rsum[t], axis=1, keepdims=True), acts)
        part = jnp.where(in_half, acts, 0.0)

        @pl.when(p == 0)
        def _():
            actw[i] = part

        @pl.when(p == 1)
        def _():
            actw[i] = g_ref[0] * jax.nn.gelu(actw[i] + part)

    @pl.when(p >= 2)
    def _():
        wm = jnp.where(in_half, actw[i], 0.0)

        for t in range(TC_TOK_TILE):
            wb_all[t] = jnp.broadcast_to(wm[:, t:t + 1], (HK, LANES))

        for t in range(TC_TOK_TILE):
            accs = [jnp.zeros((SUBLANES, LANES), F32)] * 4
            for k in range(HK):
                w = jnp.broadcast_to(wb_all[t, pl.ds(k, 1), :], (SUBLANES, LANES))
                accs[k % 4] = accs[k % 4] + w * tab[idx_s[t, k]]
            stage[t] = (accs[0] + accs[1]) + (accs[2] + accs[3])
        tile_toks = pl.ds(pl.multiple_of(i * TC_TOK_TILE, TC_TOK_TILE), TC_TOK_TILE)

        @pl.when(p == 2)
        def _():
            outp[tile_toks] = stage[...]

        @pl.when(p == 3)
        def _():
            o_ref[...] = outp[tile_toks] + stage[...]


def peer_eval_tc_call(h2, idx, g, u_tab, v_tab, first_tok=0, table_row0=0):
    n_all = h2.shape[0]
    nt = n_all - first_tok
    n_tiles = nt // TC_TOK_TILE
    first_tile = first_tok // TC_TOK_TILE
    n_rows = u_tab.shape[0]
    assert nt % TC_TOK_TILE == 0 and first_tok % TC_TOK_TILE == 0 and table_row0 + N_EXPERTS <= n_rows
    idx, g = idx[first_tok:], g[first_tok:]

    def pairs_on_sublanes(a, fill):
        a = a.reshape(n_tiles, TC_TOK_TILE, HK).transpose(0, 2, 1)
        return jnp.pad(a, ((0, 0), (0, 0), (0, LANES - TC_TOK_TILE)), constant_values=fill)

    tile3 = lambda: pl.BlockSpec((1, HK, LANES), lambda p, i: (i, 0, 0))
    scratch_shapes = [(E_HALF, SUBLANES, LANES), (n_tiles, HK, LANES), (nt, SUBLANES, LANES),
                      (TC_TOK_TILE, HK, LANES), (TC_TOK_TILE, HK, LANES), (TC_TOK_TILE, SUBLANES, LANES)]
    block_shapes = [(HK, LANES), (HK, LANES), (TC_TOK_TILE, SUBLANES, LANES), (TC_TOK_TILE, SUBLANES, LANES)]
    f32_bytes = 4
    vmem_bytes = (sum(int(np.prod(s)) for s in scratch_shapes) + 2 * sum(int(np.prod(s)) for s in block_shapes)
                  ) * f32_bytes + COMPILER_SCRATCH_BYTES
    assert vmem_bytes <= V7X_VMEM_BYTES
    out = pl.pallas_call(
        functools.partial(_tc_eval_kernel, table_row0), grid=(4, n_tiles),
        in_specs=[pl.BlockSpec((TC_TOK_TILE, HK), lambda p, i: (i, 0), memory_space=pltpu.SMEM),
                  tile3(), tile3(),
                  pl.BlockSpec((TC_TOK_TILE, SUBLANES, LANES), lambda p, i: (first_tile + i, 0, 0)),
                  pl.BlockSpec(memory_space=pl.ANY), pl.BlockSpec(memory_space=pl.ANY)],
        out_specs=pl.BlockSpec((TC_TOK_TILE, SUBLANES, LANES), lambda p, i: (jnp.where(p == 3, i, 0), 0, 0)),
        out_shape=jax.ShapeDtypeStruct((nt, SUBLANES, LANES), F32),
        scratch_shapes=[pltpu.VMEM(s, F32) for s in scratch_shapes] + [pltpu.SemaphoreType.DMA],
        compiler_params=pltpu.CompilerParams(dimension_semantics=("arbitrary", "arbitrary"),
                                             vmem_limit_bytes=vmem_bytes),
        name="peer_eval_tc",
    )(idx & (E_HALF - 1), pairs_on_sublanes(idx // E_HALF, 2), pairs_on_sublanes(g, 0.0),
      h2.reshape(n_all, SUBLANES, LANES),
      u_tab.reshape(n_rows, SUBLANES, LANES), v_tab.reshape(n_rows, SUBLANES, LANES))
    return out.reshape(nt, D_MODEL)


def _final_kernel(x_ref, y_ref, gt_ref, g_ref, o_ref):
    x = x_ref[...] + gt_ref[0] * y_ref[...]
    ms = jnp.mean(x * x, axis=-1, keepdims=True)
    o_ref[...] = x * lax.rsqrt(ms + EPS) * g_ref[...]


def final_call(x, y, mod, gf, T, tm=512):
    N = x.shape[0]
    row = pl.BlockSpec((tm, D_MODEL), lambda i: (i, 0))
    return pl.pallas_call(
        _final_kernel, grid=(N // tm,),
        in_specs=[row, row, _mod_spec(tm, T, 5), pl.BlockSpec((1, D_MODEL), lambda i: (0, 0))],
        out_specs=row, out_shape=jax.ShapeDtypeStruct((N, D_MODEL), F32),
        compiler_params=_cparams(("parallel",)), name="final_norm",
    )(x, y, mod, gf)


def kernel(x, c, ada_w, ada_b, norm1_g, norm2_g, w_in, sgu_w, sgu_b, out_norm_a, out_norm_b, w_out,
           peer_wq, peer_k1, peer_k2, peer_u, peer_v, final_g):
    B, T, Dm = x.shape
    L = ada_w.shape[0]
    N = B * T
    mods = ada_mod(c, ada_w, ada_b).reshape(L, B * 6, 1, Dm)
    assert sum(GROUP_BATCHES) == B and len(TC_EVAL_SHARE) == L
    n_groups = len(GROUP_BATCHES)
    first_batch = [sum(GROUP_BATCHES[:g]) for g in range(n_groups)]
    gmods = [[mods[l, first_batch[g] * 6:(first_batch[g] + GROUP_BATCHES[g]) * 6] for g in range(n_groups)]
             for l in range(L)]
    xs = [x.reshape(N, Dm)] * n_groups
    ys = [None] * n_groups
    n_exp = peer_u.shape[1]
    u_all = peer_u.reshape(L * n_exp, Dm)
    v_all = peer_v.reshape(L * n_exp, Dm)
    sc_step = SC_WORKERS * TOK_BLOCK
    for l in range(L):
        w_in_bf, w_out_bf, wq_bf = w_in[l].astype(BF16), w_out[l].astype(BF16), peer_wq[l].astype(BF16)
        bias_full = jnp.repeat(sgu_b[l].T, GROUP, axis=1)
        for g in range(n_groups):
            Bg = GROUP_BATCHES[g]
            Ng = Bg * T
            mod = gmods[l][g]
            x_first = first_batch[g] * T if l == 0 else 0
            outs = proj_call(xs[g], ys[g], gmods[l - 1][g] if l else None, mod, norm1_g[l].reshape(1, Dm), w_in_bf, T,
                             n_tok=Ng, x_first_tok=x_first)
            if l:
                xs[g], outs = outs[0], outs[1:]
            ua, va, q, k, v = outs
            ya = gmlp_call(ua, va, sgu_w[l], bias_full, out_norm_a[l].reshape(1, D_A))
            yb = attn_call(q, k, v, out_norm_b[l].reshape(1, D_B), Bg, T)
            xs[g], h2 = out_call(ya, yb, xs[g], w_out_bf, mod, norm2_g[l].reshape(1, Dm), T, x_first_tok=x_first)
            idx, gate = peerq_call(h2, wq_bf, peer_k1[l], peer_k2[l])
            n_sc = int(Ng * (1.0 - TC_EVAL_SHARE[l])) // sc_step * sc_step
            y_sc = peer_eval_call(h2, idx, gate, u_all, v_all, n_tok=n_sc, table_row0=l * n_exp)
            y_tc = peer_eval_tc_call(h2, idx, gate, u_all, v_all, first_tok=n_sc, table_row0=l * n_exp)
            ys[g] = jnp.concatenate([y_sc, y_tc], axis=0)
    outs = [final_call(xs[g], ys[g], gmods[L - 1][g], final_g.reshape(1, Dm), T) for g in range(n_groups)]
    return jnp.concatenate(outs, axis=0).reshape(B, T, Dm)
```

```python
import functools
import math

import numpy as np
import jax
import jax.numpy as jnp
from jax import lax
from jax.experimental import pallas as pl
from jax.experimental.pallas import tpu as pltpu
from jax.experimental.pallas import tpu_sc as plsc

F32 = jnp.float32
BF16 = jnp.bfloat16
HIGHEST = lax.Precision.HIGHEST

D_MODEL = 1024
D_A = 512
D_B = 512
GROUP = 64
CHUNK = 128
N_KEYS = 128
PEER_HEADS = 8
PEER_TOPK = 16
HK = PEER_HEADS * PEER_TOPK
EPS = 1e-6
LANES = 128
SUBLANES = 8
V7X_VMEM_BYTES = 64 * 1024 * 1024
VMEM_LIMIT = V7X_VMEM_BYTES * 3 // 4
GROUP_BATCHES = (5, 5, 4, 2)

_STAIR = [(a, b) for a in range(PEER_TOPK) for b in range(PEER_TOPK) if (a + 1) * (b + 1) <= PEER_TOPK]
N_STAIR = len(_STAIR)


def _cparams(sem):
    return pltpu.CompilerParams(dimension_semantics=sem, vmem_limit_bytes=VMEM_LIMIT)


def _dot(a, b):
    return jnp.dot(a, b, preferred_element_type=F32)


def _block_diag_mean(n, group):
    i = np.arange(n)
    return jnp.asarray((i[:, None] // group == i[None, :] // group).astype(np.float32) / group, BF16)


def _ada_kernel(c_ref, w_ref, b_ref, o_ref):
    c = c_ref[...]
    ca = c * jax.nn.sigmoid(c)
    o_ref[0] = jnp.dot(ca, w_ref[0], precision=HIGHEST, preferred_element_type=F32) + b_ref[0]


def ada_mod(c, ada_w, ada_b):
    L, Dm, E = ada_w.shape
    Bc = c.shape[0]
    tn = 1536
    return pl.pallas_call(
        _ada_kernel,
        grid=(L, E // tn),
        in_specs=[pl.BlockSpec((Bc, Dm), lambda l, j: (0, 0)),
                  pl.BlockSpec((1, Dm, tn), lambda l, j: (l, 0, j)),
                  pl.BlockSpec((1, 1, tn), lambda l, j: (l, 0, j))],
        out_specs=pl.BlockSpec((1, Bc, tn), lambda l, j: (l, 0, j)),
        out_shape=jax.ShapeDtypeStruct((L, Bc, E), F32),
        compiler_params=_cparams(("parallel", "parallel")),
        name="ada_mod",
    )(c, ada_w, ada_b.reshape(L, 1, E))


def _mod_spec(tm, T, j):
    return pl.BlockSpec((1, 1, D_MODEL), lambda i: (((i * tm) // T) * 6 + j, 0, 0))


def _rms_mod(x, g, sc, sh):
    ms = jnp.mean(x * x, axis=-1, keepdims=True)
    return (x * lax.rsqrt(ms + EPS) * g) * (1.0 + sc) + sh


def _proj_kernel(has_y, *refs):
    if has_y:
        x_ref, y_ref, gt_ref, g_ref, sc_ref, sh_ref, w_ref, xo_ref, ua_ref, va_ref, q_ref, k_ref, v_ref = refs
        x = x_ref[...] + gt_ref[0] * y_ref[...]
        xo_ref[...] = x
    else:
        x_ref, g_ref, sc_ref, sh_ref, w_ref, ua_ref, va_ref, q_ref, k_ref, v_ref = refs
        x = x_ref[...]
    h = _rms_mod(x, g_ref[...], sc_ref[0], sh_ref[0]).astype(BF16)
    outs = (ua_ref, va_ref, q_ref, k_ref, v_ref)
    for j, o_ref in enumerate(outs):
        p = _dot(h, w_ref[:, j * D_A:(j + 1) * D_A])
        if j < 2:
            p = jax.nn.gelu(p)
        o_ref[...] = p.astype(o_ref.dtype)


def proj_call(x, y, mod_prev, mod, g1, w_in_bf, T, tm=512, n_tok=None, x_first_tok=0):
    N = x.shape[0] if n_tok is None else n_tok
    has_y = y is not None
    row = pl.BlockSpec((tm, D_MODEL), lambda i: (i, 0))
    half = pl.BlockSpec((tm, D_A), lambda i: (i, 0))
    in_specs = [pl.BlockSpec((tm, D_MODEL), lambda i: (x_first_tok // tm + i, 0))]
    args = [x]
    if has_y:
        in_specs += [row, _mod_spec(tm, T, 5)]
        args += [y, mod_prev]
    in_specs += [pl.BlockSpec((1, D_MODEL), lambda i: (0, 0)), _mod_spec(tm, T, 1), _mod_spec(tm, T, 0),
                 pl.BlockSpec(w_in_bf.shape, lambda i: (0, 0))]
    args += [g1, mod, mod, w_in_bf]
    out_specs = [half] * 5
    out_shape = [jax.ShapeDtypeStruct((N, D_A), F32)] * 2 + [jax.ShapeDtypeStruct((N, D_A), BF16)] * 3
    if has_y:
        out_specs = [row] + out_specs
        out_shape = [jax.ShapeDtypeStruct((N, D_MODEL), F32)] + out_shape
    return pl.pallas_call(
        functools.partial(_proj_kernel, has_y),
        grid=(N // tm,), in_specs=in_specs, out_specs=out_specs, out_shape=out_shape,
        compiler_params=_cparams(("parallel",)), name="proj",
    )(*args)


def _gmlp_kernel(ua_ref, va_ref, w_ref, bias_ref, a_ref, ga_ref, o_ref):
    A = a_ref[...]
    row = lax.broadcasted_iota(jnp.int32, (CHUNK, CHUNK), 0)
    col = lax.broadcasted_iota(jnp.int32, (CHUNK, CHUNK), 1)
    causal = row >= col
    first_group = col < GROUP
    ws = [jnp.where(causal, w_ref[g], 0.0).astype(BF16) for g in range(D_A // GROUP)]
    bias = bias_ref[...]
    ga = ga_ref[...]
    for c in range(ua_ref.shape[0] // CHUNK):
        rows = slice(c * CHUNK, (c + 1) * CHUNK)
        v = va_ref[rows, :]
        v_hi = v.astype(BF16)
        v_lo = (v - v_hi.astype(F32)).astype(BF16)
        d = v - (_dot(v_hi, A) + _dot(v_lo, A))
        var = _dot((d * d).astype(BF16), A)
        vn = (d * lax.rsqrt(var + EPS)).astype(BF16)
        parts = []
        for p in range(D_A // LANES):
            vp = vn[:, p * LANES:(p + 1) * LANES]
            parts.append(jnp.where(first_group, _dot(ws[2 * p], vp), _dot(ws[2 * p + 1], vp)))
        s = jnp.concatenate(parts, axis=1) + bias
        y = ua_ref[rows, :] * s
        ms = _dot((y * y).astype(BF16), A)
        o_ref[rows, :] = (y * lax.rsqrt(ms + EPS) * ga).astype(BF16)


def gmlp_call(ua, va, sgu_w, bias_full, ga, tm=512):
    N = ua.shape[0]
    half = pl.BlockSpec((tm, D_A), lambda i: (i, 0))
    return pl.pallas_call(
        _gmlp_kernel, grid=(N // tm,),
        in_specs=[half, half,
                  pl.BlockSpec(sgu_w.shape, lambda i: (0, 0, 0)),
                  pl.BlockSpec((CHUNK, D_A), lambda i: (0, 0)),
                  pl.BlockSpec((D_A, D_A), lambda i: (0, 0)),
                  pl.BlockSpec((1, D_A), lambda i: (0, 0))],
        out_specs=half, out_shape=jax.ShapeDtypeStruct((N, D_A), BF16),
        compiler_params=_cparams(("parallel",)), name="gmlp",
    )(ua, va, sgu_w, bias_full, _block_diag_mean(D_A, GROUP), ga)


EXP_UNDERFLOW = -104.0


def _attn_kernel(q_ref, k_ref, v_ref, mo_ref, a_ref, gb_ref, o_ref, *scratch):
    i = pl.program_id(1)
    n_pair = D_B // LANES
    lane = lax.broadcasted_iota(jnp.int32, (CHUNK, LANES), 1)
    row = lax.broadcasted_iota(jnp.int32, (CHUNK, LANES), 0)
    head_lanes = (lane < GROUP, lane >= GROUP)
    MO = mo_ref[...]
    scale = 1.0 / math.sqrt(GROUP)
    acc_refs, carry_refs, qs_refs = scratch[:n_pair], scratch[n_pair:2 * n_pair], scratch[2 * n_pair:]
    for p in range(n_pair):
        acc_refs[p][...] = jnp.zeros_like(acc_refs[p])
        carry_refs[p][...] = jnp.zeros_like(carry_refs[p])
        q = q_ref[:, p * LANES:(p + 1) * LANES]
        qs_refs[p][...] = jnp.concatenate([jnp.where(hl, q, jnp.zeros_like(q)) for hl in head_lanes], axis=0)

    def cond(state):
        jj, cmax = state
        return jnp.logical_and(jj <= i, cmax > EXP_UNDERFLOW)

    def body(state):
        jj, _ = state
        j = i - jj
        start = pl.multiple_of(j * CHUNK, CHUNK)
        mask1 = (j * CHUNK + lane) < (i * CHUNK + row)
        mask = jnp.concatenate([mask1, mask1], axis=0)
        cmax = jnp.full((2 * CHUNK, LANES), -jnp.inf, F32)
        slabs = range(n_pair)
        cols = [slice(p * LANES, (p + 1) * LANES) for p in slabs]
        zs = [lax.dot_general(qs_refs[p][...], k_ref[pl.ds(start, CHUNK), cols[p]], (((1,), (1,)), ((), ())),
                              preferred_element_type=F32) * scale for p in slabs]
        lszs = [jnp.minimum(z, 0.0) - jnp.log1p(jnp.exp(-jnp.abs(z))) for z in zs]
        trs = []
        for p in slabs:
            L = jnp.where(mask, lszs[p] - zs[p], 0.0)
            L_hi = L.astype(BF16)
            L_lo = (L - L_hi.astype(F32)).astype(BF16)
            tr = _dot(jnp.concatenate([L_hi, L_lo], axis=0), MO)
            trs.append(tr[:2 * CHUNK] + tr[2 * CHUNK:])
        for p in slabs:
            c = carry_refs[p][...]
            a = jnp.where(mask, jnp.exp(lszs[p] + trs[p][:, :LANES] + c), 0.0).astype(BF16)
            a_cat = jnp.concatenate([a[:CHUNK], a[CHUNK:]], axis=1)
            vs = v_ref[pl.ds(start, CHUNK), cols[p]]
            v_cat = jnp.concatenate([jnp.where(hl, vs, jnp.zeros_like(vs)) for hl in head_lanes], axis=0)
            acc_refs[p][...] += _dot(a_cat, v_cat)
            c = c + trs[p][:, LANES:]
            carry_refs[p][...] = c
            cmax = jnp.maximum(cmax, c)
        return jj + 1, jnp.max(cmax)

    lax.while_loop(cond, body, (jnp.int32(0), jnp.float32(0.0)))
    for p in range(n_pair):
        cols = slice(p * LANES, (p + 1) * LANES)
        acc = acc_refs[p][...]
        ms = _dot((acc * acc).astype(BF16), a_ref[...])
        o_ref[:, cols] = (acc * lax.rsqrt(ms + EPS) * gb_ref[:, cols]).astype(BF16)


def attn_call(q, k, v, gb, B, T):
    N = q.shape[0]
    nq = T // CHUNK
    i = np.arange(CHUNK)
    MO = jnp.asarray(np.concatenate([(i[:, None] > i[None, :]), np.ones((CHUNK, LANES), bool)], axis=1)
                     .astype(np.float32), BF16)
    kv = pl.BlockSpec((T, D_B), lambda b, i: (b, 0))
    qo = pl.BlockSpec((CHUNK, D_B), lambda b, i: (b * nq + i, 0))
    n_pair = D_B // LANES
    return pl.pallas_call(
        _attn_kernel, grid=(B, nq),
        in_specs=[qo, kv, kv, pl.BlockSpec((CHUNK, 2 * LANES), lambda b, i: (0, 0)),
                  pl.BlockSpec((CHUNK, LANES), lambda b, i: (0, 0)), pl.BlockSpec((1, D_B), lambda b, i: (0, 0))],
        out_specs=qo,
        out_shape=jax.ShapeDtypeStruct((N, D_B), BF16),
        scratch_shapes=([pltpu.VMEM((CHUNK, LANES), F32)] * n_pair + [pltpu.VMEM((2 * CHUNK, LANES), F32)] * n_pair
                        + [pltpu.VMEM((2 * CHUNK, LANES), BF16)] * n_pair),
        compiler_params=_cparams(("parallel", "arbitrary")), name="stickbreak_attn",
    )(q, k, v, MO, _block_diag_mean(LANES, GROUP), gb)


def _out_kernel(ya_ref, yb_ref, x_ref, w_ref, gt_ref, g2_ref, sc_ref, sh_ref, xo_ref, h2_ref):
    o = _dot(ya_ref[...], w_ref[:D_A, :]) + _dot(yb_ref[...], w_ref[D_A:, :])
    x = x_ref[...] + gt_ref[0] * o
    xo_ref[...] = x
    h2_ref[...] = _rms_mod(x, g2_ref[...], sc_ref[0], sh_ref[0])


def out_call(ya, yb, x, w_out_bf, mod, g2, T, tm=512, x_first_tok=0):
    N = ya.shape[0]
    row = pl.BlockSpec((tm, D_MODEL), lambda i: (i, 0))
    x_row = pl.BlockSpec((tm, D_MODEL), lambda i: (x_first_tok // tm + i, 0))
    half = pl.BlockSpec((tm, D_A), lambda i: (i, 0))
    return pl.pallas_call(
        _out_kernel, grid=(N // tm,),
        in_specs=[half, half, x_row, pl.BlockSpec(w_out_bf.shape, lambda i: (0, 0)), _mod_spec(tm, T, 2),
                  pl.BlockSpec((1, D_MODEL), lambda i: (0, 0)), _mod_spec(tm, T, 4), _mod_spec(tm, T, 3)],
        out_specs=[row, row], out_shape=[jax.ShapeDtypeStruct((N, D_MODEL), F32)] * 2,
        compiler_params=_cparams(("parallel",)), name="out_proj",
    )(ya, yb, x, w_out_bf, mod, g2, mod, mod)


N_CAND_ROWS = -(-N_STAIR // SUBLANES) * SUBLANES


def _topk_cols(s, payload=None):
    n_rows, tb = s.shape
    row_f = lax.broadcasted_iota(jnp.int32, (n_rows, tb), 0).astype(F32)
    rank = lax.broadcasted_iota(jnp.int32, (PEER_TOPK, tb), 0)
    neg = jnp.float32(-jnp.inf)

    def body(r, carry):
        s, vals, second = carry
        m = jnp.max(s, axis=0, keepdims=True)
        pos = jnp.min(jnp.where(s == m, row_f, float(n_rows)), axis=0, keepdims=True)
        hit = row_f == pos
        out = pos if payload is None else jnp.max(jnp.where(hit, payload, -1.0), axis=0, keepdims=True)
        dst = rank == r
        return jnp.where(hit, neg, s), jnp.where(dst, m, vals), jnp.where(dst, out, second)

    zeros = jnp.zeros((PEER_TOPK, tb), F32)
    _, vals, second = lax.fori_loop(0, PEER_TOPK, body, (s, zeros, zeros))
    return vals, second


def _peerq_kernel(h2_ref, wq_ref, k1_ref, k2_ref, sel_ref, sele_ref, idx_ref, g_ref, q_scr, it_scr, gt_scr):
    h = pl.program_id(1)
    tm = h2_ref.shape[0]

    @pl.when(h == 0)
    def _():
        q_scr[...] = _dot(h2_ref[...].astype(BF16), wq_ref[...]).astype(BF16)

    nt = (((1,), (1,)), ((), ()))
    vals, keys = [], []
    for tb in range(tm // LANES):
        qh = q_scr[tb * LANES:(tb + 1) * LANES, pl.ds(pl.multiple_of(h * LANES, LANES), LANES)]
        s12 = jnp.concatenate([lax.dot_general(k1_ref[...], qh, nt, preferred_element_type=F32),
                               lax.dot_general(k2_ref[...], qh, nt, preferred_element_type=F32)], axis=1)
        v12, i12 = _topk_cols(s12)
        vals.append(jnp.concatenate([v12[:, :LANES], v12[:, LANES:]], axis=0))
        keys.append(jnp.concatenate([i12[:, :LANES], i12[:, LANES:]], axis=0))
    vals = jnp.concatenate(vals, axis=1)
    keys = jnp.concatenate(keys, axis=1)
    cand = jnp.dot(sel_ref[...], vals, precision=HIGHEST, preferred_element_type=F32)
    cand_row = lax.broadcasted_iota(jnp.int32, cand.shape, 0)
    cand = jnp.where(cand_row < N_STAIR, cand, -jnp.inf)
    ecand = jnp.dot(sele_ref[...], keys, precision=HIGHEST, preferred_element_type=F32)
    sc, ex = _topk_cols(cand, ecand)
    p = jnp.exp(sc - sc[0:1, :])
    out_rows = pl.ds(pl.multiple_of(h * PEER_TOPK, PEER_TOPK), PEER_TOPK)
    it_scr[out_rows, :] = ex
    gt_scr[out_rows, :] = p / jnp.sum(p, axis=0, keepdims=True)

    @pl.when(h == PEER_HEADS - 1)
    def _():
        idx_ref[...] = it_scr[...].T.astype(jnp.int32)
        g_ref[...] = gt_scr[...].T


def peerq_call(h2, wq_bf, k1, k2, tm=256):
    N = h2.shape[0]
    half = N_KEYS // 2
    k1p = jnp.concatenate([k1, jnp.zeros((N_KEYS, half), F32)], axis=1).astype(BF16)
    k2p = jnp.concatenate([jnp.zeros((N_KEYS, half), F32), k2], axis=1).astype(BF16)
    sel = np.zeros((N_CAND_ROWS, 2 * PEER_TOPK), np.float32)
    sele = np.zeros((N_CAND_ROWS, 2 * PEER_TOPK), np.float32)
    for c, (a, b) in enumerate(_STAIR):
        sel[c, a] = sel[c, PEER_TOPK + b] = 1.0
        sele[c, a] = float(N_KEYS)
        sele[c, PEER_TOPK + b] = 1.0
    const = lambda shape: pl.BlockSpec(shape, lambda i, h: (0, 0))
    row = pl.BlockSpec((tm, HK), lambda i, h: (i, 0))
    return pl.pallas_call(
        _peerq_kernel, grid=(N // tm, PEER_HEADS),
        in_specs=[pl.BlockSpec((tm, D_MODEL), lambda i, h: (i, 0)), const(wq_bf.shape), const((N_KEYS, LANES)),
                  const((N_KEYS, LANES)), const(sel.shape), const(sele.shape)],
        out_specs=[row, row],
        out_shape=[jax.ShapeDtypeStruct((N, HK), jnp.int32), jax.ShapeDtypeStruct((N, HK), F32)],
        scratch_shapes=[pltpu.VMEM((tm, D_MODEL), BF16), pltpu.VMEM((HK, tm), F32), pltpu.VMEM((HK, tm), F32)],
        compiler_params=_cparams(("parallel", "arbitrary")), name="peer_retrieve",
    )(h2, wq_bf, k1p, k2p, jnp.asarray(sel), jnp.asarray(sele))


SC_LANES = 16
SC_CORES = 2
SC_WORKERS = 32
ROWS_PER_GATHER = 16
N_ROW_BUF = 4
LOOKAHEAD = N_ROW_BUF - 1
TOK_BLOCK = 16
ROW_BLOCK = 8
U_UNROLL = 2
COL_BLOCK = 16
N_DCHUNK = D_MODEL // SC_LANES
N_GATHER = HK // ROWS_PER_GATHER


def _sc_gelu(x):
    y = math.sqrt(2.0 / math.pi) * (x + 0.044715 * (x * x * x))
    t = 1.0 - 2.0 / (jnp.exp(2.0 * y) + 1.0)
    return 0.5 * x * (1.0 + t)


def _peer_eval_body(per_w, h_hbm, idx_hbm, g_hbm, u_hbm, v_hbm, o_hbm,
                    idx_blk, g_blk, h_blk, out_blk, w_v, *bufs):
    wid = lax.axis_index("s") * SC_CORES + lax.axis_index("c")
    rows = bufs[:N_ROW_BUF]
    sems = bufs[N_ROW_BUF:]
    lane = lax.iota(jnp.int32, SC_LANES)
    zeros = jnp.zeros((SC_LANES,), F32)

    def gather(table, i, j, buf):
        return pltpu.make_async_copy(table.at[idx_blk.at[i, j]], rows[buf], sems[buf])

    def u_compute(i, n, rbuf):
        for half in range(ROWS_PER_GATHER // SC_LANES):
            actv = zeros
            for rg in range(SC_LANES // ROW_BLOCK):
                r0 = half * SC_LANES + rg * ROW_BLOCK

                def c_body(cb, accs):
                    accs = list(accs)
                    for cc in range(U_UNROLL):
                        sl = pl.ds(pl.multiple_of((cb * U_UNROLL + cc) * SC_LANES, SC_LANES), SC_LANES)
                        hc = h_blk[i, sl]
                        for r in range(ROW_BLOCK):
                            accs[r] = accs[r] + rbuf[r0 + r, sl] * hc
                    return tuple(accs)

                accs = lax.fori_loop(0, N_DCHUNK // U_UNROLL, c_body, (zeros,) * ROW_BLOCK)
                for r in range(ROW_BLOCK):
                    actv = jnp.where(lane == rg * ROW_BLOCK + r, jnp.sum(accs[r]), actv)
            sl = pl.ds(n * ROWS_PER_GATHER + half * SC_LANES, SC_LANES)
            w_v[sl] = g_blk[i, sl] * _sc_gelu(actv)

    def v_compute(i, m, rbuf):
        for cb in range(N_DCHUNK // COL_BLOCK):
            def r_body(r, accs):
                wk = plsc.load_gather(w_v, [jnp.full((SC_LANES,), m * ROWS_PER_GATHER, jnp.int32) + r])
                return tuple(accs[cc] + wk * rbuf[r, pl.ds((cb * COL_BLOCK + cc) * SC_LANES, SC_LANES)]
                             for cc in range(COL_BLOCK))

            accs = lax.fori_loop(0, ROWS_PER_GATHER, r_body, (zeros,) * COL_BLOCK)
            for cc in range(COL_BLOCK):
                sl = pl.ds((cb * COL_BLOCK + cc) * SC_LANES, SC_LANES)
                if m == 0:
                    out_blk[i, sl] = accs[cc]
                else:
                    plsc.addupdate(out_blk.at[i, sl], accs[cc])

    @pl.loop(0, per_w // TOK_BLOCK)
    def _(bi):
        t0 = wid * per_w + bi * TOK_BLOCK
        pltpu.sync_copy(idx_hbm.at[pl.ds(t0, TOK_BLOCK)], idx_blk)
        pltpu.sync_copy(g_hbm.at[pl.ds(t0, TOK_BLOCK)], g_blk)
        pltpu.sync_copy(h_hbm.at[pl.ds(t0, TOK_BLOCK)], h_blk)
        def task_gather(i, n):
            table = u_hbm if n < N_GATHER else v_hbm
            return gather(table, i, n % N_GATHER, n % N_ROW_BUF)

        for n in range(LOOKAHEAD):
            task_gather(0, n).start()

        @pl.loop(0, TOK_BLOCK)
        def _(i):
            for n in range(2 * N_GATHER):
                ahead = n + LOOKAHEAD
                if ahead < 2 * N_GATHER:
                    task_gather(i, ahead).start()
                else:
                    @pl.when(i + 1 < TOK_BLOCK)
                    def _():
                        task_gather(i + 1, ahead - 2 * N_GATHER).start()
                task_gather(i, n).wait()
                if n < N_GATHER:
                    u_compute(i, n, rows[n % N_ROW_BUF])
                else:
                    v_compute(i, n - N_GATHER, rows[n % N_ROW_BUF])

        pltpu.sync_copy(out_blk, o_hbm.at[pl.ds(t0, TOK_BLOCK)])


def peer_eval_call(h2, idx, g, u_tab, v_tab, n_tok=None, table_row0=0):
    N = h2.shape[0] if n_tok is None else n_tok
    per_w = N // SC_WORKERS
    assert N % (SC_WORKERS * TOK_BLOCK) == 0
    assert (2 * N_GATHER) % N_ROW_BUF == 0 and LOOKAHEAD < N_ROW_BUF
    mesh = plsc.VectorSubcoreMesh(core_axis_name="c", subcore_axis_name="s",
                                  num_cores=SC_CORES, num_subcores=SC_WORKERS // SC_CORES)
    return pl.kernel(
        functools.partial(_peer_eval_body, per_w),
        out_type=jax.ShapeDtypeStruct((N, D_MODEL), F32),
        mesh=mesh,
        scratch_types=[
            pltpu.VMEM((TOK_BLOCK, N_GATHER, ROWS_PER_GATHER), jnp.int32),
            pltpu.VMEM((TOK_BLOCK, HK), F32),
            pltpu.VMEM((TOK_BLOCK, D_MODEL), F32),
            pltpu.VMEM((TOK_BLOCK, D_MODEL), F32),
            pltpu.VMEM((HK,), F32),
        ] + [pltpu.VMEM((ROWS_PER_GATHER, D_MODEL), F32)] * N_ROW_BUF + [pltpu.SemaphoreType.DMA] * N_ROW_BUF,
        compiler_params=pltpu.CompilerParams(needs_layout_passes=False),
        name="peer_eval",
    )(h2, (idx + table_row0).reshape(h2.shape[0], N_GATHER, ROWS_PER_GATHER), g, u_tab, v_tab)


N_EXPERTS = N_KEYS * N_KEYS
E_HALF = N_EXPERTS // 2
TC_TOK_TILE = 16
TC_EVAL_SHARE = ((0.1875, 0.1875, 0.1875, 0.1875), (0.25, 0.25, 0.3125, 0.375))
COMPILER_SCRATCH_BYTES = 2 * 1024 * 1024


def _tc_eval_kernel(table_row0, idx_s, hi_ref, g_ref, h_ref, u_hbm, v_hbm, o_ref,
                    tab, actw, outp, rsum, wb_all, stage, sem):
    p = pl.program_id(0)
    i = pl.program_id(1)
    for ph, (src, half) in enumerate(((u_hbm, 0), (u_hbm, 1), (v_hbm, 0), (v_hbm, 1))):
        @pl.when(jnp.logical_and(p == ph, i == 0))
        def _():
            cp = pltpu.make_async_copy(src.at[pl.ds(table_row0 + half * E_HALF, E_HALF)], tab, sem)
            cp.start()
            cp.wait()

    lane = lax.broadcasted_iota(jnp.int32, (HK, LANES), 1)
    row = lax.broadcasted_iota(jnp.int32, (HK, LANES), 0)
    in_half = hi_ref[0] == (p % 2)

    @pl.when(p < 2)
    def _():
        acts = jnp.zeros((HK, LANES), F32)
        for t in range(TC_TOK_TILE):
            h = h_ref[t]
            for k in range(HK):
                rsum[t, pl.ds(k, 1), :] = jnp.sum(tab[idx_s[t, k]] * h, axis=0, keepdims=True)
            acts = jnp.where(lane == t, jnp.sum(rsum[t], axis=1, keepdims=True), acts)
        part = jnp.where(in_half, acts, 0.0)

        @pl.when(p == 0)
        def _():
            actw[i] = part

        @pl.when(p == 1)
        def _():
            actw[i] = g_ref[0] * jax.nn.gelu(actw[i] + part)

    @pl.when(p >= 2)
    def _():
        wm = jnp.where(in_half, actw[i], 0.0)

        for t in range(TC_TOK_TILE):
            wb_all[t] = jnp.broadcast_to(wm[:, t:t + 1], (HK, LANES))

        for t in range(TC_TOK_TILE):
            accs = [jnp.zeros((SUBLANES, LANES), F32)] * 4
            for k in range(HK):
                w = jnp.broadcast_to(wb_all[t, pl.ds(k, 1), :], (SUBLANES, LANES))
                accs[k % 4] = accs[k % 4] + w * tab[idx_s[t, k]]
            stage[t] = (accs[0] + accs[1]) + (accs[2] + accs[3])
        tile_toks = pl.ds(pl.multiple_of(i * TC_TOK_TILE, TC_TOK_TILE), TC_TOK_TILE)

        @pl.when(p == 2)
        def _():
            outp[tile_toks] = stage[...]

        @pl.when(p == 3)
        def _():
            o_ref[...] = outp[tile_toks] + stage[...]


def peer_eval_tc_call(h2, idx, g, u_tab, v_tab, first_tok=0, table_row0=0):
    n_all = h2.shape[0]
    nt = n_all - first_tok
    n_tiles = nt // TC_TOK_TILE
    first_tile = first_tok // TC_TOK_TILE
    n_rows = u_tab.shape[0]
    assert nt % TC_TOK_TILE == 0 and first_tok % TC_TOK_TILE == 0 and table_row0 + N_EXPERTS <= n_rows
    idx, g = idx[first_tok:], g[first_tok:]

    def pairs_on_sublanes(a, fill):
        a = a.reshape(n_tiles, TC_TOK_TILE, HK).transpose(0, 2, 1)
        return jnp.pad(a, ((0, 0), (0, 0), (0, LANES - TC_TOK_TILE)), constant_values=fill)

    tile3 = lambda: pl.BlockSpec((1, HK, LANES), lambda p, i: (i, 0, 0))
    scratch_shapes = [(E_HALF, SUBLANES, LANES), (n_tiles, HK, LANES), (nt, SUBLANES, LANES),
                      (TC_TOK_TILE, HK, LANES), (TC_TOK_TILE, HK, LANES), (TC_TOK_TILE, SUBLANES, LANES)]
    block_shapes = [(HK, LANES), (HK, LANES), (TC_TOK_TILE, SUBLANES, LANES), (TC_TOK_TILE, SUBLANES, LANES)]
    f32_bytes = 4
    vmem_bytes = (sum(int(np.prod(s)) for s in scratch_shapes) + 2 * sum(int(np.prod(s)) for s in block_shapes)
                  ) * f32_bytes + COMPILER_SCRATCH_BYTES
    assert vmem_bytes <= V7X_VMEM_BYTES
    out = pl.pallas_call(
        functools.partial(_tc_eval_kernel, table_row0), grid=(4, n_tiles),
        in_specs=[pl.BlockSpec((TC_TOK_TILE, HK), lambda p, i: (i, 0), memory_space=pltpu.SMEM),
                  tile3(), tile3(),
                  pl.BlockSpec((TC_TOK_TILE, SUBLANES, LANES), lambda p, i: (first_tile + i, 0, 0)),
                  pl.BlockSpec(memory_space=pl.ANY), pl.BlockSpec(memory_space=pl.ANY)],
        out_specs=pl.BlockSpec((TC_TOK_TILE, SUBLANES, LANES), lambda p, i: (jnp.where(p == 3, i, 0), 0, 0)),
        out_shape=jax.ShapeDtypeStruct((nt, SUBLANES, LANES), F32),
        scratch_shapes=[pltpu.VMEM(s, F32) for s in scratch_shapes] + [pltpu.SemaphoreType.DMA],
        compiler_params=pltpu.CompilerParams(dimension_semantics=("arbitrary", "arbitrary"),
                                             vmem_limit_bytes=vmem_bytes),
        name="peer_eval_tc",
    )(idx & (E_HALF - 1), pairs_on_sublanes(idx // E_HALF, 2), pairs_on_sublanes(g, 0.0),
      h2.reshape(n_all, SUBLANES, LANES),
      u_tab.reshape(n_rows, SUBLANES, LANES), v_tab.reshape(n_rows, SUBLANES, LANES))
    return out.reshape(nt, D_MODEL)


def _final_kernel(x_ref, y_ref, gt_ref, g_ref, o_ref):
    x = x_ref[...] + gt_ref[0] * y_ref[...]
    ms = jnp.mean(x * x, axis=-1, keepdims=True)
    o_ref[...] = x * lax.rsqrt(ms + EPS) * g_ref[...]


def final_call(x, y, mod, gf, T, tm=512):
    N = x.shape[0]
    row = pl.BlockSpec((tm, D_MODEL), lambda i: (i, 0))
    return pl.pallas_call(
        _final_kernel, grid=(N // tm,),
        in_specs=[row, row, _mod_spec(tm, T, 5), pl.BlockSpec((1, D_MODEL), lambda i: (0, 0))],
        out_specs=row, out_shape=jax.ShapeDtypeStruct((N, D_MODEL), F32),
        compiler_params=_cparams(("parallel",)), name="final_norm",
    )(x, y, mod, gf)


def kernel(x, c, ada_w, ada_b, norm1_g, norm2_g, w_in, sgu_w, sgu_b, out_norm_a, out_norm_b, w_out,
           peer_wq, peer_k1, peer_k2, peer_u, peer_v, final_g):
    B, T, Dm = x.shape
    L = ada_w.shape[0]
    N = B * T
    mods = ada_mod(c, ada_w, ada_b).reshape(L, B * 6, 1, Dm)
    assert sum(GROUP_BATCHES) == B and len(TC_EVAL_SHARE) == L
    assert all(len(s) == len(GROUP_BATCHES) for s in TC_EVAL_SHARE)
    n_groups = len(GROUP_BATCHES)
    first_batch = [sum(GROUP_BATCHES[:g]) for g in range(n_groups)]
    gmods = [[mods[l, first_batch[g] * 6:(first_batch[g] + GROUP_BATCHES[g]) * 6] for g in range(n_groups)]
             for l in range(L)]
    xs = [x.reshape(N, Dm)] * n_groups
    ys = [None] * n_groups
    n_exp = peer_u.shape[1]
    u_all = peer_u.reshape(L * n_exp, Dm)
    v_all = peer_v.reshape(L * n_exp, Dm)
    sc_step = SC_WORKERS * TOK_BLOCK
    for l in range(L):
        w_in_bf, w_out_bf, wq_bf = w_in[l].astype(BF16), w_out[l].astype(BF16), peer_wq[l].astype(BF16)
        bias_full = jnp.repeat(sgu_b[l].T, GROUP, axis=1)
        for g in range(n_groups):
            Bg = GROUP_BATCHES[g]
            Ng = Bg * T
            mod = gmods[l][g]
            x_first = first_batch[g] * T if l == 0 else 0
            outs = proj_call(xs[g], ys[g], gmods[l - 1][g] if l else None, mod, norm1_g[l].reshape(1, Dm), w_in_bf, T,
                             n_tok=Ng, x_first_tok=x_first)
            if l:
                xs[g], outs = outs[0], outs[1:]
            ua, va, q, k, v = outs
            ya = gmlp_call(ua, va, sgu_w[l], bias_full, out_norm_a[l].reshape(1, D_A))
            yb = attn_call(q, k, v, out_norm_b[l].reshape(1, D_B), Bg, T)
            xs[g], h2 = out_call(ya, yb, xs[g], w_out_bf, mod, norm2_g[l].reshape(1, Dm), T, x_first_tok=x_first)
            idx, gate = peerq_call(h2, wq_bf, peer_k1[l], peer_k2[l])
            n_sc = int(Ng * (1.0 - TC_EVAL_SHARE[l][g])) // sc_step * sc_step
            y_sc = peer_eval_call(h2, idx, gate, u_all, v_all, n_tok=n_sc, table_row0=l * n_exp)
            y_tc = peer_eval_tc_call(h2, idx, gate, u_all, v_all, first_tok=n_sc, table_row0=l * n_exp)
            ys[g] = jnp.concatenate([y_sc, y_tc], axis=0)
    outs = [final_call(xs[g], ys[g], gmods[L - 1][g], final_g.reshape(1, Dm), T) for g in range(n_groups)]
    return jnp.concatenate(outs, axis=0).reshape(B, T, Dm)
```

```python
import functools
import math

import numpy as np
import jax
import jax.numpy as jnp
from jax import lax
from jax.experimental import pallas as pl
from jax.experimental.pallas import tpu as pltpu
from jax.experimental.pallas import tpu_sc as plsc

F32 = jnp.float32
BF16 = jnp.bfloat16
HIGHEST = lax.Precision.HIGHEST

D_MODEL = 1024
D_A = 512
D_B = 512
GROUP = 64
CHUNK = 128
N_KEYS = 128
PEER_HEADS = 8
PEER_TOPK = 16
HK = PEER_HEADS * PEER_TOPK
EPS = 1e-6
LANES = 128
SUBLANES = 8
V7X_VMEM_BYTES = 64 * 1024 * 1024
VMEM_LIMIT = V7X_VMEM_BYTES * 3 // 4
GROUP_BATCHES = (4, 4, 4, 2, 2)

_STAIR = [(a, b) for a in range(PEER_TOPK) for b in range(PEER_TOPK) if (a + 1) * (b + 1) <= PEER_TOPK]
N_STAIR = len(_STAIR)


def _cparams(sem):
    return pltpu.CompilerParams(dimension_semantics=sem, vmem_limit_bytes=VMEM_LIMIT)


def _dot(a, b):
    return jnp.dot(a, b, preferred_element_type=F32)


def _block_diag_mean(n, group):
    i = np.arange(n)
    return jnp.asarray((i[:, None] // group == i[None, :] // group).astype(np.float32) / group, BF16)


def _ada_kernel(c_ref, w_ref, b_ref, o_ref):
    c = c_ref[...]
    ca = c * jax.nn.sigmoid(c)
    o_ref[0] = jnp.dot(ca, w_ref[0], precision=HIGHEST, preferred_element_type=F32) + b_ref[0]


def ada_mod(c, ada_w, ada_b):
    L, Dm, E = ada_w.shape
    Bc = c.shape[0]
    tn = 1536
    return pl.pallas_call(
        _ada_kernel,
        grid=(L, E // tn),
        in_specs=[pl.BlockSpec((Bc, Dm), lambda l, j: (0, 0)),
                  pl.BlockSpec((1, Dm, tn), lambda l, j: (l, 0, j)),
                  pl.BlockSpec((1, 1, tn), lambda l, j: (l, 0, j))],
        out_specs=pl.BlockSpec((1, Bc, tn), lambda l, j: (l, 0, j)),
        out_shape=jax.ShapeDtypeStruct((L, Bc, E), F32),
        compiler_params=_cparams(("parallel", "parallel")),
        name="ada_mod",
    )(c, ada_w, ada_b.reshape(L, 1, E))


def _mod_spec(tm, T, j):
    return pl.BlockSpec((1, 1, D_MODEL), lambda i: (((i * tm) // T) * 6 + j, 0, 0))


def _rms_mod(x, g, sc, sh):
    ms = jnp.mean(x * x, axis=-1, keepdims=True)
    return (x * lax.rsqrt(ms + EPS) * g) * (1.0 + sc) + sh


def _proj_kernel(has_y, *refs):
    if has_y:
        x_ref, y_ref, gt_ref, g_ref, sc_ref, sh_ref, w_ref, xo_ref, ua_ref, va_ref, q_ref, k_ref, v_ref = refs
        x = x_ref[...] + gt_ref[0] * y_ref[...]
        xo_ref[...] = x
    else:
        x_ref, g_ref, sc_ref, sh_ref, w_ref, ua_ref, va_ref, q_ref, k_ref, v_ref = refs
        x = x_ref[...]
    h = _rms_mod(x, g_ref[...], sc_ref[0], sh_ref[0]).astype(BF16)
    outs = (ua_ref, va_ref, q_ref, k_ref, v_ref)
    for j, o_ref in enumerate(outs):
        p = _dot(h, w_ref[:, j * D_A:(j + 1) * D_A])
        if j < 2:
            p = jax.nn.gelu(p)
        o_ref[...] = p.astype(o_ref.dtype)


def proj_call(x, y, mod_prev, mod, g1, w_in_bf, T, tm=512, n_tok=None, x_first_tok=0):
    N = x.shape[0] if n_tok is None else n_tok
    has_y = y is not None
    row = pl.BlockSpec((tm, D_MODEL), lambda i: (i, 0))
    half = pl.BlockSpec((tm, D_A), lambda i: (i, 0))
    in_specs = [pl.BlockSpec((tm, D_MODEL), lambda i: (x_first_tok // tm + i, 0))]
    args = [x]
    if has_y:
        in_specs += [row, _mod_spec(tm, T, 5)]
        args += [y, mod_prev]
    in_specs += [pl.BlockSpec((1, D_MODEL), lambda i: (0, 0)), _mod_spec(tm, T, 1), _mod_spec(tm, T, 0),
                 pl.BlockSpec(w_in_bf.shape, lambda i: (0, 0))]
    args += [g1, mod, mod, w_in_bf]
    out_specs = [half] * 5
    out_shape = [jax.ShapeDtypeStruct((N, D_A), F32)] * 2 + [jax.ShapeDtypeStruct((N, D_A), BF16)] * 3
    if has_y:
        out_specs = [row] + out_specs
        out_shape = [jax.ShapeDtypeStruct((N, D_MODEL), F32)] + out_shape
    return pl.pallas_call(
        functools.partial(_proj_kernel, has_y),
        grid=(N // tm,), in_specs=in_specs, out_specs=out_specs, out_shape=out_shape,
        compiler_params=_cparams(("parallel",)), name="proj",
    )(*args)


def _gmlp_kernel(ua_ref, va_ref, w_ref, bias_ref, a_ref, ga_ref, o_ref):
    A = a_ref[...]
    row = lax.broadcasted_iota(jnp.int32, (CHUNK, CHUNK), 0)
    col = lax.broadcasted_iota(jnp.int32, (CHUNK, CHUNK), 1)
    causal = row >= col
    first_group = col < GROUP
    ws = [jnp.where(causal, w_ref[g], 0.0).astype(BF16) for g in range(D_A // GROUP)]
    bias = bias_ref[...]
    ga = ga_ref[...]
    for c in range(ua_ref.shape[0] // CHUNK):
        rows = slice(c * CHUNK, (c + 1) * CHUNK)
        v = va_ref[rows, :]
        v_hi = v.astype(BF16)
        v_lo = (v - v_hi.astype(F32)).astype(BF16)
        d = v - (_dot(v_hi, A) + _dot(v_lo, A))
        var = _dot((d * d).astype(BF16), A)
        vn = (d * lax.rsqrt(var + EPS)).astype(BF16)
        parts = []
        for p in range(D_A // LANES):
            vp = vn[:, p * LANES:(p + 1) * LANES]
            parts.append(jnp.where(first_group, _dot(ws[2 * p], vp), _dot(ws[2 * p + 1], vp)))
        s = jnp.concatenate(parts, axis=1) + bias
        y = ua_ref[rows, :] * s
        ms = _dot((y * y).astype(BF16), A)
        o_ref[rows, :] = (y * lax.rsqrt(ms + EPS) * ga).astype(BF16)


def gmlp_call(ua, va, sgu_w, bias_full, ga, tm=512):
    N = ua.shape[0]
    half = pl.BlockSpec((tm, D_A), lambda i: (i, 0))
    return pl.pallas_call(
        _gmlp_kernel, grid=(N // tm,),
        in_specs=[half, half,
                  pl.BlockSpec(sgu_w.shape, lambda i: (0, 0, 0)),
                  pl.BlockSpec((CHUNK, D_A), lambda i: (0, 0)),
                  pl.BlockSpec((D_A, D_A), lambda i: (0, 0)),
                  pl.BlockSpec((1, D_A), lambda i: (0, 0))],
        out_specs=half, out_shape=jax.ShapeDtypeStruct((N, D_A), BF16),
        compiler_params=_cparams(("parallel",)), name="gmlp",
    )(ua, va, sgu_w, bias_full, _block_diag_mean(D_A, GROUP), ga)


EXP_UNDERFLOW = -104.0


def _attn_kernel(q_ref, k_ref, v_ref, mo_ref, a_ref, gb_ref, o_ref, *scratch):
    i = pl.program_id(1)
    n_pair = D_B // LANES
    lane = lax.broadcasted_iota(jnp.int32, (CHUNK, LANES), 1)
    row = lax.broadcasted_iota(jnp.int32, (CHUNK, LANES), 0)
    head_lanes = (lane < GROUP, lane >= GROUP)
    MO = mo_ref[...]
    scale = 1.0 / math.sqrt(GROUP)
    acc_refs, carry_refs, qs_refs = scratch[:n_pair], scratch[n_pair:2 * n_pair], scratch[2 * n_pair:]
    for p in range(n_pair):
        acc_refs[p][...] = jnp.zeros_like(acc_refs[p])
        carry_refs[p][...] = jnp.zeros_like(carry_refs[p])
        q = q_ref[:, p * LANES:(p + 1) * LANES]
        qs_refs[p][...] = jnp.concatenate([jnp.where(hl, q, jnp.zeros_like(q)) for hl in head_lanes], axis=0)

    def cond(state):
        jj, cmax = state
        return jnp.logical_and(jj <= i, cmax > EXP_UNDERFLOW)

    def body(state):
        jj, _ = state
        j = i - jj
        start = pl.multiple_of(j * CHUNK, CHUNK)
        mask1 = (j * CHUNK + lane) < (i * CHUNK + row)
        mask = jnp.concatenate([mask1, mask1], axis=0)
        cmax = jnp.full((2 * CHUNK, LANES), -jnp.inf, F32)
        slabs = range(n_pair)
        cols = [slice(p * LANES, (p + 1) * LANES) for p in slabs]
        zs = [lax.dot_general(qs_refs[p][...], k_ref[pl.ds(start, CHUNK), cols[p]], (((1,), (1,)), ((), ())),
                              preferred_element_type=F32) * scale for p in slabs]
        lszs = [jnp.minimum(z, 0.0) - jnp.log1p(jnp.exp(-jnp.abs(z))) for z in zs]
        trs = []
        for p in slabs:
            L = jnp.where(mask, lszs[p] - zs[p], 0.0)
            L_hi = L.astype(BF16)
            L_lo = (L - L_hi.astype(F32)).astype(BF16)
            tr = _dot(jnp.concatenate([L_hi, L_lo], axis=0), MO)
            trs.append(tr[:2 * CHUNK] + tr[2 * CHUNK:])
        for p in slabs:
            c = carry_refs[p][...]
            a = jnp.where(mask, jnp.exp(lszs[p] + trs[p][:, :LANES] + c), 0.0).astype(BF16)
            a_cat = jnp.concatenate([a[:CHUNK], a[CHUNK:]], axis=1)
            vs = v_ref[pl.ds(start, CHUNK), cols[p]]
            v_cat = jnp.concatenate([jnp.where(hl, vs, jnp.zeros_like(vs)) for hl in head_lanes], axis=0)
            acc_refs[p][...] += _dot(a_cat, v_cat)
            c = c + trs[p][:, LANES:]
            carry_refs[p][...] = c
            cmax = jnp.maximum(cmax, c)
        return jj + 1, jnp.max(cmax)

    lax.while_loop(cond, body, (jnp.int32(0), jnp.float32(0.0)))
    for p in range(n_pair):
        cols = slice(p * LANES, (p + 1) * LANES)
        acc = acc_refs[p][...]
        ms = _dot((acc * acc).astype(BF16), a_ref[...])
        o_ref[:, cols] = (acc * lax.rsqrt(ms + EPS) * gb_ref[:, cols]).astype(BF16)


def attn_call(q, k, v, gb, B, T):
    N = q.shape[0]
    nq = T // CHUNK
    i = np.arange(CHUNK)
    MO = jnp.asarray(np.concatenate([(i[:, None] > i[None, :]), np.ones((CHUNK, LANES), bool)], axis=1)
                     .astype(np.float32), BF16)
    kv = pl.BlockSpec((T, D_B), lambda b, i: (b, 0))
    qo = pl.BlockSpec((CHUNK, D_B), lambda b, i: (b * nq + i, 0))
    n_pair = D_B // LANES
    return pl.pallas_call(
        _attn_kernel, grid=(B, nq),
        in_specs=[qo, kv, kv, pl.BlockSpec((CHUNK, 2 * LANES), lambda b, i: (0, 0)),
                  pl.BlockSpec((CHUNK, LANES), lambda b, i: (0, 0)), pl.BlockSpec((1, D_B), lambda b, i: (0, 0))],
        out_specs=qo,
        out_shape=jax.ShapeDtypeStruct((N, D_B), BF16),
        scratch_shapes=([pltpu.VMEM((CHUNK, LANES), F32)] * n_pair + [pltpu.VMEM((2 * CHUNK, LANES), F32)] * n_pair
                        + [pltpu.VMEM((2 * CHUNK, LANES), BF16)] * n_pair),
        compiler_params=_cparams(("parallel", "arbitrary")), name="stickbreak_attn",
    )(q, k, v, MO, _block_diag_mean(LANES, GROUP), gb)


def _out_kernel(ya_ref, yb_ref, x_ref, w_ref, gt_ref, g2_ref, sc_ref, sh_ref, xo_ref, h2_ref):
    o = _dot(ya_ref[...], w_ref[:D_A, :]) + _dot(yb_ref[...], w_ref[D_A:, :])
    x = x_ref[...] + gt_ref[0] * o
    xo_ref[...] = x
    h2_ref[...] = _rms_mod(x, g2_ref[...], sc_ref[0], sh_ref[0])


def out_call(ya, yb, x, w_out_bf, mod, g2, T, tm=512, x_first_tok=0):
    N = ya.shape[0]
    row = pl.BlockSpec((tm, D_MODEL), lambda i: (i, 0))
    x_row = pl.BlockSpec((tm, D_MODEL), lambda i: (x_first_tok // tm + i, 0))
    half = pl.BlockSpec((tm, D_A), lambda i: (i, 0))
    return pl.pallas_call(
        _out_kernel, grid=(N // tm,),
        in_specs=[half, half, x_row, pl.BlockSpec(w_out_bf.shape, lambda i: (0, 0)), _mod_spec(tm, T, 2),
                  pl.BlockSpec((1, D_MODEL), lambda i: (0, 0)), _mod_spec(tm, T, 4), _mod_spec(tm, T, 3)],
        out_specs=[row, row], out_shape=[jax.ShapeDtypeStruct((N, D_MODEL), F32)] * 2,
        compiler_params=_cparams(("parallel",)), name="out_proj",
    )(ya, yb, x, w_out_bf, mod, g2, mod, mod)


N_CAND_ROWS = -(-N_STAIR // SUBLANES) * SUBLANES


def _topk_cols(s, payload=None):
    n_rows, tb = s.shape
    row_f = lax.broadcasted_iota(jnp.int32, (n_rows, tb), 0).astype(F32)
    rank = lax.broadcasted_iota(jnp.int32, (PEER_TOPK, tb), 0)
    neg = jnp.float32(-jnp.inf)

    def body(r, carry):
        s, vals, second = carry
        m = jnp.max(s, axis=0, keepdims=True)
        pos = jnp.min(jnp.where(s == m, row_f, float(n_rows)), axis=0, keepdims=True)
        hit = row_f == pos
        out = pos if payload is None else jnp.max(jnp.where(hit, payload, -1.0), axis=0, keepdims=True)
        dst = rank == r
        return jnp.where(hit, neg, s), jnp.where(dst, m, vals), jnp.where(dst, out, second)

    zeros = jnp.zeros((PEER_TOPK, tb), F32)
    _, vals, second = lax.fori_loop(0, PEER_TOPK, body, (s, zeros, zeros))
    return vals, second


def _peerq_kernel(h2_ref, wq_ref, k1_ref, k2_ref, sel_ref, sele_ref, idx_ref, g_ref, q_scr, it_scr, gt_scr):
    h = pl.program_id(1)
    tm = h2_ref.shape[0]

    @pl.when(h == 0)
    def _():
        q_scr[...] = _dot(h2_ref[...].astype(BF16), wq_ref[...]).astype(BF16)

    nt = (((1,), (1,)), ((), ()))
    vals, keys = [], []
    for tb in range(tm // LANES):
        qh = q_scr[tb * LANES:(tb + 1) * LANES, pl.ds(pl.multiple_of(h * LANES, LANES), LANES)]
        s12 = jnp.concatenate([lax.dot_general(k1_ref[...], qh, nt, preferred_element_type=F32),
                               lax.dot_general(k2_ref[...], qh, nt, preferred_element_type=F32)], axis=1)
        v12, i12 = _topk_cols(s12)
        vals.append(jnp.concatenate([v12[:, :LANES], v12[:, LANES:]], axis=0))
        keys.append(jnp.concatenate([i12[:, :LANES], i12[:, LANES:]], axis=0))
    vals = jnp.concatenate(vals, axis=1)
    keys = jnp.concatenate(keys, axis=1)
    cand = jnp.dot(sel_ref[...], vals, precision=HIGHEST, preferred_element_type=F32)
    cand_row = lax.broadcasted_iota(jnp.int32, cand.shape, 0)
    cand = jnp.where(cand_row < N_STAIR, cand, -jnp.inf)
    ecand = jnp.dot(sele_ref[...], keys, precision=HIGHEST, preferred_element_type=F32)
    sc, ex = _topk_cols(cand, ecand)
    p = jnp.exp(sc - sc[0:1, :])
    out_rows = pl.ds(pl.multiple_of(h * PEER_TOPK, PEER_TOPK), PEER_TOPK)
    it_scr[out_rows, :] = ex
    gt_scr[out_rows, :] = p / jnp.sum(p, axis=0, keepdims=True)

    @pl.when(h == PEER_HEADS - 1)
    def _():
        idx_ref[...] = it_scr[...].T.astype(jnp.int32)
        g_ref[...] = gt_scr[...].T


def peerq_call(h2, wq_bf, k1, k2, tm=256):
    N = h2.shape[0]
    half = N_KEYS // 2
    k1p = jnp.concatenate([k1, jnp.zeros((N_KEYS, half), F32)], axis=1).astype(BF16)
    k2p = jnp.concatenate([jnp.zeros((N_KEYS, half), F32), k2], axis=1).astype(BF16)
    sel = np.zeros((N_CAND_ROWS, 2 * PEER_TOPK), np.float32)
    sele = np.zeros((N_CAND_ROWS, 2 * PEER_TOPK), np.float32)
    for c, (a, b) in enumerate(_STAIR):
        sel[c, a] = sel[c, PEER_TOPK + b] = 1.0
        sele[c, a] = float(N_KEYS)
        sele[c, PEER_TOPK + b] = 1.0
    const = lambda shape: pl.BlockSpec(shape, lambda i, h: (0, 0))
    row = pl.BlockSpec((tm, HK), lambda i, h: (i, 0))
    return pl.pallas_call(
        _peerq_kernel, grid=(N // tm, PEER_HEADS),
        in_specs=[pl.BlockSpec((tm, D_MODEL), lambda i, h: (i, 0)), const(wq_bf.shape), const((N_KEYS, LANES)),
                  const((N_KEYS, LANES)), const(sel.shape), const(sele.shape)],
        out_specs=[row, row],
        out_shape=[jax.ShapeDtypeStruct((N, HK), jnp.int32), jax.ShapeDtypeStruct((N, HK), F32)],
        scratch_shapes=[pltpu.VMEM((tm, D_MODEL), BF16), pltpu.VMEM((HK, tm), F32), pltpu.VMEM((HK, tm), F32)],
        compiler_params=_cparams(("parallel", "arbitrary")), name="peer_retrieve",
    )(h2, wq_bf, k1p, k2p, jnp.asarray(sel), jnp.asarray(sele))


SC_LANES = 16
SC_CORES = 2
SC_WORKERS = 32
ROWS_PER_GATHER = 16
N_ROW_BUF = 4
LOOKAHEAD = N_ROW_BUF - 1
TOK_BLOCK = 16
ROW_BLOCK = 8
U_UNROLL = 2
COL_BLOCK = 16
N_DCHUNK = D_MODEL // SC_LANES
N_GATHER = HK // ROWS_PER_GATHER


def _sc_gelu(x):
    y = math.sqrt(2.0 / math.pi) * (x + 0.044715 * (x * x * x))
    t = 1.0 - 2.0 / (jnp.exp(2.0 * y) + 1.0)
    return 0.5 * x * (1.0 + t)


def _peer_eval_body(per_w, h_hbm, idx_hbm, g_hbm, u_hbm, v_hbm, o_hbm,
                    idx_blk, g_blk, h_blk, out_blk, w_v, *bufs):
    wid = lax.axis_index("s") * SC_CORES + lax.axis_index("c")
    rows = bufs[:N_ROW_BUF]
    sems = bufs[N_ROW_BUF:]
    lane = lax.iota(jnp.int32, SC_LANES)
    zeros = jnp.zeros((SC_LANES,), F32)

    def gather(table, i, j, buf):
        return pltpu.make_async_copy(table.at[idx_blk.at[i, j]], rows[buf], sems[buf])

    def u_compute(i, n, rbuf):
        for half in range(ROWS_PER_GATHER // SC_LANES):
            actv = zeros
            for rg in range(SC_LANES // ROW_BLOCK):
                r0 = half * SC_LANES + rg * ROW_BLOCK

                def c_body(cb, accs):
                    accs = list(accs)
                    for cc in range(U_UNROLL):
                        sl = pl.ds(pl.multiple_of((cb * U_UNROLL + cc) * SC_LANES, SC_LANES), SC_LANES)
                        hc = h_blk[i, sl]
                        for r in range(ROW_BLOCK):
                            accs[r] = accs[r] + rbuf[r0 + r, sl] * hc
                    return tuple(accs)

                accs = lax.fori_loop(0, N_DCHUNK // U_UNROLL, c_body, (zeros,) * ROW_BLOCK)
                for r in range(ROW_BLOCK):
                    actv = jnp.where(lane == rg * ROW_BLOCK + r, jnp.sum(accs[r]), actv)
            sl = pl.ds(n * ROWS_PER_GATHER + half * SC_LANES, SC_LANES)
            w_v[sl] = g_blk[i, sl] * _sc_gelu(actv)

    def v_compute(i, m, rbuf):
        for cb in range(N_DCHUNK // COL_BLOCK):
            def r_body(r, accs):
                wk = plsc.load_gather(w_v, [jnp.full((SC_LANES,), m * ROWS_PER_GATHER, jnp.int32) + r])
                return tuple(accs[cc] + wk * rbuf[r, pl.ds((cb * COL_BLOCK + cc) * SC_LANES, SC_LANES)]
                             for cc in range(COL_BLOCK))

            accs = lax.fori_loop(0, ROWS_PER_GATHER, r_body, (zeros,) * COL_BLOCK)
            for cc in range(COL_BLOCK):
                sl = pl.ds((cb * COL_BLOCK + cc) * SC_LANES, SC_LANES)
                if m == 0:
                    out_blk[i, sl] = accs[cc]
                else:
                    plsc.addupdate(out_blk.at[i, sl], accs[cc])

    @pl.loop(0, per_w // TOK_BLOCK)
    def _(bi):
        t0 = wid * per_w + bi * TOK_BLOCK
        pltpu.sync_copy(idx_hbm.at[pl.ds(t0, TOK_BLOCK)], idx_blk)
        pltpu.sync_copy(g_hbm.at[pl.ds(t0, TOK_BLOCK)], g_blk)
        pltpu.sync_copy(h_hbm.at[pl.ds(t0, TOK_BLOCK)], h_blk)
        def task_gather(i, n):
            table = u_hbm if n < N_GATHER else v_hbm
            return gather(table, i, n % N_GATHER, n % N_ROW_BUF)

        for n in range(LOOKAHEAD):
            task_gather(0, n).start()

        @pl.loop(0, TOK_BLOCK)
        def _(i):
            for n in range(2 * N_GATHER):
                ahead = n + LOOKAHEAD
                if ahead < 2 * N_GATHER:
                    task_gather(i, ahead).start()
                else:
                    @pl.when(i + 1 < TOK_BLOCK)
                    def _():
                        task_gather(i + 1, ahead - 2 * N_GATHER).start()
                task_gather(i, n).wait()
                if n < N_GATHER:
                    u_compute(i, n, rows[n % N_ROW_BUF])
                else:
                    v_compute(i, n - N_GATHER, rows[n % N_ROW_BUF])

        pltpu.sync_copy(out_blk, o_hbm.at[pl.ds(t0, TOK_BLOCK)])


def peer_eval_call(h2, idx, g, u_tab, v_tab, n_tok=None, table_row0=0):
    N = h2.shape[0] if n_tok is None else n_tok
    per_w = N // SC_WORKERS
    assert N % (SC_WORKERS * TOK_BLOCK) == 0
    assert (2 * N_GATHER) % N_ROW_BUF == 0 and LOOKAHEAD < N_ROW_BUF
    mesh = plsc.VectorSubcoreMesh(core_axis_name="c", subcore_axis_name="s",
                                  num_cores=SC_CORES, num_subcores=SC_WORKERS // SC_CORES)
    return pl.kernel(
        functools.partial(_peer_eval_body, per_w),
        out_type=jax.ShapeDtypeStruct((N, D_MODEL), F32),
        mesh=mesh,
        scratch_types=[
            pltpu.VMEM((TOK_BLOCK, N_GATHER, ROWS_PER_GATHER), jnp.int32),
            pltpu.VMEM((TOK_BLOCK, HK), F32),
            pltpu.VMEM((TOK_BLOCK, D_MODEL), F32),
            pltpu.VMEM((TOK_BLOCK, D_MODEL), F32),
            pltpu.VMEM((HK,), F32),
        ] + [pltpu.VMEM((ROWS_PER_GATHER, D_MODEL), F32)] * N_ROW_BUF + [pltpu.SemaphoreType.DMA] * N_ROW_BUF,
        compiler_params=pltpu.CompilerParams(needs_layout_passes=False),
        name="peer_eval",
    )(h2, (idx + table_row0).reshape(h2.shape[0], N_GATHER, ROWS_PER_GATHER), g, u_tab, v_tab)


N_EXPERTS = N_KEYS * N_KEYS
E_HALF = N_EXPERTS // 2
TC_TOK_TILE = 16
TC_EVAL_SHARE = ((0.1875,) * 5, (0.25, 0.25, 0.25, 0.375, 0.375))
COMPILER_SCRATCH_BYTES = 2 * 1024 * 1024


def _tc_eval_kernel(table_row0, idx_s, hi_ref, g_ref, h_ref, u_hbm, v_hbm, o_ref,
                    tab, actw, outp, rsum, wb_all, stage, sem):
    p = pl.program_id(0)
    i = pl.program_id(1)
    for ph, (src, half) in enumerate(((u_hbm, 0), (u_hbm, 1), (v_hbm, 0), (v_hbm, 1))):
        @pl.when(jnp.logical_and(p == ph, i == 0))
        def _():
            cp = pltpu.make_async_copy(src.at[pl.ds(table_row0 + half * E_HALF, E_HALF)], tab, sem)
            cp.start()
            cp.wait()

    lane = lax.broadcasted_iota(jnp.int32, (HK, LANES), 1)
    row = lax.broadcasted_iota(jnp.int32, (HK, LANES), 0)
    in_half = hi_ref[0] == (p % 2)

    @pl.when(p < 2)
    def _():
        acts = jnp.zeros((HK, LANES), F32)
        for t in range(TC_TOK_TILE):
            h = h_ref[t]
            for k in range(HK):
                rsum[t, pl.ds(k, 1), :] = jnp.sum(tab[idx_s[t, k]] * h, axis=0, keepdims=True)
            acts = jnp.where(lane == t, jnp.sum(rsum[t], axis=1, keepdims=True), acts)
        part = jnp.where(in_half, acts, 0.0)

        @pl.when(p == 0)
        def _():
            actw[i] = part

        @pl.when(p == 1)
        def _():
            actw[i] = g_ref[0] * jax.nn.gelu(actw[i] + part)

    @pl.when(p >= 2)
    def _():
        wm = jnp.where(in_half, actw[i], 0.0)

        for t in range(TC_TOK_TILE):
            wb_all[t] = jnp.broadcast_to(wm[:, t:t + 1], (HK, LANES))

        for t in range(TC_TOK_TILE):
            accs = [jnp.zeros((SUBLANES, LANES), F32)] * 4
            for k in range(HK):
                w = jnp.broadcast_to(wb_all[t, pl.ds(k, 1), :], (SUBLANES, LANES))
                accs[k % 4] = accs[k % 4] + w * tab[idx_s[t, k]]
            stage[t] = (accs[0] + accs[1]) + (accs[2] + accs[3])
        tile_toks = pl.ds(pl.multiple_of(i * TC_TOK_TILE, TC_TOK_TILE), TC_TOK_TILE)

        @pl.when(p == 2)
        def _():
            outp[tile_toks] = stage[...]

        @pl.when(p == 3)
        def _():
            o_ref[...] = outp[tile_toks] + stage[...]


def peer_eval_tc_call(h2, idx, g, u_tab, v_tab, first_tok=0, table_row0=0):
    n_all = h2.shape[0]
    nt = n_all - first_tok
    n_tiles = nt // TC_TOK_TILE
    first_tile = first_tok // TC_TOK_TILE
    n_rows = u_tab.shape[0]
    assert nt % TC_TOK_TILE == 0 and first_tok % TC_TOK_TILE == 0 and table_row0 + N_EXPERTS <= n_rows
    idx, g = idx[first_tok:], g[first_tok:]

    def pairs_on_sublanes(a, fill):
        a = a.reshape(n_tiles, TC_TOK_TILE, HK).transpose(0, 2, 1)
        return jnp.pad(a, ((0, 0), (0, 0), (0, LANES - TC_TOK_TILE)), constant_values=fill)

    tile3 = lambda: pl.BlockSpec((1, HK, LANES), lambda p, i: (i, 0, 0))
    scratch_shapes = [(E_HALF, SUBLANES, LANES), (n_tiles, HK, LANES), (nt, SUBLANES, LANES),
                      (TC_TOK_TILE, HK, LANES), (TC_TOK_TILE, HK, LANES), (TC_TOK_TILE, SUBLANES, LANES)]
    block_shapes = [(HK, LANES), (HK, LANES), (TC_TOK_TILE, SUBLANES, LANES), (TC_TOK_TILE, SUBLANES, LANES)]
    f32_bytes = 4
    vmem_bytes = (sum(int(np.prod(s)) for s in scratch_shapes) + 2 * sum(int(np.prod(s)) for s in block_shapes)
                  ) * f32_bytes + COMPILER_SCRATCH_BYTES
    assert vmem_bytes <= V7X_VMEM_BYTES
    out = pl.pallas_call(
        functools.partial(_tc_eval_kernel, table_row0), grid=(4, n_tiles),
        in_specs=[pl.BlockSpec((TC_TOK_TILE, HK), lambda p, i: (i, 0), memory_space=pltpu.SMEM),
                  tile3(), tile3(),
                  pl.BlockSpec((TC_TOK_TILE, SUBLANES, LANES), lambda p, i: (first_tile + i, 0, 0)),
                  pl.BlockSpec(memory_space=pl.ANY), pl.BlockSpec(memory_space=pl.ANY)],
        out_specs=pl.BlockSpec((TC_TOK_TILE, SUBLANES, LANES), lambda p, i: (jnp.where(p == 3, i, 0), 0, 0)),
        out_shape=jax.ShapeDtypeStruct((nt, SUBLANES, LANES), F32),
        scratch_shapes=[pltpu.VMEM(s, F32) for s in scratch_shapes] + [pltpu.SemaphoreType.DMA],
        compiler_params=pltpu.CompilerParams(dimension_semantics=("arbitrary", "arbitrary"),
                                             vmem_limit_bytes=vmem_bytes),
        name="peer_eval_tc",
    )(idx & (E_HALF - 1), pairs_on_sublanes(idx // E_HALF, 2), pairs_on_sublanes(g, 0.0),
      h2.reshape(n_all, SUBLANES, LANES),
      u_tab.reshape(n_rows, SUBLANES, LANES), v_tab.reshape(n_rows, SUBLANES, LANES))
    return out.reshape(nt, D_MODEL)


def _final_kernel(x_ref, y_ref, gt_ref, g_ref, o_ref):
    x = x_ref[...] + gt_ref[0] * y_ref[...]
    ms = jnp.mean(x * x, axis=-1, keepdims=True)
    o_ref[...] = x * lax.rsqrt(ms + EPS) * g_ref[...]


def final_call(x, y, mod, gf, T, tm=512):
    N = x.shape[0]
    row = pl.BlockSpec((tm, D_MODEL), lambda i: (i, 0))
    return pl.pallas_call(
        _final_kernel, grid=(N // tm,),
        in_specs=[row, row, _mod_spec(tm, T, 5), pl.BlockSpec((1, D_MODEL), lambda i: (0, 0))],
        out_specs=row, out_shape=jax.ShapeDtypeStruct((N, D_MODEL), F32),
        compiler_params=_cparams(("parallel",)), name="final_norm",
    )(x, y, mod, gf)


def kernel(x, c, ada_w, ada_b, norm1_g, norm2_g, w_in, sgu_w, sgu_b, out_norm_a, out_norm_b, w_out,
           peer_wq, peer_k1, peer_k2, peer_u, peer_v, final_g):
    B, T, Dm = x.shape
    L = ada_w.shape[0]
    N = B * T
    mods = ada_mod(c, ada_w, ada_b).reshape(L, B * 6, 1, Dm)
    assert sum(GROUP_BATCHES) == B and len(TC_EVAL_SHARE) == L
    assert all(len(s) == len(GROUP_BATCHES) for s in TC_EVAL_SHARE)
    n_groups = len(GROUP_BATCHES)
    first_batch = [sum(GROUP_BATCHES[:g]) for g in range(n_groups)]
    gmods = [[mods[l, first_batch[g] * 6:(first_batch[g] + GROUP_BATCHES[g]) * 6] for g in range(n_groups)]
             for l in range(L)]
    xs = [x.reshape(N, Dm)] * n_groups
    ys = [None] * n_groups
    n_exp = peer_u.shape[1]
    u_all = peer_u.reshape(L * n_exp, Dm)
    v_all = peer_v.reshape(L * n_exp, Dm)
    sc_step = SC_WORKERS * TOK_BLOCK
    for l in range(L):
        w_in_bf, w_out_bf, wq_bf = w_in[l].astype(BF16), w_out[l].astype(BF16), peer_wq[l].astype(BF16)
        bias_full = jnp.repeat(sgu_b[l].T, GROUP, axis=1)
        for g in range(n_groups):
            Bg = GROUP_BATCHES[g]
            Ng = Bg * T
            mod = gmods[l][g]
            x_first = first_batch[g] * T if l == 0 else 0
            outs = proj_call(xs[g], ys[g], gmods[l - 1][g] if l else None, mod, norm1_g[l].reshape(1, Dm), w_in_bf, T,
                             n_tok=Ng, x_first_tok=x_first)
            if l:
                xs[g], outs = outs[0], outs[1:]
            ua, va, q, k, v = outs
            ya = gmlp_call(ua, va, sgu_w[l], bias_full, out_norm_a[l].reshape(1, D_A))
            yb = attn_call(q, k, v, out_norm_b[l].reshape(1, D_B), Bg, T)
            xs[g], h2 = out_call(ya, yb, xs[g], w_out_bf, mod, norm2_g[l].reshape(1, Dm), T, x_first_tok=x_first)
            idx, gate = peerq_call(h2, wq_bf, peer_k1[l], peer_k2[l])
            n_sc = int(Ng * (1.0 - TC_EVAL_SHARE[l][g])) // sc_step * sc_step
            y_sc = peer_eval_call(h2, idx, gate, u_all, v_all, n_tok=n_sc, table_row0=l * n_exp)
            y_tc = peer_eval_tc_call(h2, idx, gate, u_all, v_all, first_tok=n_sc, table_row0=l * n_exp)
            ys[g] = jnp.concatenate([y_sc, y_tc], axis=0)
    outs = [final_call(xs[g], ys[g], gmods[L - 1][g], final_g.reshape(1, Dm), T) for g in range(n_groups)]
    return jnp.concatenate(outs, axis=0).reshape(B, T, Dm)
```

```python
import functools
import math

import numpy as np
import jax
import jax.numpy as jnp
from jax import lax
from jax.experimental import pallas as pl
from jax.experimental.pallas import tpu as pltpu
from jax.experimental.pallas import tpu_sc as plsc

F32 = jnp.float32
BF16 = jnp.bfloat16
HIGHEST = lax.Precision.HIGHEST

D_MODEL = 1024
D_A = 512
D_B = 512
GROUP = 64
CHUNK = 128
N_KEYS = 128
PEER_HEADS = 8
PEER_TOPK = 16
HK = PEER_HEADS * PEER_TOPK
EPS = 1e-6
LANES = 128
SUBLANES = 8
V7X_VMEM_BYTES = 64 * 1024 * 1024
VMEM_LIMIT = V7X_VMEM_BYTES * 3 // 4
GROUP_BATCHES = (4, 4, 4, 4)

_STAIR = [(a, b) for a in range(PEER_TOPK) for b in range(PEER_TOPK) if (a + 1) * (b + 1) <= PEER_TOPK]
N_STAIR = len(_STAIR)


def _cparams(sem):
    return pltpu.CompilerParams(dimension_semantics=sem, vmem_limit_bytes=VMEM_LIMIT)


def _dot(a, b):
    return jnp.dot(a, b, preferred_element_type=F32)


def _block_diag_mean(n, group):
    i = np.arange(n)
    return jnp.asarray((i[:, None] // group == i[None, :] // group).astype(np.float32) / group, BF16)


def _ada_kernel(c_ref, w_ref, b_ref, o_ref):
    c = c_ref[...]
    ca = c * jax.nn.sigmoid(c)
    o_ref[0] = jnp.dot(ca, w_ref[0], precision=HIGHEST, preferred_element_type=F32) + b_ref[0]


def ada_mod(c, ada_w, ada_b):
    L, Dm, E = ada_w.shape
    Bc = c.shape[0]
    tn = 1536
    return pl.pallas_call(
        _ada_kernel,
        grid=(L, E // tn),
        in_specs=[pl.BlockSpec((Bc, Dm), lambda l, j: (0, 0)),
                  pl.BlockSpec((1, Dm, tn), lambda l, j: (l, 0, j)),
                  pl.BlockSpec((1, 1, tn), lambda l, j: (l, 0, j))],
        out_specs=pl.BlockSpec((1, Bc, tn), lambda l, j: (l, 0, j)),
        out_shape=jax.ShapeDtypeStruct((L, Bc, E), F32),
        compiler_params=_cparams(("parallel", "parallel")),
        name="ada_mod",
    )(c, ada_w, ada_b.reshape(L, 1, E))


def _mod_spec(tm, T, j):
    return pl.BlockSpec((1, 1, D_MODEL), lambda i: (((i * tm) // T) * 6 + j, 0, 0))


def _rms_mod(x, g, sc, sh):
    ms = jnp.mean(x * x, axis=-1, keepdims=True)
    return (x * lax.rsqrt(ms + EPS) * g) * (1.0 + sc) + sh


def _proj_kernel(has_y, *refs):
    if has_y:
        x_ref, y_ref, gt_ref, g_ref, sc_ref, sh_ref, w_ref, xo_ref, ua_ref, va_ref, q_ref, k_ref, v_ref = refs
        x = x_ref[...] + gt_ref[0] * y_ref[...]
        xo_ref[...] = x
    else:
        x_ref, g_ref, sc_ref, sh_ref, w_ref, ua_ref, va_ref, q_ref, k_ref, v_ref = refs
        x = x_ref[...]
    h = _rms_mod(x, g_ref[...], sc_ref[0], sh_ref[0]).astype(BF16)
    outs = (ua_ref, va_ref, q_ref, k_ref, v_ref)
    for j, o_ref in enumerate(outs):
        p = _dot(h, w_ref[:, j * D_A:(j + 1) * D_A])
        if j < 2:
            p = jax.nn.gelu(p)
        o_ref[...] = p.astype(o_ref.dtype)


def proj_call(x, y, mod_prev, mod, g1, w_in_bf, T, tm=512, n_tok=None, x_first_tok=0):
    N = x.shape[0] if n_tok is None else n_tok
    has_y = y is not None
    row = pl.BlockSpec((tm, D_MODEL), lambda i: (i, 0))
    half = pl.BlockSpec((tm, D_A), lambda i: (i, 0))
    in_specs = [pl.BlockSpec((tm, D_MODEL), lambda i: (x_first_tok // tm + i, 0))]
    args = [x]
    if has_y:
        in_specs += [row, _mod_spec(tm, T, 5)]
        args += [y, mod_prev]
    in_specs += [pl.BlockSpec((1, D_MODEL), lambda i: (0, 0)), _mod_spec(tm, T, 1), _mod_spec(tm, T, 0),
                 pl.BlockSpec(w_in_bf.shape, lambda i: (0, 0))]
    args += [g1, mod, mod, w_in_bf]
    out_specs = [half] * 5
    out_shape = [jax.ShapeDtypeStruct((N, D_A), F32)] * 2 + [jax.ShapeDtypeStruct((N, D_A), BF16)] * 3
    if has_y:
        out_specs = [row] + out_specs
        out_shape = [jax.ShapeDtypeStruct((N, D_MODEL), F32)] + out_shape
    return pl.pallas_call(
        functools.partial(_proj_kernel, has_y),
        grid=(N // tm,), in_specs=in_specs, out_specs=out_specs, out_shape=out_shape,
        compiler_params=_cparams(("parallel",)), name="proj",
    )(*args)


def _gmlp_kernel(ua_ref, va_ref, w_ref, bias_ref, a_ref, ga_ref, o_ref):
    A = a_ref[...]
    row = lax.broadcasted_iota(jnp.int32, (CHUNK, CHUNK), 0)
    col = lax.broadcasted_iota(jnp.int32, (CHUNK, CHUNK), 1)
    causal = row >= col
    first_group = col < GROUP
    ws = [jnp.where(causal, w_ref[g], 0.0).astype(BF16) for g in range(D_A // GROUP)]
    bias = bias_ref[...]
    ga = ga_ref[...]
    for c in range(ua_ref.shape[0] // CHUNK):
        rows = slice(c * CHUNK, (c + 1) * CHUNK)
        v = va_ref[rows, :]
        v_hi = v.astype(BF16)
        v_lo = (v - v_hi.astype(F32)).astype(BF16)
        d = v - (_dot(v_hi, A) + _dot(v_lo, A))
        var = _dot((d * d).astype(BF16), A)
        vn = (d * lax.rsqrt(var + EPS)).astype(BF16)
        parts = []
        for p in range(D_A // LANES):
            vp = vn[:, p * LANES:(p + 1) * LANES]
            parts.append(jnp.where(first_group, _dot(ws[2 * p], vp), _dot(ws[2 * p + 1], vp)))
        s = jnp.concatenate(parts, axis=1) + bias
        y = ua_ref[rows, :] * s
        ms = _dot((y * y).astype(BF16), A)
        o_ref[rows, :] = (y * lax.rsqrt(ms + EPS) * ga).astype(BF16)


def gmlp_call(ua, va, sgu_w, bias_full, ga, tm=512):
    N = ua.shape[0]
    half = pl.BlockSpec((tm, D_A), lambda i: (i, 0))
    return pl.pallas_call(
        _gmlp_kernel, grid=(N // tm,),
        in_specs=[half, half,
                  pl.BlockSpec(sgu_w.shape, lambda i: (0, 0, 0)),
                  pl.BlockSpec((CHUNK, D_A), lambda i: (0, 0)),
                  pl.BlockSpec((D_A, D_A), lambda i: (0, 0)),
                  pl.BlockSpec((1, D_A), lambda i: (0, 0))],
        out_specs=half, out_shape=jax.ShapeDtypeStruct((N, D_A), BF16),
        compiler_params=_cparams(("parallel",)), name="gmlp",
    )(ua, va, sgu_w, bias_full, _block_diag_mean(D_A, GROUP), ga)


EXP_UNDERFLOW = -104.0


def _attn_kernel(q_ref, k_ref, v_ref, mo_ref, a_ref, gb_ref, o_ref, *scratch):
    i = pl.program_id(1)
    n_pair = D_B // LANES
    lane = lax.broadcasted_iota(jnp.int32, (CHUNK, LANES), 1)
    row = lax.broadcasted_iota(jnp.int32, (CHUNK, LANES), 0)
    head_lanes = (lane < GROUP, lane >= GROUP)
    MO = mo_ref[...]
    scale = 1.0 / math.sqrt(GROUP)
    acc_refs, carry_refs, qs_refs = scratch[:n_pair], scratch[n_pair:2 * n_pair], scratch[2 * n_pair:]
    for p in range(n_pair):
        acc_refs[p][...] = jnp.zeros_like(acc_refs[p])
        carry_refs[p][...] = jnp.zeros_like(carry_refs[p])
        q = q_ref[:, p * LANES:(p + 1) * LANES]
        qs_refs[p][...] = jnp.concatenate([jnp.where(hl, q, jnp.zeros_like(q)) for hl in head_lanes], axis=0)

    def cond(state):
        jj, cmax = state
        return jnp.logical_and(jj <= i, cmax > EXP_UNDERFLOW)

    def body(state):
        jj, _ = state
        j = i - jj
        start = pl.multiple_of(j * CHUNK, CHUNK)
        mask1 = (j * CHUNK + lane) < (i * CHUNK + row)
        mask = jnp.concatenate([mask1, mask1], axis=0)
        cmax = jnp.full((2 * CHUNK, LANES), -jnp.inf, F32)
        slabs = range(n_pair)
        cols = [slice(p * LANES, (p + 1) * LANES) for p in slabs]
        zs = [lax.dot_general(qs_refs[p][...], k_ref[pl.ds(start, CHUNK), cols[p]], (((1,), (1,)), ((), ())),
                              preferred_element_type=F32) * scale for p in slabs]
        lszs = [jnp.minimum(z, 0.0) - jnp.log1p(jnp.exp(-jnp.abs(z))) for z in zs]
        trs = []
        for p in slabs:
            L = jnp.where(mask, lszs[p] - zs[p], 0.0)
            L_hi = L.astype(BF16)
            L_lo = (L - L_hi.astype(F32)).astype(BF16)
            tr = _dot(jnp.concatenate([L_hi, L_lo], axis=0), MO)
            trs.append(tr[:2 * CHUNK] + tr[2 * CHUNK:])
        for p in slabs:
            c = carry_refs[p][...]
            a = jnp.where(mask, jnp.exp(lszs[p] + trs[p][:, :LANES] + c), 0.0).astype(BF16)
            a_cat = jnp.concatenate([a[:CHUNK], a[CHUNK:]], axis=1)
            vs = v_ref[pl.ds(start, CHUNK), cols[p]]
            v_cat = jnp.concatenate([jnp.where(hl, vs, jnp.zeros_like(vs)) for hl in head_lanes], axis=0)
            acc_refs[p][...] += _dot(a_cat, v_cat)
            c = c + trs[p][:, LANES:]
            carry_refs[p][...] = c
            cmax = jnp.maximum(cmax, c)
        return jj + 1, jnp.max(cmax)

    lax.while_loop(cond, body, (jnp.int32(0), jnp.float32(0.0)))
    for p in range(n_pair):
        cols = slice(p * LANES, (p + 1) * LANES)
        acc = acc_refs[p][...]
        ms = _dot((acc * acc).astype(BF16), a_ref[...])
        o_ref[:, cols] = (acc * lax.rsqrt(ms + EPS) * gb_ref[:, cols]).astype(BF16)


def attn_call(q, k, v, gb, B, T):
    N = q.shape[0]
    nq = T // CHUNK
    i = np.arange(CHUNK)
    MO = jnp.asarray(np.concatenate([(i[:, None] > i[None, :]), np.ones((CHUNK, LANES), bool)], axis=1)
                     .astype(np.float32), BF16)
    kv = pl.BlockSpec((T, D_B), lambda b, i: (b, 0))
    qo = pl.BlockSpec((CHUNK, D_B), lambda b, i: (b * nq + i, 0))
    n_pair = D_B // LANES
    return pl.pallas_call(
        _attn_kernel, grid=(B, nq),
        in_specs=[qo, kv, kv, pl.BlockSpec((CHUNK, 2 * LANES), lambda b, i: (0, 0)),
                  pl.BlockSpec((CHUNK, LANES), lambda b, i: (0, 0)), pl.BlockSpec((1, D_B), lambda b, i: (0, 0))],
        out_specs=qo,
        out_shape=jax.ShapeDtypeStruct((N, D_B), BF16),
        scratch_shapes=([pltpu.VMEM((CHUNK, LANES), F32)] * n_pair + [pltpu.VMEM((2 * CHUNK, LANES), F32)] * n_pair
                        + [pltpu.VMEM((2 * CHUNK, LANES), BF16)] * n_pair),
        compiler_params=_cparams(("parallel", "arbitrary")), name="stickbreak_attn",
    )(q, k, v, MO, _block_diag_mean(LANES, GROUP), gb)


def _out_kernel(ya_ref, yb_ref, x_ref, w_ref, gt_ref, g2_ref, sc_ref, sh_ref, xo_ref, h2_ref):
    o = _dot(ya_ref[...], w_ref[:D_A, :]) + _dot(yb_ref[...], w_ref[D_A:, :])
    x = x_ref[...] + gt_ref[0] * o
    xo_ref[...] = x
    h2_ref[...] = _rms_mod(x, g2_ref[...], sc_ref[0], sh_ref[0])


def out_call(ya, yb, x, w_out_bf, mod, g2, T, tm=512, x_first_tok=0):
    N = ya.shape[0]
    row = pl.BlockSpec((tm, D_MODEL), lambda i: (i, 0))
    x_row = pl.BlockSpec((tm, D_MODEL), lambda i: (x_first_tok // tm + i, 0))
    half = pl.BlockSpec((tm, D_A), lambda i: (i, 0))
    return pl.pallas_call(
        _out_kernel, grid=(N // tm,),
        in_specs=[half, half, x_row, pl.BlockSpec(w_out_bf.shape, lambda i: (0, 0)), _mod_spec(tm, T, 2),
                  pl.BlockSpec((1, D_MODEL), lambda i: (0, 0)), _mod_spec(tm, T, 4), _mod_spec(tm, T, 3)],
        out_specs=[row, row], out_shape=[jax.ShapeDtypeStruct((N, D_MODEL), F32)] * 2,
        compiler_params=_cparams(("parallel",)), name="out_proj",
    )(ya, yb, x, w_out_bf, mod, g2, mod, mod)


N_CAND_ROWS = -(-N_STAIR // SUBLANES) * SUBLANES


def _topk_cols(s, payload=None):
    n_rows, tb = s.shape
    row_f = lax.broadcasted_iota(jnp.int32, (n_rows, tb), 0).astype(F32)
    rank = lax.broadcasted_iota(jnp.int32, (PEER_TOPK, tb), 0)
    neg = jnp.float32(-jnp.inf)

    def body(r, carry):
        s, vals, second = carry
        m = jnp.max(s, axis=0, keepdims=True)
        pos = jnp.min(jnp.where(s == m, row_f, float(n_rows)), axis=0, keepdims=True)
        hit = row_f == pos
        out = pos if payload is None else jnp.max(jnp.where(hit, payload, -1.0), axis=0, keepdims=True)
        dst = rank == r
        return jnp.where(hit, neg, s), jnp.where(dst, m, vals), jnp.where(dst, out, second)

    zeros = jnp.zeros((PEER_TOPK, tb), F32)
    _, vals, second = lax.fori_loop(0, PEER_TOPK, body, (s, zeros, zeros))
    return vals, second


def _peerq_kernel(h2_ref, wq_ref, k1_ref, k2_ref, sel_ref, sele_ref, idx_ref, g_ref, q_scr, it_scr, gt_scr):
    h = pl.program_id(1)
    tm = h2_ref.shape[0]

    @pl.when(h == 0)
    def _():
        q_scr[...] = _dot(h2_ref[...].astype(BF16), wq_ref[...]).astype(BF16)

    nt = (((1,), (1,)), ((), ()))
    vals, keys = [], []
    for tb in range(tm // LANES):
        qh = q_scr[tb * LANES:(tb + 1) * LANES, pl.ds(pl.multiple_of(h * LANES, LANES), LANES)]
        s12 = jnp.concatenate([lax.dot_general(k1_ref[...], qh, nt, preferred_element_type=F32),
                               lax.dot_general(k2_ref[...], qh, nt, preferred_element_type=F32)], axis=1)
        v12, i12 = _topk_cols(s12)
        vals.append(jnp.concatenate([v12[:, :LANES], v12[:, LANES:]], axis=0))
        keys.append(jnp.concatenate([i12[:, :LANES], i12[:, LANES:]], axis=0))
    vals = jnp.concatenate(vals, axis=1)
    keys = jnp.concatenate(keys, axis=1)
    cand = jnp.dot(sel_ref[...], vals, precision=HIGHEST, preferred_element_type=F32)
    cand_row = lax.broadcasted_iota(jnp.int32, cand.shape, 0)
    cand = jnp.where(cand_row < N_STAIR, cand, -jnp.inf)
    ecand = jnp.dot(sele_ref[...], keys, precision=HIGHEST, preferred_element_type=F32)
    sc, ex = _topk_cols(cand, ecand)
    p = jnp.exp(sc - sc[0:1, :])
    out_rows = pl.ds(pl.multiple_of(h * PEER_TOPK, PEER_TOPK), PEER_TOPK)
    it_scr[out_rows, :] = ex
    gt_scr[out_rows, :] = p / jnp.sum(p, axis=0, keepdims=True)

    @pl.when(h == PEER_HEADS - 1)
    def _():
        idx_ref[...] = it_scr[...].T.astype(jnp.int32)
        g_ref[...] = gt_scr[...].T


def peerq_call(h2, wq_bf, k1, k2, tm=256):
    N = h2.shape[0]
    half = N_KEYS // 2
    k1p = jnp.concatenate([k1, jnp.zeros((N_KEYS, half), F32)], axis=1).astype(BF16)
    k2p = jnp.concatenate([jnp.zeros((N_KEYS, half), F32), k2], axis=1).astype(BF16)
    sel = np.zeros((N_CAND_ROWS, 2 * PEER_TOPK), np.float32)
    sele = np.zeros((N_CAND_ROWS, 2 * PEER_TOPK), np.float32)
    for c, (a, b) in enumerate(_STAIR):
        sel[c, a] = sel[c, PEER_TOPK + b] = 1.0
        sele[c, a] = float(N_KEYS)
        sele[c, PEER_TOPK + b] = 1.0
    const = lambda shape: pl.BlockSpec(shape, lambda i, h: (0, 0))
    row = pl.BlockSpec((tm, HK), lambda i, h: (i, 0))
    return pl.pallas_call(
        _peerq_kernel, grid=(N // tm, PEER_HEADS),
        in_specs=[pl.BlockSpec((tm, D_MODEL), lambda i, h: (i, 0)), const(wq_bf.shape), const((N_KEYS, LANES)),
                  const((N_KEYS, LANES)), const(sel.shape), const(sele.shape)],
        out_specs=[row, row],
        out_shape=[jax.ShapeDtypeStruct((N, HK), jnp.int32), jax.ShapeDtypeStruct((N, HK), F32)],
        scratch_shapes=[pltpu.VMEM((tm, D_MODEL), BF16), pltpu.VMEM((HK, tm), F32), pltpu.VMEM((HK, tm), F32)],
        compiler_params=_cparams(("parallel", "arbitrary")), name="peer_retrieve",
    )(h2, wq_bf, k1p, k2p, jnp.asarray(sel), jnp.asarray(sele))


SC_LANES = 16
SC_CORES = 2
SC_WORKERS = 32
ROWS_PER_GATHER = 16
N_ROW_BUF = 4
LOOKAHEAD = N_ROW_BUF - 1
TOK_BLOCK = 16
ROW_BLOCK = 8
U_UNROLL = 2
COL_BLOCK = 16
N_DCHUNK = D_MODEL // SC_LANES
N_GATHER = HK // ROWS_PER_GATHER


def _sc_gelu(x):
    y = math.sqrt(2.0 / math.pi) * (x + 0.044715 * (x * x * x))
    t = 1.0 - 2.0 / (jnp.exp(2.0 * y) + 1.0)
    return 0.5 * x * (1.0 + t)


def _peer_eval_body(per_w, h_hbm, idx_hbm, g_hbm, u_hbm, v_hbm, o_hbm,
                    idx_blk, g_blk, h_blk, out_blk, w_v, *bufs):
    wid = lax.axis_index("s") * SC_CORES + lax.axis_index("c")
    rows = bufs[:N_ROW_BUF]
    sems = bufs[N_ROW_BUF:]
    lane = lax.iota(jnp.int32, SC_LANES)
    zeros = jnp.zeros((SC_LANES,), F32)

    def gather(table, i, j, buf):
        return pltpu.make_async_copy(table.at[idx_blk.at[i, j]], rows[buf], sems[buf])

    def u_compute(i, n, rbuf):
        for half in range(ROWS_PER_GATHER // SC_LANES):
            actv = zeros
            for rg in range(SC_LANES // ROW_BLOCK):
                r0 = half * SC_LANES + rg * ROW_BLOCK

                def c_body(cb, accs):
                    accs = list(accs)
                    for cc in range(U_UNROLL):
                        sl = pl.ds(pl.multiple_of((cb * U_UNROLL + cc) * SC_LANES, SC_LANES), SC_LANES)
                        hc = h_blk[i, sl]
                        for r in range(ROW_BLOCK):
                            accs[r] = accs[r] + rbuf[r0 + r, sl] * hc
                    return tuple(accs)

                accs = lax.fori_loop(0, N_DCHUNK // U_UNROLL, c_body, (zeros,) * ROW_BLOCK)
                for r in range(ROW_BLOCK):
                    actv = jnp.where(lane == rg * ROW_BLOCK + r, jnp.sum(accs[r]), actv)
            sl = pl.ds(n * ROWS_PER_GATHER + half * SC_LANES, SC_LANES)
            w_v[sl] = g_blk[i, sl] * _sc_gelu(actv)

    def v_compute(i, m, rbuf):
        for cb in range(N_DCHUNK // COL_BLOCK):
            def r_body(r, accs):
                wk = plsc.load_gather(w_v, [jnp.full((SC_LANES,), m * ROWS_PER_GATHER, jnp.int32) + r])
                return tuple(accs[cc] + wk * rbuf[r, pl.ds((cb * COL_BLOCK + cc) * SC_LANES, SC_LANES)]
                             for cc in range(COL_BLOCK))

            accs = lax.fori_loop(0, ROWS_PER_GATHER, r_body, (zeros,) * COL_BLOCK)
            for cc in range(COL_BLOCK):
                sl = pl.ds((cb * COL_BLOCK + cc) * SC_LANES, SC_LANES)
                if m == 0:
                    out_blk[i, sl] = accs[cc]
                else:
                    plsc.addupdate(out_blk.at[i, sl], accs[cc])

    @pl.loop(0, per_w // TOK_BLOCK)
    def _(bi):
        t0 = wid * per_w + bi * TOK_BLOCK
        pltpu.sync_copy(idx_hbm.at[pl.ds(t0, TOK_BLOCK)], idx_blk)
        pltpu.sync_copy(g_hbm.at[pl.ds(t0, TOK_BLOCK)], g_blk)
        pltpu.sync_copy(h_hbm.at[pl.ds(t0, TOK_BLOCK)], h_blk)
        def task_gather(i, n):
            table = u_hbm if n < N_GATHER else v_hbm
            return gather(table, i, n % N_GATHER, n % N_ROW_BUF)

        for n in range(LOOKAHEAD):
            task_gather(0, n).start()

        @pl.loop(0, TOK_BLOCK)
        def _(i):
            for n in range(2 * N_GATHER):
                ahead = n + LOOKAHEAD
                if ahead < 2 * N_GATHER:
                    task_gather(i, ahead).start()
                else:
                    @pl.when(i + 1 < TOK_BLOCK)
                    def _():
                        task_gather(i + 1, ahead - 2 * N_GATHER).start()
                task_gather(i, n).wait()
                if n < N_GATHER:
                    u_compute(i, n, rows[n % N_ROW_BUF])
                else:
                    v_compute(i, n - N_GATHER, rows[n % N_ROW_BUF])

        pltpu.sync_copy(out_blk, o_hbm.at[pl.ds(t0, TOK_BLOCK)])


def peer_eval_call(h2, idx, g, u_tab, v_tab, n_tok=None, table_row0=0):
    N = h2.shape[0] if n_tok is None else n_tok
    per_w = N // SC_WORKERS
    assert N % (SC_WORKERS * TOK_BLOCK) == 0
    assert (2 * N_GATHER) % N_ROW_BUF == 0 and LOOKAHEAD < N_ROW_BUF
    mesh = plsc.VectorSubcoreMesh(core_axis_name="c", subcore_axis_name="s",
                                  num_cores=SC_CORES, num_subcores=SC_WORKERS // SC_CORES)
    return pl.kernel(
        functools.partial(_peer_eval_body, per_w),
        out_type=jax.ShapeDtypeStruct((N, D_MODEL), F32),
        mesh=mesh,
        scratch_types=[
            pltpu.VMEM((TOK_BLOCK, N_GATHER, ROWS_PER_GATHER), jnp.int32),
            pltpu.VMEM((TOK_BLOCK, HK), F32),
            pltpu.VMEM((TOK_BLOCK, D_MODEL), F32),
            pltpu.VMEM((TOK_BLOCK, D_MODEL), F32),
            pltpu.VMEM((HK,), F32),
        ] + [pltpu.VMEM((ROWS_PER_GATHER, D_MODEL), F32)] * N_ROW_BUF + [pltpu.SemaphoreType.DMA] * N_ROW_BUF,
        compiler_params=pltpu.CompilerParams(needs_layout_passes=False),
        name="peer_eval",
    )(h2, (idx + table_row0).reshape(h2.shape[0], N_GATHER, ROWS_PER_GATHER), g, u_tab, v_tab)


N_EXPERTS = N_KEYS * N_KEYS
E_HALF = N_EXPERTS // 2
TC_TOK_TILE = 16
TC_EVAL_SHARE = ((0.1875, 0.1875, 0.1875, 0.25), (0.25, 0.25, 0.3125, 0.3125))
COMPILER_SCRATCH_BYTES = 2 * 1024 * 1024


def _tc_eval_kernel(table_row0, idx_s, hi_ref, g_ref, h_ref, u_hbm, v_hbm, o_ref,
                    tab, actw, outp, rsum, wb_all, stage, sem):
    p = pl.program_id(0)
    i = pl.program_id(1)
    for ph, (src, half) in enumerate(((u_hbm, 0), (u_hbm, 1), (v_hbm, 0), (v_hbm, 1))):
        @pl.when(jnp.logical_and(p == ph, i == 0))
        def _():
            cp = pltpu.make_async_copy(src.at[pl.ds(table_row0 + half * E_HALF, E_HALF)], tab, sem)
            cp.start()
            cp.wait()

    lane = lax.broadcasted_iota(jnp.int32, (HK, LANES), 1)
    row = lax.broadcasted_iota(jnp.int32, (HK, LANES), 0)
    in_half = hi_ref[0] == (p % 2)

    @pl.when(p < 2)
    def _():
        acts = jnp.zeros((HK, LANES), F32)
        for t in range(TC_TOK_TILE):
            h = h_ref[t]
            for k in range(HK):
                rsum[t, pl.ds(k, 1), :] = jnp.sum(tab[idx_s[t, k]] * h, axis=0, keepdims=True)
            acts = jnp.where(lane == t, jnp.sum(rsum[t], axis=1, keepdims=True), acts)
        part = jnp.where(in_half, acts, 0.0)

        @pl.when(p == 0)
        def _():
            actw[i] = part

        @pl.when(p == 1)
        def _():
            actw[i] = g_ref[0] * jax.nn.gelu(actw[i] + part)

    @pl.when(p >= 2)
    def _():
        wm = jnp.where(in_half, actw[i], 0.0)

        for t in range(TC_TOK_TILE):
            wb_all[t] = jnp.broadcast_to(wm[:, t:t + 1], (HK, LANES))

        for t in range(TC_TOK_TILE):
            accs = [jnp.zeros((SUBLANES, LANES), F32)] * 4
            for k in range(HK):
                w = jnp.broadcast_to(wb_all[t, pl.ds(k, 1), :], (SUBLANES, LANES))
                accs[k % 4] = accs[k % 4] + w * tab[idx_s[t, k]]
            stage[t] = (accs[0] + accs[1]) + (accs[2] + accs[3])
        tile_toks = pl.ds(pl.multiple_of(i * TC_TOK_TILE, TC_TOK_TILE), TC_TOK_TILE)

        @pl.when(p == 2)
        def _():
            outp[tile_toks] = stage[...]

        @pl.when(p == 3)
        def _():
            o_ref[...] = outp[tile_toks] + stage[...]


def peer_eval_tc_call(h2, idx, g, u_tab, v_tab, first_tok=0, table_row0=0):
    n_all = h2.shape[0]
    nt = n_all - first_tok
    n_tiles = nt // TC_TOK_TILE
    first_tile = first_tok // TC_TOK_TILE
    n_rows = u_tab.shape[0]
    assert nt % TC_TOK_TILE == 0 and first_tok % TC_TOK_TILE == 0 and table_row0 + N_EXPERTS <= n_rows
    idx, g = idx[first_tok:], g[first_tok:]

    def pairs_on_sublanes(a, fill):
        a = a.reshape(n_tiles, TC_TOK_TILE, HK).transpose(0, 2, 1)
        return jnp.pad(a, ((0, 0), (0, 0), (0, LANES - TC_TOK_TILE)), constant_values=fill)

    tile3 = lambda: pl.BlockSpec((1, HK, LANES), lambda p, i: (i, 0, 0))
    scratch_shapes = [(E_HALF, SUBLANES, LANES), (n_tiles, HK, LANES), (nt, SUBLANES, LANES),
                      (TC_TOK_TILE, HK, LANES), (TC_TOK_TILE, HK, LANES), (TC_TOK_TILE, SUBLANES, LANES)]
    block_shapes = [(HK, LANES), (HK, LANES), (TC_TOK_TILE, SUBLANES, LANES), (TC_TOK_TILE, SUBLANES, LANES)]
    f32_bytes = 4
    vmem_bytes = (sum(int(np.prod(s)) for s in scratch_shapes) + 2 * sum(int(np.prod(s)) for s in block_shapes)
                  ) * f32_bytes + COMPILER_SCRATCH_BYTES
    assert vmem_bytes <= V7X_VMEM_BYTES
    out = pl.pallas_call(
        functools.partial(_tc_eval_kernel, table_row0), grid=(4, n_tiles),
        in_specs=[pl.BlockSpec((TC_TOK_TILE, HK), lambda p, i: (i, 0), memory_space=pltpu.SMEM),
                  tile3(), tile3(),
                  pl.BlockSpec((TC_TOK_TILE, SUBLANES, LANES), lambda p, i: (first_tile + i, 0, 0)),
                  pl.BlockSpec(memory_space=pl.ANY), pl.BlockSpec(memory_space=pl.ANY)],
        out_specs=pl.BlockSpec((TC_TOK_TILE, SUBLANES, LANES), lambda p, i: (jnp.where(p == 3, i, 0), 0, 0)),
        out_shape=jax.ShapeDtypeStruct((nt, SUBLANES, LANES), F32),
        scratch_shapes=[pltpu.VMEM(s, F32) for s in scratch_shapes] + [pltpu.SemaphoreType.DMA],
        compiler_params=pltpu.CompilerParams(dimension_semantics=("arbitrary", "arbitrary"),
                                             vmem_limit_bytes=vmem_bytes),
        name="peer_eval_tc",
    )(idx & (E_HALF - 1), pairs_on_sublanes(idx // E_HALF, 2), pairs_on_sublanes(g, 0.0),
      h2.reshape(n_all, SUBLANES, LANES),
      u_tab.reshape(n_rows, SUBLANES, LANES), v_tab.reshape(n_rows, SUBLANES, LANES))
    return out.reshape(nt, D_MODEL)


def _final_kernel(x_ref, y_ref, gt_ref, g_ref, o_ref):
    x = x_ref[...] + gt_ref[0] * y_ref[...]
    ms = jnp.mean(x * x, axis=-1, keepdims=True)
    o_ref[...] = x * lax.rsqrt(ms + EPS) * g_ref[...]


def final_call(x, y, mod, gf, T, tm=512):
    N = x.shape[0]
    row = pl.BlockSpec((tm, D_MODEL), lambda i: (i, 0))
    return pl.pallas_call(
        _final_kernel, grid=(N // tm,),
        in_specs=[row, row, _mod_spec(tm, T, 5), pl.BlockSpec((1, D_MODEL), lambda i: (0, 0))],
        out_specs=row, out_shape=jax.ShapeDtypeStruct((N, D_MODEL), F32),
        compiler_params=_cparams(("parallel",)), name="final_norm",
    )(x, y, mod, gf)


def kernel(x, c, ada_w, ada_b, norm1_g, norm2_g, w_in, sgu_w, sgu_b, out_norm_a, out_norm_b, w_out,
           peer_wq, peer_k1, peer_k2, peer_u, peer_v, final_g):
    B, T, Dm = x.shape
    L = ada_w.shape[0]
    N = B * T
    mods = ada_mod(c, ada_w, ada_b).reshape(L, B * 6, 1, Dm)
    assert sum(GROUP_BATCHES) == B and len(TC_EVAL_SHARE) == L
    assert all(len(s) == len(GROUP_BATCHES) for s in TC_EVAL_SHARE)
    n_groups = len(GROUP_BATCHES)
    first_batch = [sum(GROUP_BATCHES[:g]) for g in range(n_groups)]
    gmods = [[mods[l, first_batch[g] * 6:(first_batch[g] + GROUP_BATCHES[g]) * 6] for g in range(n_groups)]
             for l in range(L)]
    xs = [x.reshape(N, Dm)] * n_groups
    ys = [None] * n_groups
    n_exp = peer_u.shape[1]
    u_all = peer_u.reshape(L * n_exp, Dm)
    v_all = peer_v.reshape(L * n_exp, Dm)
    sc_step = SC_WORKERS * TOK_BLOCK
    for l in range(L):
        w_in_bf, w_out_bf, wq_bf = w_in[l].astype(BF16), w_out[l].astype(BF16), peer_wq[l].astype(BF16)
        bias_full = jnp.repeat(sgu_b[l].T, GROUP, axis=1)
        for g in range(n_groups):
            Bg = GROUP_BATCHES[g]
            Ng = Bg * T
            mod = gmods[l][g]
            x_first = first_batch[g] * T if l == 0 else 0
            outs = proj_call(xs[g], ys[g], gmods[l - 1][g] if l else None, mod, norm1_g[l].reshape(1, Dm), w_in_bf, T,
                             n_tok=Ng, x_first_tok=x_first)
            if l:
                xs[g], outs = outs[0], outs[1:]
            ua, va, q, k, v = outs
            ya = gmlp_call(ua, va, sgu_w[l], bias_full, out_norm_a[l].reshape(1, D_A))
            yb = attn_call(q, k, v, out_norm_b[l].reshape(1, D_B), Bg, T)
            xs[g], h2 = out_call(ya, yb, xs[g], w_out_bf, mod, norm2_g[l].reshape(1, Dm), T, x_first_tok=x_first)
            idx, gate = peerq_call(h2, wq_bf, peer_k1[l], peer_k2[l])
            n_sc = int(Ng * (1.0 - TC_EVAL_SHARE[l][g])) // sc_step * sc_step
            y_sc = peer_eval_call(h2, idx, gate, u_all, v_all, n_tok=n_sc, table_row0=l * n_exp)
            y_tc = peer_eval_tc_call(h2, idx, gate, u_all, v_all, first_tok=n_sc, table_row0=l * n_exp)
            ys[g] = jnp.concatenate([y_sc, y_tc], axis=0)
    outs = [final_call(xs[g], ys[g], gmods[L - 1][g], final_g.reshape(1, Dm), T) for g in range(n_groups)]
    return jnp.concatenate(outs, axis=0).reshape(B, T, Dm)
```

```python
import functools
import math

import numpy as np
import jax
import jax.numpy as jnp
from jax import lax
from jax.experimental import pallas as pl
from jax.experimental.pallas import tpu as pltpu
from jax.experimental.pallas import tpu_sc as plsc

F32 = jnp.float32
BF16 = jnp.bfloat16
HIGHEST = lax.Precision.HIGHEST

D_MODEL = 1024
D_A = 512
D_B = 512
GROUP = 64
CHUNK = 128
N_KEYS = 128
PEER_HEADS = 8
PEER_TOPK = 16
HK = PEER_HEADS * PEER_TOPK
EPS = 1e-6
LANES = 128
SUBLANES = 8
V7X_VMEM_BYTES = 64 * 1024 * 1024
VMEM_LIMIT = V7X_VMEM_BYTES * 3 // 4
GROUP_BATCHES = (4, 4, 4, 4)

_STAIR = [(a, b) for a in range(PEER_TOPK) for b in range(PEER_TOPK) if (a + 1) * (b + 1) <= PEER_TOPK]
N_STAIR = len(_STAIR)


def _cparams(sem):
    return pltpu.CompilerParams(dimension_semantics=sem, vmem_limit_bytes=VMEM_LIMIT)


def _dot(a, b):
    return jnp.dot(a, b, preferred_element_type=F32)


def _block_diag_mean(n, group):
    i = np.arange(n)
    return jnp.asarray((i[:, None] // group == i[None, :] // group).astype(np.float32) / group, BF16)


def _ada_kernel(c_ref, w_ref, b_ref, o_ref):
    c = c_ref[...]
    ca = c * jax.nn.sigmoid(c)
    o_ref[0] = jnp.dot(ca, w_ref[0], precision=HIGHEST, preferred_element_type=F32) + b_ref[0]


def ada_mod(c, ada_w, ada_b):
    L, Dm, E = ada_w.shape
    Bc = c.shape[0]
    tn = 1536
    return pl.pallas_call(
        _ada_kernel,
        grid=(L, E // tn),
        in_specs=[pl.BlockSpec((Bc, Dm), lambda l, j: (0, 0)),
                  pl.BlockSpec((1, Dm, tn), lambda l, j: (l, 0, j)),
                  pl.BlockSpec((1, 1, tn), lambda l, j: (l, 0, j))],
        out_specs=pl.BlockSpec((1, Bc, tn), lambda l, j: (l, 0, j)),
        out_shape=jax.ShapeDtypeStruct((L, Bc, E), F32),
        compiler_params=_cparams(("parallel", "parallel")),
        name="ada_mod",
    )(c, ada_w, ada_b.reshape(L, 1, E))


def _mod_spec(tm, T, j):
    return pl.BlockSpec((1, 1, D_MODEL), lambda i: (((i * tm) // T) * 6 + j, 0, 0))


def _rms_mod(x, g, sc, sh):
    ms = jnp.mean(x * x, axis=-1, keepdims=True)
    return (x * lax.rsqrt(ms + EPS) * g) * (1.0 + sc) + sh


def _proj_kernel(has_y, *refs):
    if has_y:
        x_ref, y_ref, gt_ref, g_ref, sc_ref, sh_ref, w_ref, xo_ref, ua_ref, va_ref, q_ref, k_ref, v_ref = refs
        x = x_ref[...] + gt_ref[0] * y_ref[...]
        xo_ref[...] = x
    else:
        x_ref, g_ref, sc_ref, sh_ref, w_ref, ua_ref, va_ref, q_ref, k_ref, v_ref = refs
        x = x_ref[...]
    h = _rms_mod(x, g_ref[...], sc_ref[0], sh_ref[0]).astype(BF16)
    outs = (ua_ref, va_ref, q_ref, k_ref, v_ref)
    for j, o_ref in enumerate(outs):
        p = _dot(h, w_ref[:, j * D_A:(j + 1) * D_A])
        if j < 2:
            p = jax.nn.gelu(p)
        o_ref[...] = p.astype(o_ref.dtype)


def proj_call(x, y, mod_prev, mod, g1, w_in_bf, T, tm=512, n_tok=None, x_first_tok=0):
    N = x.shape[0] if n_tok is None else n_tok
    has_y = y is not None
    row = pl.BlockSpec((tm, D_MODEL), lambda i: (i, 0))
    half = pl.BlockSpec((tm, D_A), lambda i: (i, 0))
    in_specs = [pl.BlockSpec((tm, D_MODEL), lambda i: (x_first_tok // tm + i, 0))]
    args = [x]
    if has_y:
        in_specs += [row, _mod_spec(tm, T, 5)]
        args += [y, mod_prev]
    in_specs += [pl.BlockSpec((1, D_MODEL), lambda i: (0, 0)), _mod_spec(tm, T, 1), _mod_spec(tm, T, 0),
                 pl.BlockSpec(w_in_bf.shape, lambda i: (0, 0))]
    args += [g1, mod, mod, w_in_bf]
    out_specs = [half] * 5
    out_shape = [jax.ShapeDtypeStruct((N, D_A), F32)] * 2 + [jax.ShapeDtypeStruct((N, D_A), BF16)] * 3
    if has_y:
        out_specs = [row] + out_specs
        out_shape = [jax.ShapeDtypeStruct((N, D_MODEL), F32)] + out_shape
    return pl.pallas_call(
        functools.partial(_proj_kernel, has_y),
        grid=(N // tm,), in_specs=in_specs, out_specs=out_specs, out_shape=out_shape,
        compiler_params=_cparams(("parallel",)), name="proj",
    )(*args)


def _gmlp_kernel(ua_ref, va_ref, w_ref, bias_ref, a_ref, ga_ref, o_ref):
    A = a_ref[...]
    row = lax.broadcasted_iota(jnp.int32, (CHUNK, CHUNK), 0)
    col = lax.broadcasted_iota(jnp.int32, (CHUNK, CHUNK), 1)
    causal = row >= col
    first_group = col < GROUP
    ws = [jnp.where(causal, w_ref[g], 0.0).astype(BF16) for g in range(D_A // GROUP)]
    bias = bias_ref[...]
    ga = ga_ref[...]
    for c in range(ua_ref.shape[0] // CHUNK):
        rows = slice(c * CHUNK, (c + 1) * CHUNK)
        v = va_ref[rows, :]
        v_hi = v.astype(BF16)
        v_lo = (v - v_hi.astype(F32)).astype(BF16)
        d = v - (_dot(v_hi, A) + _dot(v_lo, A))
        var = _dot((d * d).astype(BF16), A)
        vn = (d * lax.rsqrt(var + EPS)).astype(BF16)
        parts = []
        for p in range(D_A // LANES):
            vp = vn[:, p * LANES:(p + 1) * LANES]
            parts.append(jnp.where(first_group, _dot(ws[2 * p], vp), _dot(ws[2 * p + 1], vp)))
        s = jnp.concatenate(parts, axis=1) + bias
        y = ua_ref[rows, :] * s
        ms = _dot((y * y).astype(BF16), A)
        o_ref[rows, :] = (y * lax.rsqrt(ms + EPS) * ga).astype(BF16)


def gmlp_call(ua, va, sgu_w, bias_full, ga, tm=512):
    N = ua.shape[0]
    half = pl.BlockSpec((tm, D_A), lambda i: (i, 0))
    return pl.pallas_call(
        _gmlp_kernel, grid=(N // tm,),
        in_specs=[half, half,
                  pl.BlockSpec(sgu_w.shape, lambda i: (0, 0, 0)),
                  pl.BlockSpec((CHUNK, D_A), lambda i: (0, 0)),
                  pl.BlockSpec((D_A, D_A), lambda i: (0, 0)),
                  pl.BlockSpec((1, D_A), lambda i: (0, 0))],
        out_specs=half, out_shape=jax.ShapeDtypeStruct((N, D_A), BF16),
        compiler_params=_cparams(("parallel",)), name="gmlp",
    )(ua, va, sgu_w, bias_full, _block_diag_mean(D_A, GROUP), ga)


EXP_UNDERFLOW = -104.0


def _attn_kernel(q_ref, k_ref, v_ref, mo_ref, a_ref, gb_ref, o_ref, *scratch):
    i = pl.program_id(1)
    n_pair = D_B // LANES
    lane = lax.broadcasted_iota(jnp.int32, (CHUNK, LANES), 1)
    row = lax.broadcasted_iota(jnp.int32, (CHUNK, LANES), 0)
    head_lanes = (lane < GROUP, lane >= GROUP)
    MO = mo_ref[...]
    scale = 1.0 / math.sqrt(GROUP)
    acc_refs, carry_refs, qs_refs = scratch[:n_pair], scratch[n_pair:2 * n_pair], scratch[2 * n_pair:]
    for p in range(n_pair):
        acc_refs[p][...] = jnp.zeros_like(acc_refs[p])
        carry_refs[p][...] = jnp.zeros_like(carry_refs[p])
        q = q_ref[:, p * LANES:(p + 1) * LANES]
        qs_refs[p][...] = jnp.concatenate([jnp.where(hl, q, jnp.zeros_like(q)) for hl in head_lanes], axis=0)

    def cond(state):
        jj, cmax = state
        return jnp.logical_and(jj <= i, cmax > EXP_UNDERFLOW)

    def body(state):
        jj, _ = state
        j = i - jj
        start = pl.multiple_of(j * CHUNK, CHUNK)
        mask1 = (j * CHUNK + lane) < (i * CHUNK + row)
        mask = jnp.concatenate([mask1, mask1], axis=0)
        cmax = jnp.full((2 * CHUNK, LANES), -jnp.inf, F32)
        slabs = range(n_pair)
        cols = [slice(p * LANES, (p + 1) * LANES) for p in slabs]
        zs = [lax.dot_general(qs_refs[p][...], k_ref[pl.ds(start, CHUNK), cols[p]], (((1,), (1,)), ((), ())),
                              preferred_element_type=F32) * scale for p in slabs]
        lszs = [jnp.minimum(z, 0.0) - jnp.log1p(jnp.exp(-jnp.abs(z))) for z in zs]
        trs = []
        for p in slabs:
            L = jnp.where(mask, lszs[p] - zs[p], 0.0)
            L_hi = L.astype(BF16)
            L_lo = (L - L_hi.astype(F32)).astype(BF16)
            tr = _dot(jnp.concatenate([L_hi, L_lo], axis=0), MO)
            trs.append(tr[:2 * CHUNK] + tr[2 * CHUNK:])
        for p in slabs:
            c = carry_refs[p][...]
            a = jnp.where(mask, jnp.exp(lszs[p] + trs[p][:, :LANES] + c), 0.0).astype(BF16)
            a_cat = jnp.concatenate([a[:CHUNK], a[CHUNK:]], axis=1)
            vs = v_ref[pl.ds(start, CHUNK), cols[p]]
            v_cat = jnp.concatenate([jnp.where(hl, vs, jnp.zeros_like(vs)) for hl in head_lanes], axis=0)
            acc_refs[p][...] += _dot(a_cat, v_cat)
            c = c + trs[p][:, LANES:]
            carry_refs[p][...] = c
            cmax = jnp.maximum(cmax, c)
        return jj + 1, jnp.max(cmax)

    lax.while_loop(cond, body, (jnp.int32(0), jnp.float32(0.0)))
    for p in range(n_pair):
        cols = slice(p * LANES, (p + 1) * LANES)
        acc = acc_refs[p][...]
        ms = _dot((acc * acc).astype(BF16), a_ref[...])
        o_ref[:, cols] = (acc * lax.rsqrt(ms + EPS) * gb_ref[:, cols]).astype(BF16)


def attn_call(q, k, v, gb, B, T):
    N = q.shape[0]
    nq = T // CHUNK
    i = np.arange(CHUNK)
    MO = jnp.asarray(np.concatenate([(i[:, None] > i[None, :]), np.ones((CHUNK, LANES), bool)], axis=1)
                     .astype(np.float32), BF16)
    kv = pl.BlockSpec((T, D_B), lambda b, i: (b, 0))
    qo = pl.BlockSpec((CHUNK, D_B), lambda b, i: (b * nq + i, 0))
    n_pair = D_B // LANES
    return pl.pallas_call(
        _attn_kernel, grid=(B, nq),
        in_specs=[qo, kv, kv, pl.BlockSpec((CHUNK, 2 * LANES), lambda b, i: (0, 0)),
                  pl.BlockSpec((CHUNK, LANES), lambda b, i: (0, 0)), pl.BlockSpec((1, D_B), lambda b, i: (0, 0))],
        out_specs=qo,
        out_shape=jax.ShapeDtypeStruct((N, D_B), BF16),
        scratch_shapes=([pltpu.VMEM((CHUNK, LANES), F32)] * n_pair + [pltpu.VMEM((2 * CHUNK, LANES), F32)] * n_pair
                        + [pltpu.VMEM((2 * CHUNK, LANES), BF16)] * n_pair),
        compiler_params=_cparams(("parallel", "arbitrary")), name="stickbreak_attn",
    )(q, k, v, MO, _block_diag_mean(LANES, GROUP), gb)


def _out_kernel(ya_ref, yb_ref, x_ref, w_ref, gt_ref, g2_ref, sc_ref, sh_ref, xo_ref, h2_ref):
    o = _dot(ya_ref[...], w_ref[:D_A, :]) + _dot(yb_ref[...], w_ref[D_A:, :])
    x = x_ref[...] + gt_ref[0] * o
    xo_ref[...] = x
    h2_ref[...] = _rms_mod(x, g2_ref[...], sc_ref[0], sh_ref[0])


def out_call(ya, yb, x, w_out_bf, mod, g2, T, tm=512, x_first_tok=0):
    N = ya.shape[0]
    row = pl.BlockSpec((tm, D_MODEL), lambda i: (i, 0))
    x_row = pl.BlockSpec((tm, D_MODEL), lambda i: (x_first_tok // tm + i, 0))
    half = pl.BlockSpec((tm, D_A), lambda i: (i, 0))
    return pl.pallas_call(
        _out_kernel, grid=(N // tm,),
        in_specs=[half, half, x_row, pl.BlockSpec(w_out_bf.shape, lambda i: (0, 0)), _mod_spec(tm, T, 2),
                  pl.BlockSpec((1, D_MODEL), lambda i: (0, 0)), _mod_spec(tm, T, 4), _mod_spec(tm, T, 3)],
        out_specs=[row, row], out_shape=[jax.ShapeDtypeStruct((N, D_MODEL), F32)] * 2,
        compiler_params=_cparams(("parallel",)), name="out_proj",
    )(ya, yb, x, w_out_bf, mod, g2, mod, mod)


N_CAND_ROWS = -(-N_STAIR // SUBLANES) * SUBLANES


def _topk_cols(s, payload=None):
    n_rows, tb = s.shape
    row_f = lax.broadcasted_iota(jnp.int32, (n_rows, tb), 0).astype(F32)
    rank = lax.broadcasted_iota(jnp.int32, (PEER_TOPK, tb), 0)
    neg = jnp.float32(-jnp.inf)

    def body(r, carry):
        s, vals, second = carry
        m = jnp.max(s, axis=0, keepdims=True)
        pos = jnp.min(jnp.where(s == m, row_f, float(n_rows)), axis=0, keepdims=True)
        hit = row_f == pos
        out = pos if payload is None else jnp.max(jnp.where(hit, payload, -1.0), axis=0, keepdims=True)
        dst = rank == r
        return jnp.where(hit, neg, s), jnp.where(dst, m, vals), jnp.where(dst, out, second)

    zeros = jnp.zeros((PEER_TOPK, tb), F32)
    _, vals, second = lax.fori_loop(0, PEER_TOPK, body, (s, zeros, zeros), unroll=True)
    return vals, second


def _peerq_kernel(h2_ref, wq_ref, k1_ref, k2_ref, sel_ref, sele_ref, idx_ref, g_ref, q_scr, it_scr, gt_scr):
    h = pl.program_id(1)
    tm = h2_ref.shape[0]

    @pl.when(h == 0)
    def _():
        q_scr[...] = _dot(h2_ref[...].astype(BF16), wq_ref[...]).astype(BF16)

    nt = (((1,), (1,)), ((), ()))
    vals, keys = [], []
    for tb in range(tm // LANES):
        qh = q_scr[tb * LANES:(tb + 1) * LANES, pl.ds(pl.multiple_of(h * LANES, LANES), LANES)]
        s12 = jnp.concatenate([lax.dot_general(k1_ref[...], qh, nt, preferred_element_type=F32),
                               lax.dot_general(k2_ref[...], qh, nt, preferred_element_type=F32)], axis=1)
        v12, i12 = _topk_cols(s12)
        vals.append(jnp.concatenate([v12[:, :LANES], v12[:, LANES:]], axis=0))
        keys.append(jnp.concatenate([i12[:, :LANES], i12[:, LANES:]], axis=0))
    vals = jnp.concatenate(vals, axis=1)
    keys = jnp.concatenate(keys, axis=1)
    cand = jnp.dot(sel_ref[...], vals, precision=HIGHEST, preferred_element_type=F32)
    cand_row = lax.broadcasted_iota(jnp.int32, cand.shape, 0)
    cand = jnp.where(cand_row < N_STAIR, cand, -jnp.inf)
    ecand = jnp.dot(sele_ref[...], keys, precision=HIGHEST, preferred_element_type=F32)
    sc, ex = _topk_cols(cand, ecand)
    p = jnp.exp(sc - sc[0:1, :])
    out_rows = pl.ds(pl.multiple_of(h * PEER_TOPK, PEER_TOPK), PEER_TOPK)
    it_scr[out_rows, :] = ex
    gt_scr[out_rows, :] = p / jnp.sum(p, axis=0, keepdims=True)

    @pl.when(h == PEER_HEADS - 1)
    def _():
        idx_ref[...] = it_scr[...].T.astype(jnp.int32)
        g_ref[...] = gt_scr[...].T


def peerq_call(h2, wq_bf, k1, k2, tm=256):
    N = h2.shape[0]
    half = N_KEYS // 2
    k1p = jnp.concatenate([k1, jnp.zeros((N_KEYS, half), F32)], axis=1).astype(BF16)
    k2p = jnp.concatenate([jnp.zeros((N_KEYS, half), F32), k2], axis=1).astype(BF16)
    sel = np.zeros((N_CAND_ROWS, 2 * PEER_TOPK), np.float32)
    sele = np.zeros((N_CAND_ROWS, 2 * PEER_TOPK), np.float32)
    for c, (a, b) in enumerate(_STAIR):
        sel[c, a] = sel[c, PEER_TOPK + b] = 1.0
        sele[c, a] = float(N_KEYS)
        sele[c, PEER_TOPK + b] = 1.0
    const = lambda shape: pl.BlockSpec(shape, lambda i, h: (0, 0))
    row = pl.BlockSpec((tm, HK), lambda i, h: (i, 0))
    return pl.pallas_call(
        _peerq_kernel, grid=(N // tm, PEER_HEADS),
        in_specs=[pl.BlockSpec((tm, D_MODEL), lambda i, h: (i, 0)), const(wq_bf.shape), const((N_KEYS, LANES)),
                  const((N_KEYS, LANES)), const(sel.shape), const(sele.shape)],
        out_specs=[row, row],
        out_shape=[jax.ShapeDtypeStruct((N, HK), jnp.int32), jax.ShapeDtypeStruct((N, HK), F32)],
        scratch_shapes=[pltpu.VMEM((tm, D_MODEL), BF16), pltpu.VMEM((HK, tm), F32), pltpu.VMEM((HK, tm), F32)],
        compiler_params=_cparams(("parallel", "arbitrary")), name="peer_retrieve",
    )(h2, wq_bf, k1p, k2p, jnp.asarray(sel), jnp.asarray(sele))


SC_LANES = 16
SC_CORES = 2
SC_WORKERS = 32
ROWS_PER_GATHER = 16
N_ROW_BUF = 4
LOOKAHEAD = N_ROW_BUF - 1
TOK_BLOCK = 16
ROW_BLOCK = 8
U_UNROLL = 2
COL_BLOCK = 16
N_DCHUNK = D_MODEL // SC_LANES
N_GATHER = HK // ROWS_PER_GATHER


def _sc_gelu(x):
    y = math.sqrt(2.0 / math.pi) * (x + 0.044715 * (x * x * x))
    t = 1.0 - 2.0 / (jnp.exp(2.0 * y) + 1.0)
    return 0.5 * x * (1.0 + t)


def _peer_eval_body(per_w, h_hbm, idx_hbm, g_hbm, u_hbm, v_hbm, o_hbm,
                    idx_blk, g_blk, h_blk, out_blk, w_v, *bufs):
    wid = lax.axis_index("s") * SC_CORES + lax.axis_index("c")
    rows = bufs[:N_ROW_BUF]
    sems = bufs[N_ROW_BUF:]
    lane = lax.iota(jnp.int32, SC_LANES)
    zeros = jnp.zeros((SC_LANES,), F32)

    def gather(table, i, j, buf):
        return pltpu.make_async_copy(table.at[idx_blk.at[i, j]], rows[buf], sems[buf])

    def u_compute(i, n, rbuf):
        for half in range(ROWS_PER_GATHER // SC_LANES):
            actv = zeros
            for rg in range(SC_LANES // ROW_BLOCK):
                r0 = half * SC_LANES + rg * ROW_BLOCK

                def c_body(cb, accs):
                    accs = list(accs)
                    for cc in range(U_UNROLL):
                        sl = pl.ds(pl.multiple_of((cb * U_UNROLL + cc) * SC_LANES, SC_LANES), SC_LANES)
                        hc = h_blk[i, sl]
                        for r in range(ROW_BLOCK):
                            accs[r] = accs[r] + rbuf[r0 + r, sl] * hc
                    return tuple(accs)

                accs = lax.fori_loop(0, N_DCHUNK // U_UNROLL, c_body, (zeros,) * ROW_BLOCK)
                for r in range(ROW_BLOCK):
                    actv = jnp.where(lane == rg * ROW_BLOCK + r, jnp.sum(accs[r]), actv)
            sl = pl.ds(n * ROWS_PER_GATHER + half * SC_LANES, SC_LANES)
            w_v[sl] = g_blk[i, sl] * _sc_gelu(actv)

    def v_compute(i, m, rbuf):
        for cb in range(N_DCHUNK // COL_BLOCK):
            def r_body(r, accs):
                wk = plsc.load_gather(w_v, [jnp.full((SC_LANES,), m * ROWS_PER_GATHER, jnp.int32) + r])
                return tuple(accs[cc] + wk * rbuf[r, pl.ds((cb * COL_BLOCK + cc) * SC_LANES, SC_LANES)]
                             for cc in range(COL_BLOCK))

            accs = lax.fori_loop(0, ROWS_PER_GATHER, r_body, (zeros,) * COL_BLOCK)
            for cc in range(COL_BLOCK):
                sl = pl.ds((cb * COL_BLOCK + cc) * SC_LANES, SC_LANES)
                if m == 0:
                    out_blk[i, sl] = accs[cc]
                else:
                    plsc.addupdate(out_blk.at[i, sl], accs[cc])

    @pl.loop(0, per_w // TOK_BLOCK)
    def _(bi):
        t0 = wid * per_w + bi * TOK_BLOCK
        pltpu.sync_copy(idx_hbm.at[pl.ds(t0, TOK_BLOCK)], idx_blk)
        pltpu.sync_copy(g_hbm.at[pl.ds(t0, TOK_BLOCK)], g_blk)
        pltpu.sync_copy(h_hbm.at[pl.ds(t0, TOK_BLOCK)], h_blk)
        def task_gather(i, n):
            table = u_hbm if n < N_GATHER else v_hbm
            return gather(table, i, n % N_GATHER, n % N_ROW_BUF)

        for n in range(LOOKAHEAD):
            task_gather(0, n).start()

        @pl.loop(0, TOK_BLOCK)
        def _(i):
            for n in range(2 * N_GATHER):
                ahead = n + LOOKAHEAD
                if ahead < 2 * N_GATHER:
                    task_gather(i, ahead).start()
                else:
                    @pl.when(i + 1 < TOK_BLOCK)
                    def _():
                        task_gather(i + 1, ahead - 2 * N_GATHER).start()
                task_gather(i, n).wait()
                if n < N_GATHER:
                    u_compute(i, n, rows[n % N_ROW_BUF])
                else:
                    v_compute(i, n - N_GATHER, rows[n % N_ROW_BUF])

        pltpu.sync_copy(out_blk, o_hbm.at[pl.ds(t0, TOK_BLOCK)])


def peer_eval_call(h2, idx, g, u_tab, v_tab, n_tok=None, table_row0=0):
    N = h2.shape[0] if n_tok is None else n_tok
    per_w = N // SC_WORKERS
    assert N % (SC_WORKERS * TOK_BLOCK) == 0
    assert (2 * N_GATHER) % N_ROW_BUF == 0 and LOOKAHEAD < N_ROW_BUF
    mesh = plsc.VectorSubcoreMesh(core_axis_name="c", subcore_axis_name="s",
                                  num_cores=SC_CORES, num_subcores=SC_WORKERS // SC_CORES)
    return pl.kernel(
        functools.partial(_peer_eval_body, per_w),
        out_type=jax.ShapeDtypeStruct((N, D_MODEL), F32),
        mesh=mesh,
        scratch_types=[
            pltpu.VMEM((TOK_BLOCK, N_GATHER, ROWS_PER_GATHER), jnp.int32),
            pltpu.VMEM((TOK_BLOCK, HK), F32),
            pltpu.VMEM((TOK_BLOCK, D_MODEL), F32),
            pltpu.VMEM((TOK_BLOCK, D_MODEL), F32),
            pltpu.VMEM((HK,), F32),
        ] + [pltpu.VMEM((ROWS_PER_GATHER, D_MODEL), F32)] * N_ROW_BUF + [pltpu.SemaphoreType.DMA] * N_ROW_BUF,
        compiler_params=pltpu.CompilerParams(needs_layout_passes=False),
        name="peer_eval",
    )(h2, (idx + table_row0).reshape(h2.shape[0], N_GATHER, ROWS_PER_GATHER), g, u_tab, v_tab)


N_EXPERTS = N_KEYS * N_KEYS
E_HALF = N_EXPERTS // 2
TC_TOK_TILE = 16
TC_EVAL_SHARE = ((0.1875, 0.1875, 0.1875, 0.25), (0.25, 0.25, 0.3125, 0.3125))
COMPILER_SCRATCH_BYTES = 2 * 1024 * 1024


def _tc_eval_kernel(table_row0, idx_s, hi_ref, g_ref, h_ref, u_hbm, v_hbm, o_ref,
                    tab, actw, outp, rsum, wb_all, stage, sem):
    p = pl.program_id(0)
    i = pl.program_id(1)
    for ph, (src, half) in enumerate(((u_hbm, 0), (u_hbm, 1), (v_hbm, 0), (v_hbm, 1))):
        @pl.when(jnp.logical_and(p == ph, i == 0))
        def _():
            cp = pltpu.make_async_copy(src.at[pl.ds(table_row0 + half * E_HALF, E_HALF)], tab, sem)
            cp.start()
            cp.wait()

    lane = lax.broadcasted_iota(jnp.int32, (HK, LANES), 1)
    row = lax.broadcasted_iota(jnp.int32, (HK, LANES), 0)
    in_half = hi_ref[0] == (p % 2)

    @pl.when(p < 2)
    def _():
        acts = jnp.zeros((HK, LANES), F32)
        for t in range(TC_TOK_TILE):
            h = h_ref[t]
            for k in range(HK):
                rsum[t, pl.ds(k, 1), :] = jnp.sum(tab[idx_s[t, k]] * h, axis=0, keepdims=True)
            acts = jnp.where(lane == t, jnp.sum(rsum[t], axis=1, keepdims=True), acts)
        part = jnp.where(in_half, acts, 0.0)

        @pl.when(p == 0)
        def _():
            actw[i] = part

        @pl.when(p == 1)
        def _():
            actw[i] = g_ref[0] * jax.nn.gelu(actw[i] + part)

    @pl.when(p >= 2)
    def _():
        wm = jnp.where(in_half, actw[i], 0.0)

        for t in range(TC_TOK_TILE):
            wb_all[t] = jnp.broadcast_to(wm[:, t:t + 1], (HK, LANES))

        for t in range(TC_TOK_TILE):
            accs = [jnp.zeros((SUBLANES, LANES), F32)] * 4
            for k in range(HK):
                w = jnp.broadcast_to(wb_all[t, pl.ds(k, 1), :], (SUBLANES, LANES))
                accs[k % 4] = accs[k % 4] + w * tab[idx_s[t, k]]
            stage[t] = (accs[0] + accs[1]) + (accs[2] + accs[3])
        tile_toks = pl.ds(pl.multiple_of(i * TC_TOK_TILE, TC_TOK_TILE), TC_TOK_TILE)

        @pl.when(p == 2)
        def _():
            outp[tile_toks] = stage[...]

        @pl.when(p == 3)
        def _():
            o_ref[...] = outp[tile_toks] + stage[...]


def peer_eval_tc_call(h2, idx, g, u_tab, v_tab, first_tok=0, table_row0=0):
    n_all = h2.shape[0]
    nt = n_all - first_tok
    n_tiles = nt // TC_TOK_TILE
    first_tile = first_tok // TC_TOK_TILE
    n_rows = u_tab.shape[0]
    assert nt % TC_TOK_TILE == 0 and first_tok % TC_TOK_TILE == 0 and table_row0 + N_EXPERTS <= n_rows
    idx, g = idx[first_tok:], g[first_tok:]

    def pairs_on_sublanes(a, fill):
        a = a.reshape(n_tiles, TC_TOK_TILE, HK).transpose(0, 2, 1)
        return jnp.pad(a, ((0, 0), (0, 0), (0, LANES - TC_TOK_TILE)), constant_values=fill)

    tile3 = lambda: pl.BlockSpec((1, HK, LANES), lambda p, i: (i, 0, 0))
    scratch_shapes = [(E_HALF, SUBLANES, LANES), (n_tiles, HK, LANES), (nt, SUBLANES, LANES),
                      (TC_TOK_TILE, HK, LANES), (TC_TOK_TILE, HK, LANES), (TC_TOK_TILE, SUBLANES, LANES)]
    block_shapes = [(HK, LANES), (HK, LANES), (TC_TOK_TILE, SUBLANES, LANES), (TC_TOK_TILE, SUBLANES, LANES)]
    f32_bytes = 4
    vmem_bytes = (sum(int(np.prod(s)) for s in scratch_shapes) + 2 * sum(int(np.prod(s)) for s in block_shapes)
                  ) * f32_bytes + COMPILER_SCRATCH_BYTES
    assert vmem_bytes <= V7X_VMEM_BYTES
    out = pl.pallas_call(
        functools.partial(_tc_eval_kernel, table_row0), grid=(4, n_tiles),
        in_specs=[pl.BlockSpec((TC_TOK_TILE, HK), lambda p, i: (i, 0), memory_space=pltpu.SMEM),
                  tile3(), tile3(),
                  pl.BlockSpec((TC_TOK_TILE, SUBLANES, LANES), lambda p, i: (first_tile + i, 0, 0)),
                  pl.BlockSpec(memory_space=pl.ANY), pl.BlockSpec(memory_space=pl.ANY)],
        out_specs=pl.BlockSpec((TC_TOK_TILE, SUBLANES, LANES), lambda p, i: (jnp.where(p == 3, i, 0), 0, 0)),
        out_shape=jax.ShapeDtypeStruct((nt, SUBLANES, LANES), F32),
        scratch_shapes=[pltpu.VMEM(s, F32) for s in scratch_shapes] + [pltpu.SemaphoreType.DMA],
        compiler_params=pltpu.CompilerParams(dimension_semantics=("arbitrary", "arbitrary"),
                                             vmem_limit_bytes=vmem_bytes),
        name="peer_eval_tc",
    )(idx & (E_HALF - 1), pairs_on_sublanes(idx // E_HALF, 2), pairs_on_sublanes(g, 0.0),
      h2.reshape(n_all, SUBLANES, LANES),
      u_tab.reshape(n_rows, SUBLANES, LANES), v_tab.reshape(n_rows, SUBLANES, LANES))
    return out.reshape(nt, D_MODEL)


def _final_kernel(x_ref, y_ref, gt_ref, g_ref, o_ref):
    x = x_ref[...] + gt_ref[0] * y_ref[...]
    ms = jnp.mean(x * x, axis=-1, keepdims=True)
    o_ref[...] = x * lax.rsqrt(ms + EPS) * g_ref[...]


def final_call(x, y, mod, gf, T, tm=512):
    N = x.shape[0]
    row = pl.BlockSpec((tm, D_MODEL), lambda i: (i, 0))
    return pl.pallas_call(
        _final_kernel, grid=(N // tm,),
        in_specs=[row, row, _mod_spec(tm, T, 5), pl.BlockSpec((1, D_MODEL), lambda i: (0, 0))],
        out_specs=row, out_shape=jax.ShapeDtypeStruct((N, D_MODEL), F32),
        compiler_params=_cparams(("parallel",)), name="final_norm",
    )(x, y, mod, gf)


def kernel(x, c, ada_w, ada_b, norm1_g, norm2_g, w_in, sgu_w, sgu_b, out_norm_a, out_norm_b, w_out,
           peer_wq, peer_k1, peer_k2, peer_u, peer_v, final_g):
    B, T, Dm = x.shape
    L = ada_w.shape[0]
    N = B * T
    mods = ada_mod(c, ada_w, ada_b).reshape(L, B * 6, 1, Dm)
    assert sum(GROUP_BATCHES) == B and len(TC_EVAL_SHARE) == L
    assert all(len(s) == len(GROUP_BATCHES) for s in TC_EVAL_SHARE)
    n_groups = len(GROUP_BATCHES)
    first_batch = [sum(GROUP_BATCHES[:g]) for g in range(n_groups)]
    gmods = [[mods[l, first_batch[g] * 6:(first_batch[g] + GROUP_BATCHES[g]) * 6] for g in range(n_groups)]
             for l in range(L)]
    xs = [x.reshape(N, Dm)] * n_groups
    ys = [None] * n_groups
    n_exp = peer_u.shape[1]
    u_all = peer_u.reshape(L * n_exp, Dm)
    v_all = peer_v.reshape(L * n_exp, Dm)
    sc_step = SC_WORKERS * TOK_BLOCK
    for l in range(L):
        w_in_bf, w_out_bf, wq_bf = w_in[l].astype(BF16), w_out[l].astype(BF16), peer_wq[l].astype(BF16)
        bias_full = jnp.repeat(sgu_b[l].T, GROUP, axis=1)
        for g in range(n_groups):
            Bg = GROUP_BATCHES[g]
            Ng = Bg * T
            mod = gmods[l][g]
            x_first = first_batch[g] * T if l == 0 else 0
            outs = proj_call(xs[g], ys[g], gmods[l - 1][g] if l else None, mod, norm1_g[l].reshape(1, Dm), w_in_bf, T,
                             n_tok=Ng, x_first_tok=x_first)
            if l:
                xs[g], outs = outs[0], outs[1:]
            ua, va, q, k, v = outs
            ya = gmlp_call(ua, va, sgu_w[l], bias_full, out_norm_a[l].reshape(1, D_A))
            yb = attn_call(q, k, v, out_norm_b[l].reshape(1, D_B), Bg, T)
            xs[g], h2 = out_call(ya, yb, xs[g], w_out_bf, mod, norm2_g[l].reshape(1, Dm), T, x_first_tok=x_first)
            idx, gate = peerq_call(h2, wq_bf, peer_k1[l], peer_k2[l])
            n_sc = int(Ng * (1.0 - TC_EVAL_SHARE[l][g])) // sc_step * sc_step
            y_sc = peer_eval_call(h2, idx, gate, u_all, v_all, n_tok=n_sc, table_row0=l * n_exp)
            y_tc = peer_eval_tc_call(h2, idx, gate, u_all, v_all, first_tok=n_sc, table_row0=l * n_exp)
            ys[g] = jnp.concatenate([y_sc, y_tc], axis=0)
    outs = [final_call(xs[g], ys[g], gmods[L - 1][g], final_g.reshape(1, Dm), T) for g in range(n_groups)]
    return jnp.concatenate(outs, axis=0).reshape(B, T, Dm)
```
